```python
import jax
import jax.numpy as jnp
from jax import lax
import numpy as np

D_MODEL = 2048
BATCH = 8
SEQ = 2048
DEPTH = 1

D_MIX = D_MODEL
RWKV_HEAD = 64
RWKV_WIDTH = D_MIX // 2
RWKV_HEADS = RWKV_WIDTH // RWKV_HEAD
DECAY_LORA = 64
AAA_LORA = 64
GATE_LORA = 160
GN_EPS = 64e-5
QK_NOPE = 128
QK_ROPE = 64
V_HEAD = 128
MLA_WIDTH = D_MIX - RWKV_WIDTH
MLA_HEADS = MLA_WIDTH // V_HEAD
Q_LORA = 512
KV_LORA = 512
ROPE_THETA = 10000.0
Q_BLOCK = 128
D_FF = 5632
CONV_W = 3
NORM_EPS = 1e-6
NEG_INF = -1e30

IN_SPLITS = (RWKV_WIDTH, RWKV_WIDTH, RWKV_WIDTH, DECAY_LORA, AAA_LORA, GATE_LORA,
             Q_LORA, KV_LORA, QK_ROPE)
D_IN = sum(IN_SPLITS)
RWKV_SHIFT_DIM = 3 * RWKV_WIDTH + DECAY_LORA + AAA_LORA + GATE_LORA

kernel_name = 'hybrid_rwkv7_mla_convglu'


def _split(t, sizes):
    idx = np.cumsum(sizes)[:-1].tolist()
    return jnp.split(t, idx, axis=-1)


def rms_norm(x, g):
    xf = x.astype(jnp.float32)
    y = xf * lax.rsqrt(jnp.mean(xf * xf, axis=-1, keepdims=True) + NORM_EPS)
    return (y * g.astype(jnp.float32)).astype(x.dtype)


def token_shift(h, mu):
    prev = jnp.pad(h, ((0, 0), (1, 0), (0, 0)))[:, :-1]
    return h + (prev - h) * mu


def apply_rope(t, positions):
    half = t.shape[-1] // 2
    inv_freq = ROPE_THETA ** (-jnp.arange(half, dtype=jnp.float32) / half)
    ang = positions.astype(jnp.float32)[..., None] * inv_freq
    ang = ang.reshape(ang.shape[:2] + (1,) * (t.ndim - 3) + (half,))
    cos, sin = jnp.cos(ang), jnp.sin(ang)
    tf = t.astype(jnp.float32)
    t1, t2 = tf[..., :half], tf[..., half:]
    return jnp.concatenate([t1 * cos - t2 * sin, t2 * cos + t1 * sin], axis=-1).astype(t.dtype)


def rwkv7_scan(r, w, k, v, a, b):
    bsz, _, h, n = r.shape

    def step(state, inp):
        r_t, w_t, k_t, v_t, a_t, b_t = inp
        sa = jnp.einsum('bhvk,bhk->bhv', state, a_t)
        state = (state * w_t[:, :, None, :]
                 + sa[..., None] * b_t[:, :, None, :]
                 + v_t[..., None] * k_t[:, :, None, :])
        return state, jnp.einsum('bhvk,bhk->bhv', state, r_t)

    init = jnp.zeros((bsz, h, n, n), jnp.float32)
    xs = tuple(jnp.moveaxis(t, 1, 0) for t in (r, w, k, v, a, b))
    _, out = lax.scan(step, init, xs)
    return jnp.moveaxis(out, 0, 1)


def rwkv7_mixer(h_r, h_k, h_v, h_w, h_a, h_g, w0, w2, a0, a2, g2, k_k, k_a, r_k, gn_w, gn_b):
    f32 = jnp.float32
    bsz, s, c = h_r.shape
    hd = (bsz, s, RWKV_HEADS, RWKV_HEAD)
    r = h_r.astype(f32)
    k = h_k.astype(f32)
    v = h_v.astype(f32)
    w_log = -jax.nn.softplus(-(w0.astype(f32) + jnp.tanh(h_w.astype(f32)) @ w2.astype(f32))) - 0.5
    decay = jnp.exp(-jnp.exp(w_log))
    a = jax.nn.sigmoid(a0.astype(f32) + h_a.astype(f32) @ a2.astype(f32))
    g = jax.nn.sigmoid(h_g.astype(f32)) @ g2.astype(f32)
    kk = (k * k_k).reshape(hd)
    kk = kk * lax.rsqrt(jnp.maximum(jnp.sum(kk * kk, axis=-1, keepdims=True), 1e-24))
    k = k * (1.0 + (a - 1.0) * k_a)
    r, k, v, decay, a = (t.reshape(hd) for t in (r, k, v, decay, a))
    y = rwkv7_scan(r, decay, k, v, -kk, kk * a)
    mu = jnp.mean(y, axis=-1, keepdims=True)
    var = jnp.mean(jnp.square(y - mu), axis=-1, keepdims=True)
    y = ((y - mu) * lax.rsqrt(var + GN_EPS)).reshape(bsz, s, c) * gn_w + gn_b
    bonus = jnp.sum(r * k * r_k, axis=-1, keepdims=True) * v
    return (y + bonus.reshape(bsz, s, c)) * g


def mla_mixer(c_q, c_kv, k_pe, positions, q_norm_g, w_uq, kv_norm_g, w_ukv):
    f32 = jnp.float32
    bsz, s, _ = c_q.shape
    q = (rms_norm(c_q, q_norm_g) @ w_uq).reshape(bsz, s, MLA_HEADS, QK_NOPE + QK_ROPE)
    q_nope = q[..., :QK_NOPE]
    q_pe = apply_rope(q[..., QK_NOPE:], positions)
    kv = (rms_norm(c_kv, kv_norm_g) @ w_ukv).reshape(bsz, s, MLA_HEADS, QK_NOPE + V_HEAD)
    k_nope, v = kv[..., :QK_NOPE], kv[..., QK_NOPE:]
    k_pe = apply_rope(k_pe, positions)
    scale = (QK_NOPE + QK_ROPE) ** -0.5
    outs = []
    for i in range(s // Q_BLOCK):
        q0, q1 = i * Q_BLOCK, (i + 1) * Q_BLOCK
        sc = (jnp.einsum('bqhd,bkhd->bhqk', q_nope[:, q0:q1], k_nope[:, :q1])
              + jnp.einsum('bqhr,bkr->bhqk', q_pe[:, q0:q1], k_pe[:, :q1])).astype(f32) * scale
        causal = (q0 + jnp.arange(Q_BLOCK))[:, None] >= jnp.arange(q1)[None, :]
        p = jax.nn.softmax(jnp.where(causal, sc, NEG_INF), axis=-1)
        outs.append(jnp.einsum('bhqk,bkhd->bqhd', p.astype(v.dtype), v[:, :q1]))
    return jnp.concatenate(outs, axis=1).reshape(bsz, s, MLA_WIDTH)


def conv_glu_ffn(h, w_gate, w_up, conv_w, conv_b, w_down):
    gate = h @ w_gate
    gate = lax.conv_general_dilated(
        gate, conv_w[:, None, :].astype(gate.dtype), window_strides=(1,),
        padding=[(CONV_W - 1, 0)], dimension_numbers=('NWC', 'WIO', 'NWC'),
        feature_group_count=D_FF) + conv_b
    return (jax.nn.silu(gate) * (h @ w_up)) @ w_down


def setup_inputs(seed: int = 0) -> dict:
    key = jax.random.key(seed)
    ks = jax.random.split(key, 32)
    L = DEPTH

    def nrm(k, shape, scale):
        return jax.random.normal(k, shape, jnp.float32) * scale

    def gain(k, shape):
        return 1.0 + nrm(k, shape, 0.05)

    x = nrm(ks[0], (BATCH, SEQ, D_MODEL), 1.0)
    offset = jax.random.randint(ks[1], (BATCH, 1), 0, 4096, dtype=jnp.int32)
    positions = offset + jnp.arange(SEQ, dtype=jnp.int32)[None, :]
    return {
        'x': x,
        'positions': positions,
        'attn_norm_g': gain(ks[2], (L, D_MODEL)),
        'w_in': nrm(ks[3], (L, D_MODEL, D_IN), D_MODEL ** -0.5),
        'rwkv_mu': jax.random.uniform(ks[4], (L, RWKV_SHIFT_DIM), jnp.float32),
        'rwkv_w0': jax.random.uniform(ks[5], (L, RWKV_WIDTH), jnp.float32, -6.0, -1.0),
        'rwkv_w2': nrm(ks[6], (L, DECAY_LORA, RWKV_WIDTH), 0.1 * DECAY_LORA ** -0.5),
        'rwkv_a0': nrm(ks[7], (L, RWKV_WIDTH), 0.5),
        'rwkv_a2': nrm(ks[8], (L, AAA_LORA, RWKV_WIDTH), 0.5 * AAA_LORA ** -0.5),
        'rwkv_g2': nrm(ks[9], (L, GATE_LORA, RWKV_WIDTH), GATE_LORA ** -0.5),
        'rwkv_k_k': 0.85 + nrm(ks[10], (L, RWKV_WIDTH), 0.05),
        'rwkv_k_a': gain(ks[11], (L, RWKV_WIDTH)),
        'rwkv_r_k': nrm(ks[12], (L, RWKV_HEADS, RWKV_HEAD), 0.1),
        'rwkv_gn_w': gain(ks[13], (L, RWKV_WIDTH)),
        'rwkv_gn_b': nrm(ks[14], (L, RWKV_WIDTH), 0.02),
        'mla_q_norm_g': gain(ks[15], (L, Q_LORA)),
        'mla_w_uq': nrm(ks[16], (L, Q_LORA, MLA_HEADS * (QK_NOPE + QK_ROPE)), Q_LORA ** -0.5),
        'mla_kv_norm_g': gain(ks[17], (L, KV_LORA)),
        'mla_w_ukv': nrm(ks[18], (L, KV_LORA, MLA_HEADS * (QK_NOPE + V_HEAD)), KV_LORA ** -0.5),
        'w_out': nrm(ks[19], (L, D_MIX, D_MODEL), D_MIX ** -0.5),
        'ffn_norm_g': gain(ks[20], (L, D_MODEL)),
        'ffn_w_gate': nrm(ks[21], (L, D_MODEL, D_FF), D_MODEL ** -0.5),
        'ffn_w_up': nrm(ks[22], (L, D_MODEL, D_FF), D_MODEL ** -0.5),
        'ffn_conv_w': nrm(ks[23], (L, CONV_W, D_FF), CONV_W ** -0.5),
        'ffn_conv_b': nrm(ks[24], (L, D_FF), 0.02),
        'ffn_w_down': nrm(ks[25], (L, D_FF, D_MODEL), D_FF ** -0.5),
        'final_norm_g': gain(ks[26], (D_MODEL,)),
    }


def reference(x, positions, attn_norm_g, w_in, rwkv_mu, rwkv_w0, rwkv_w2, rwkv_a0, rwkv_a2,
              rwkv_g2, rwkv_k_k, rwkv_k_a, rwkv_r_k, rwkv_gn_w, rwkv_gn_b, mla_q_norm_g,
              mla_w_uq, mla_kv_norm_g, mla_w_ukv, w_out, ffn_norm_g, ffn_w_gate, ffn_w_up,
              ffn_conv_w, ffn_conv_b, ffn_w_down, final_norm_g):
    for l in range(DEPTH):
        h = rms_norm(x, attn_norm_g[l])
        proj = h @ w_in[l]
        shifted = token_shift(proj[..., :RWKV_SHIFT_DIM], rwkv_mu[l])
        h_r, h_k, h_v, h_w, h_a, h_g = _split(shifted, IN_SPLITS[:6])
        c_q, c_kv, k_pe = _split(proj[..., RWKV_SHIFT_DIM:], IN_SPLITS[6:])
        y_rwkv = rwkv7_mixer(h_r, h_k, h_v, h_w, h_a, h_g, rwkv_w0[l], rwkv_w2[l], rwkv_a0[l],
                             rwkv_a2[l], rwkv_g2[l], rwkv_k_k[l], rwkv_k_a[l], rwkv_r_k[l],
                             rwkv_gn_w[l], rwkv_gn_b[l])
        y_mla = mla_mixer(c_q, c_kv, k_pe, positions, mla_q_norm_g[l], mla_w_uq[l],
                          mla_kv_norm_g[l], mla_w_ukv[l])
        y = jnp.concatenate([y_rwkv.astype(x.dtype), y_mla.astype(x.dtype)], axis=-1)
        x = x + y @ w_out[l]
        h = rms_norm(x, ffn_norm_g[l])
        x = x + conv_glu_ffn(h, ffn_w_gate[l], ffn_w_up[l], ffn_conv_w[l], ffn_conv_b[l],
                             ffn_w_down[l]).astype(x.dtype)
    return rms_norm(x, final_norm_g)
```

```python
import functools
import math

import jax
import jax.numpy as jnp
from jax import lax
from jax.experimental import pallas as pl
from jax.experimental.pallas import tpu as pltpu

F32 = jnp.float32
BF16 = jnp.bfloat16

D_MODEL = 2048
RWKV_HEAD = 64
RWKV_WIDTH = 1024
RWKV_HEADS = 16
DECAY_LORA = 64
AAA_LORA = 64
GATE_LORA = 160
GN_EPS = 64e-5
QK_NOPE = 128
QK_ROPE = 64
V_HEAD = 128
MLA_WIDTH = 1024
MLA_HEADS = 8
Q_LORA = 512
KV_LORA = 512
ROPE_THETA = 10000.0
D_FF = 5632
NORM_EPS = 1e-6
NEG_INF = -1e30

LANES = 128
SMALL = 512
D_IN_P = 3 * RWKV_WIDTH + Q_LORA + KV_LORA + SMALL
COL_CQ = 3 * RWKV_WIDTH // Q_LORA
COL_CKV = COL_CQ + 1
COL_SMALL = COL_CKV + 1
CHUNK = 64
PAIRS = RWKV_WIDTH // LANES
VMEM_LIMIT = 56 * 1024 * 1024


def _cparams(sem):
    return pltpu.CompilerParams(dimension_semantics=sem, vmem_limit_bytes=VMEM_LIMIT)


def _rms(x, g):
    ms = jnp.mean(x * x, axis=-1, keepdims=True)
    return x * lax.rsqrt(ms + NORM_EPS) * g


def _sigmoid(x):
    return 1.0 / (1.0 + jnp.exp(-x))


def _dot(a, b):
    return jnp.dot(a, b, preferred_element_type=F32)


def _dot_nt(a, b):
    return lax.dot_general(a, b, (((1,), (1,)), ((), ())), preferred_element_type=F32)


def _dot_tn(a, b):
    return lax.dot_general(a, b, (((0,), (0,)), ((), ())), preferred_element_type=F32)


def _inproj_kernel(x_ref, g_ref, w_ref, o_ref, h_scr):
    @pl.when(pl.program_id(1) == 0)
    def _():
        h_scr[...] = _rms(x_ref[...], g_ref[...]).astype(BF16)

    o_ref[...] = _dot(h_scr[...], w_ref[...])


def _inproj(x2, g, w_p, tm=512, tn=512):
    t, d = x2.shape
    n = w_p.shape[1]
    return pl.pallas_call(
        _inproj_kernel,
        grid=(t // tm, n // tn),
        in_specs=[
            pl.BlockSpec((tm, d), lambda i, j: (i, 0)),
            pl.BlockSpec((1, d), lambda i, j: (0, 0)),
            pl.BlockSpec((d, tn), lambda i, j: (0, j)),
        ],
        out_specs=pl.BlockSpec((tm, tn), lambda i, j: (i, j)),
        out_shape=jax.ShapeDtypeStruct((t, n), F32),
        scratch_shapes=[pltpu.VMEM((tm, d), BF16)],
        compiler_params=_cparams(("parallel", "arbitrary")),
        name="inproj",
    )(x2, g, w_p)


def _rwkv_kernel(r_ref, k_ref, v_ref, sm_ref, mur_ref, muk_ref, muv_ref, musm_ref,
                 w0_ref, a0_ref, kk_ref, ka_ref, rk_ref, gnw_ref, gnb_ref,
                 w2_ref, a2_ref, g2_ref, o_ref,
                 st_scr, pr_scr, pk_scr, pv_scr, psm_scr):
    c = CHUNK
    half = RWKV_HEAD

    @pl.when(pl.program_id(1) == 0)
    def _():
        st_scr[...] = jnp.zeros_like(st_scr)
        pr_scr[...] = jnp.zeros_like(pr_scr)
        pk_scr[...] = jnp.zeros_like(pk_scr)
        pv_scr[...] = jnp.zeros_like(pv_scr)
        psm_scr[...] = jnp.zeros_like(psm_scr)

    def shift(x_ref, p_scr, mu_ref):
        x = x_ref[...]
        row = lax.broadcasted_iota(jnp.int32, x.shape, 0)
        xp = jnp.where(row == 0, p_scr[...], pltpu.roll(x, 1, 0))
        p_scr[...] = x_ref[c - 1:c, :]
        return x + (xp - x) * mu_ref[...]

    rs = shift(r_ref, pr_scr, mur_ref)
    ks = shift(k_ref, pk_scr, muk_ref)
    vs = shift(v_ref, pv_scr, muv_ref)
    sm = shift(sm_ref, psm_scr, musm_ref)

    ri = lax.broadcasted_iota(jnp.int32, (c, LANES), 0)
    li = lax.broadcasted_iota(jnp.int32, (c, LANES), 1)
    lj = li & (half - 1)
    lane_a = li < half
    strict = lj < ri
    incl = lj <= ri
    eye2 = (lj == ri).astype(F32)
    r2 = lax.broadcasted_iota(jnp.int32, (LANES, LANES), 0)
    l2 = lax.broadcasted_iota(jnp.int32, (LANES, LANES), 1)
    bd = (r2 < half) == (l2 < half)
    bd_ones = bd.astype(BF16)
    tr = lax.broadcasted_iota(jnp.int32, (c, c), 0)
    tc = lax.broadcasted_iota(jnp.int32, (c, c), 1)
    tril = (tc <= tr).astype(BF16)

    def bd_stack(x):
        xb = x.astype(BF16)
        z = jnp.zeros_like(xb)
        return jnp.concatenate([jnp.where(lane_a, xb, z), jnp.where(lane_a, z, xb)], axis=0)

    def segsum(x):
        xs = jnp.concatenate([x[:, LANES * g:LANES * (g + 1)] for g in range(PAIRS)], axis=0)
        hi = xs.astype(BF16)
        lo = (xs - hi.astype(F32)).astype(BF16)
        s = _dot(hi, bd_ones) + _dot(lo, bd_ones)
        return jnp.concatenate([s[c * g:c * (g + 1), :] for g in range(PAIRS)], axis=1)

    def cumsum_rows(x):
        hi = x.astype(BF16)
        r1 = x - hi.astype(F32)
        mid = r1.astype(BF16)
        lo = (r1 - mid.astype(F32)).astype(BF16)
        return _dot(tril, hi) + _dot(tril, mid) + _dot(tril, lo)

    g0 = sm[:, 0:LANES]
    g12 = sm[:, LANES:3 * LANES]
    zw = _dot(jnp.tanh(g0).astype(BF16), w2_ref[...])
    za = _dot(g0.astype(BF16), a2_ref[...])
    gate = _dot(_sigmoid(g12).astype(BF16), g2_ref[...])

    ld = (-math.exp(-0.5)) * _sigmoid(w0_ref[...] + zw)
    a = _sigmoid(a0_ref[...] + za)
    kkr = ks * kk_ref[...]
    kk = kkr * lax.rsqrt(jnp.maximum(segsum(kkr * kkr), 1e-24))
    kp = ks * (1.0 + (a - 1.0) * ka_ref[...])
    beta = kk * a
    bonus = segsum(rs * kp * rk_ref[...])

    lg = cumsum_rows(ld)
    lgc = lg[c - 1:c, :]
    einv = jnp.exp(-lg)
    ec = jnp.exp(lgc - lg)
    gc = jnp.exp(lgc)
    rt = rs * jnp.exp(lg)
    at = -kk * jnp.exp(lg - ld)
    bt = beta * einv
    kt = kp * einv
    bh = beta * ec
    kh = kp * ec

    ys = []
    for p in range(PAIRS):
        sl = slice(LANES * p, LANES * (p + 1))
        v_p = vs[:, sl]
        lhs1 = jnp.concatenate([at[:, sl], rt[:, sl]], axis=0).astype(BF16)
        rhs1 = jnp.concatenate([bd_stack(bt[:, sl]), bd_stack(kt[:, sl])], axis=0)
        pm = _dot_nt(lhs1, rhs1)
        lab = jnp.where(strict, pm[0:c, 0:LANES], 0.0)
        lak = jnp.where(strict, pm[0:c, LANES:2 * LANES], 0.0)
        mrb = jnp.where(incl, pm[c:2 * c, 0:LANES], 0.0)
        mrk = jnp.where(incl, pm[c:2 * c, LANES:2 * LANES], 0.0)

        tinv = eye2 + lab
        pw = lab
        for _ in range(5):
            pw = _dot(pw.astype(BF16), bd_stack(pw))
            tinv = tinv + _dot(tinv.astype(BF16), bd_stack(pw))

        s0 = st_scr[p]
        xh = _dot_nt(lhs1, s0.astype(BF16))
        lm = jnp.concatenate([lak, mrk], axis=0).astype(BF16)
        lv = _dot(lm, bd_stack(v_p))
        x = xh[0:c] + lv[0:c]
        u = _dot(tinv.astype(BF16), bd_stack(x))
        y = xh[c:2 * c] + lv[c:2 * c] + _dot(mrb.astype(BF16), bd_stack(u))
        ys.append(y)

        uv = jnp.concatenate([u, v_p], axis=0).astype(BF16)
        bk = jnp.concatenate([bh[:, sl], kh[:, sl]], axis=0).astype(BF16)
        ds = _dot_tn(uv, bk)
        st_scr[p] = s0 * gc[:, sl] + jnp.where(bd, ds, 0.0)

    y = jnp.concatenate(ys, axis=1)
    inv_n = 1.0 / RWKV_HEAD
    mean = segsum(y) * inv_n
    d = y - mean
    var = segsum(d * d) * inv_n
    yn = d * lax.rsqrt(var + GN_EPS) * gnw_ref[...] + gnb_ref[...]
    o_ref[...] = ((yn + bonus * vs) * gate).astype(o_ref.dtype)


def _rwkv(proj, bsz, seq, mu_r, mu_k, mu_v, mu_sm, w0, a0, k_k, k_a, r_k, gn_w, gn_b, w2p, a2p, g2p):
    t = proj.shape[0]
    w = RWKV_WIDTH
    nc = seq // CHUNK
    row = lambda b, c: b * nc + c
    vec = lambda n: pl.BlockSpec((1, n), lambda b, c: (0, 0))
    full = lambda a: pl.BlockSpec(a.shape, lambda b, c: (0, 0))
    return pl.pallas_call(
        _rwkv_kernel,
        grid=(bsz, nc),
        in_specs=[
            pl.BlockSpec((CHUNK, w), lambda b, c: (row(b, c), 0)),
            pl.BlockSpec((CHUNK, w), lambda b, c: (row(b, c), 1)),
            pl.BlockSpec((CHUNK, w), lambda b, c: (row(b, c), 2)),
            pl.BlockSpec((CHUNK, SMALL), lambda b, c: (row(b, c), COL_SMALL)),
            vec(w), vec(w), vec(w), vec(SMALL),
            vec(w), vec(w), vec(w), vec(w), vec(w), vec(w), vec(w),
            full(w2p), full(a2p), full(g2p),
        ],
        out_specs=pl.BlockSpec((CHUNK, w), lambda b, c: (row(b, c), 0)),
        out_shape=jax.ShapeDtypeStruct((t, w), BF16),
        scratch_shapes=[
            pltpu.VMEM((PAIRS, LANES, LANES), F32),
            pltpu.VMEM((1, w), F32), pltpu.VMEM((1, w), F32), pltpu.VMEM((1, w), F32),
            pltpu.VMEM((1, SMALL), F32),
        ],
        compiler_params=_cparams(("parallel", "arbitrary")),
        name="rwkv",
    )(proj, proj, proj, proj, mu_r, mu_k, mu_v, mu_sm, w0, a0, k_k, k_a, r_k, gn_w, gn_b,
      w2p, a2p, g2p)


def _rope_tables(pos_ref, invf_ref):
    ang = pos_ref[...].astype(F32) * invf_ref[...]
    keep = lax.broadcasted_iota(jnp.int32, ang.shape, 1) < QK_ROPE
    return jnp.where(keep, jnp.cos(ang), 0.0), jnp.where(keep, jnp.sin(ang), 0.0)


def _mla_q_kernel(cq_ref, pos_ref, invf_ref, g_ref, w1_ref, w2_ref, q_ref):
    h = _rms(cq_ref[...], g_ref[...]).astype(BF16)
    z1 = _dot(h, w1_ref[...])
    z2 = _dot(h, w2_ref[...])
    cosm, sinm = _rope_tables(pos_ref, invf_ref)
    hw = QK_NOPE + LANES
    for hd in range(MLA_HEADS):
        q_ref[:, hw * hd:hw * hd + QK_NOPE] = z1[:, hw * hd:hw * hd + QK_NOPE].astype(BF16)
        pe = z1[:, hw * hd + QK_NOPE:hw * (hd + 1)] * cosm + z2[:, LANES * hd:LANES * (hd + 1)] * sinm
        q_ref[:, hw * hd + QK_NOPE:hw * (hd + 1)] = pe.astype(BF16)


def _mla_q(proj, pos, invf, g, w1, w2, tm=512):
    t = proj.shape[0]
    n = MLA_HEADS * (QK_NOPE + LANES)
    return pl.pallas_call(
        _mla_q_kernel,
        grid=(t // tm,),
        in_specs=[
            pl.BlockSpec((tm, Q_LORA), lambda i: (i, COL_CQ)),
            pl.BlockSpec((tm, 1), lambda i: (i, 0)),
            pl.BlockSpec((1, LANES), lambda i: (0, 0)),
            pl.BlockSpec((1, Q_LORA), lambda i: (0, 0)),
            pl.BlockSpec(w1.shape, lambda i: (0, 0)),
            pl.BlockSpec(w2.shape, lambda i: (0, 0)),
        ],
        out_specs=pl.BlockSpec((tm, n), lambda i: (i, 0)),
        out_shape=jax.ShapeDtypeStruct((t, n), BF16),
        compiler_params=_cparams(("parallel",)),
        name="mla_q",
    )(proj, pos, invf, g, w1, w2)


def _mla_kv_kernel(ckv_ref, sm_ref, pos_ref, invf_ref, g_ref, w_ref, k_ref, v_ref):
    h = _rms(ckv_ref[...], g_ref[...]).astype(BF16)
    z = _dot(h, w_ref[...])
    cosm, sinm = _rope_tables(pos_ref, invf_ref)
    kpe = (sm_ref[:, 2 * LANES:3 * LANES] * cosm + sm_ref[:, 3 * LANES:4 * LANES] * sinm).astype(BF16)
    hw = QK_NOPE + LANES
    for hd in range(MLA_HEADS):
        k_ref[:, hw * hd:hw * hd + QK_NOPE] = z[:, QK_NOPE * hd:QK_NOPE * (hd + 1)].astype(BF16)
        k_ref[:, hw * hd + QK_NOPE:hw * (hd + 1)] = kpe
    v_ref[...] = z[:, MLA_HEADS * QK_NOPE:].astype(BF16)


def _mla_kv(proj, pos, invf, g, w, tm=512):
    t = proj.shape[0]
    nk = MLA_HEADS * (QK_NOPE + LANES)
    nv = MLA_HEADS * V_HEAD
    return pl.pallas_call(
        _mla_kv_kernel,
        grid=(t // tm,),
        in_specs=[
            pl.BlockSpec((tm, KV_LORA), lambda i: (i, COL_CKV)),
            pl.BlockSpec((tm, SMALL), lambda i: (i, COL_SMALL)),
            pl.BlockSpec((tm, 1), lambda i: (i, 0)),
            pl.BlockSpec((1, LANES), lambda i: (0, 0)),
            pl.BlockSpec((1, KV_LORA), lambda i: (0, 0)),
            pl.BlockSpec(w.shape, lambda i: (0, 0)),
        ],
        out_specs=[
            pl.BlockSpec((tm, nk), lambda i: (i, 0)),
            pl.BlockSpec((tm, nv), lambda i: (i, 0)),
        ],
        out_shape=[jax.ShapeDtypeStruct((t, nk), BF16), jax.ShapeDtypeStruct((t, nv), BF16)],
        compiler_params=_cparams(("parallel",)),
        name="mla_kv",
    )(proj, proj, pos, invf, g, w)


def _attn_kernel(q_ref, k_ref, v_ref, o_ref, *, blk, scale):
    qi = pl.program_id(2)
    q = q_ref[...]

    def step(j, carry, masked):
        m, l, acc = carry
        start = pl.multiple_of(j * blk, blk)
        s = _dot_nt(q, k_ref[pl.ds(start, blk), :]) * scale
        if masked:
            row = lax.broadcasted_iota(jnp.int32, s.shape, 0)
            col = lax.broadcasted_iota(jnp.int32, s.shape, 1)
            s = jnp.where(row >= col, s, NEG_INF)
        m_new = jnp.maximum(m, jnp.max(s, axis=-1, keepdims=True))
        alpha = jnp.exp(m - m_new)
        p = jnp.exp(s - m_new)
        l = alpha * l + jnp.sum(p, axis=-1, keepdims=True)
        acc = alpha * acc + _dot(p.astype(BF16), v_ref[pl.ds(start, blk), :])
        return m_new, l, acc

    init = (jnp.full((blk, 1), NEG_INF, F32), jnp.zeros((blk, 1), F32), jnp.zeros((blk, V_HEAD), F32))
    carry = lax.fori_loop(0, qi, lambda j, cr: step(j, cr, False), init)
    _, l, acc = step(qi, carry, True)
    o_ref[...] = (acc / l).astype(o_ref.dtype)


def _attn(q, k, v, bsz, seq, blk=256):
    t = q.shape[0]
    nq = seq // blk
    hw = QK_NOPE + LANES
    kern = functools.partial(_attn_kernel, blk=blk, scale=(QK_NOPE + QK_ROPE) ** -0.5)
    return pl.pallas_call(
        kern,
        grid=(bsz, MLA_HEADS, nq),
        in_specs=[
            pl.BlockSpec((blk, hw), lambda b, h, i: (b * nq + i, h)),
            pl.BlockSpec((seq, hw), lambda b, h, i: (b, h)),
            pl.BlockSpec((seq, V_HEAD), lambda b, h, i: (b, h)),
        ],
        out_specs=pl.BlockSpec((blk, V_HEAD), lambda b, h, i: (b * nq + i, h)),
        out_shape=jax.ShapeDtypeStruct((t, MLA_HEADS * V_HEAD), BF16),
        compiler_params=_cparams(("parallel", "parallel", "arbitrary")),
        name="attn",
    )(q, k, v)


def _outproj_kernel(yr_ref, ym_ref, wt_ref, wb_ref, x_ref, g_ref, x1_ref, h2_ref):
    x1 = x_ref[...] + _dot(yr_ref[...], wt_ref[...]) + _dot(ym_ref[...], wb_ref[...])
    x1_ref[...] = x1
    h2_ref[...] = _rms(x1, g_ref[...]).astype(BF16)


def _outproj(yr, ym, wt, wb, x2, g, tm=256):
    t, d = x2.shape
    return pl.pallas_call(
        _outproj_kernel,
        grid=(t // tm,),
        in_specs=[
            pl.BlockSpec((tm, RWKV_WIDTH), lambda i: (i, 0)),
            pl.BlockSpec((tm, MLA_WIDTH), lambda i: (i, 0)),
            pl.BlockSpec(wt.shape, lambda i: (0, 0)),
            pl.BlockSpec(wb.shape, lambda i: (0, 0)),
            pl.BlockSpec((tm, d), lambda i: (i, 0)),
            pl.BlockSpec((1, d), lambda i: (0, 0)),
        ],
        out_specs=[pl.BlockSpec((tm, d), lambda i: (i, 0)), pl.BlockSpec((tm, d), lambda i: (i, 0))],
        out_shape=[jax.ShapeDtypeStruct((t, d), F32), jax.ShapeDtypeStruct((t, d), BF16)],
        compiler_params=_cparams(("parallel",)),
        name="outproj",
    )(yr, ym, wt, wb, x2, g)


HALO = 16


def _ffn_kernel(h_ref, halo_ref, wg_ref, wu_ref, cw_ref, cb_ref, wd_ref, x1_ref, gf_ref, o_ref,
                acc_scr, *, tm, seq):
    i = pl.program_id(0)
    j = pl.program_id(1)
    h = h_ref[...]
    gate = _dot(h, wg_ref[...])
    gh = _dot(halo_ref[...], wg_ref[...])
    keep = jnp.where((i * tm) % seq != 0, 1.0, 0.0)
    gm1 = gh[HALO - 1:HALO, :] * keep
    gm2 = gh[HALO - 2:HALO - 1, :] * keep
    row = lax.broadcasted_iota(jnp.int32, gate.shape, 0)
    g1 = jnp.where(row == 0, gm1, pltpu.roll(gate, 1, 0))
    g2 = jnp.where(row == 0, gm2, jnp.where(row == 1, gm1, pltpu.roll(gate, 2, 0)))
    cw = cw_ref[...]
    conv = cw[0:1, :] * g2 + cw[1:2, :] * g1 + cw[2:3, :] * gate + cb_ref[...]
    up = _dot(h, wu_ref[...])
    act = (conv * _sigmoid(conv) * up).astype(BF16)
    part = _dot(act, wd_ref[...])

    @pl.when(j == 0)
    def _():
        acc_scr[...] = part

    @pl.when(j > 0)
    def _():
        acc_scr[...] += part

    @pl.when(j == pl.num_programs(1) - 1)
    def _():
        o_ref[...] = _rms(x1_ref[...] + acc_scr[...], gf_ref[...])


def _ffn(h2, wg, wu, cw, cb, wd, x1, gf, seq, tm=512, tf=512):
    t, d = h2.shape
    f = wg.shape[1]
    kern = functools.partial(_ffn_kernel, tm=tm, seq=seq)
    hb = tm // HALO
    return pl.pallas_call(
        kern,
        grid=(t // tm, f // tf),
        in_specs=[
            pl.BlockSpec((tm, d), lambda i, j: (i, 0)),
            pl.BlockSpec((HALO, d), lambda i, j: (jnp.maximum(i * hb - 1, 0), 0)),
            pl.BlockSpec((d, tf), lambda i, j: (0, j)),
            pl.BlockSpec((d, tf), lambda i, j: (0, j)),
            pl.BlockSpec((3, tf), lambda i, j: (0, j)),
            pl.BlockSpec((1, tf), lambda i, j: (0, j)),
            pl.BlockSpec((tf, d), lambda i, j: (j, 0)),
            pl.BlockSpec((tm, d), lambda i, j: (i, 0)),
            pl.BlockSpec((1, d), lambda i, j: (0, 0)),
        ],
        out_specs=pl.BlockSpec((tm, d), lambda i, j: (i, 0)),
        out_shape=jax.ShapeDtypeStruct((t, d), F32),
        scratch_shapes=[pltpu.VMEM((tm, d), F32)],
        compiler_params=_cparams(("parallel", "arbitrary")),
        name="ffn",
    )(h2, h2, wg, wu, cw, cb, wd, x1, gf)


def _pack_in(w_in, mu):
    d = w_in.shape[0]
    o = 3 * RWKV_WIDTH
    wl = w_in[:, o:o + DECAY_LORA]
    al = w_in[:, o + DECAY_LORA:o + DECAY_LORA + AAA_LORA]
    og = o + DECAY_LORA + AAA_LORA
    gl = w_in[:, og:og + GATE_LORA]
    oq = og + GATE_LORA
    cq = w_in[:, oq:oq + Q_LORA]
    ckv = w_in[:, oq + Q_LORA:oq + Q_LORA + KV_LORA]
    kpe = w_in[:, oq + Q_LORA + KV_LORA:]
    hr = QK_ROPE // 2
    kpe_rot = jnp.concatenate([-kpe[:, hr:], kpe[:, :hr]], axis=1)
    z = lambda n: jnp.zeros((d, n), w_in.dtype)
    rest = GATE_LORA - LANES
    small = jnp.concatenate(
        [wl, al, gl[:, :LANES], kpe, gl[:, LANES:], z(LANES - QK_ROPE - rest), kpe_rot, z(LANES - QK_ROPE)],
        axis=1)
    w_p = jnp.concatenate([w_in[:, :o], cq, ckv, small], axis=1).astype(BF16)
    zm = lambda n: jnp.zeros((n,), mu.dtype)
    mu_sm = jnp.concatenate(
        [mu[o:og], mu[og:og + LANES], zm(QK_ROPE), mu[og + LANES:og + GATE_LORA],
         zm(LANES - QK_ROPE - rest), zm(LANES)])
    return w_p, mu[:RWKV_WIDTH], mu[RWKV_WIDTH:2 * RWKV_WIDTH], mu[2 * RWKV_WIDTH:o], mu_sm


def _pack_lora(w2, a2, g2):
    n = w2.shape[1]
    z = lambda r: jnp.zeros((r, n), w2.dtype)
    rest = GATE_LORA - LANES
    w2p = jnp.concatenate([w2, z(LANES - DECAY_LORA)], axis=0)
    a2p = jnp.concatenate([z(DECAY_LORA), a2], axis=0)
    g2p = jnp.concatenate([g2[:LANES], z(QK_ROPE), g2[LANES:], z(LANES - QK_ROPE - rest)], axis=0)
    return w2p.astype(BF16), a2p.astype(BF16), g2p.astype(BF16)


def _pack_q(w_uq):
    d = w_uq.shape[0]
    hw = QK_NOPE + QK_ROPE
    hr = QK_ROPE // 2
    w = w_uq.reshape(d, MLA_HEADS, hw)
    pe = w[:, :, QK_NOPE:]
    z = jnp.zeros((d, MLA_HEADS, LANES - QK_ROPE), w_uq.dtype)
    w1 = jnp.concatenate([w[:, :, :QK_NOPE], pe, z], axis=2).reshape(d, -1)
    w2 = jnp.concatenate([-pe[:, :, hr:], pe[:, :, :hr], z], axis=2).reshape(d, -1)
    return w1.astype(BF16), w2.astype(BF16)


def _pack_kv(w_ukv):
    d = w_ukv.shape[0]
    w = w_ukv.reshape(d, MLA_HEADS, QK_NOPE + V_HEAD)
    return jnp.concatenate([w[:, :, :QK_NOPE].reshape(d, -1), w[:, :, QK_NOPE:].reshape(d, -1)],
                           axis=1).astype(BF16)


def kernel(x, positions, attn_norm_g, w_in, rwkv_mu, rwkv_w0, rwkv_w2, rwkv_a0, rwkv_a2, rwkv_g2,
           rwkv_k_k, rwkv_k_a, rwkv_r_k, rwkv_gn_w, rwkv_gn_b, mla_q_norm_g, mla_w_uq,
           mla_kv_norm_g, mla_w_ukv, w_out, ffn_norm_g, ffn_w_gate, ffn_w_up, ffn_conv_w,
           ffn_conv_b, ffn_w_down, final_norm_g):
    bsz, seq, d = x.shape
    t = bsz * seq
    depth = w_in.shape[0]
    row = lambda a: a.reshape(1, -1)
    xc = x.reshape(t, d)
    pos = positions.reshape(t, 1)
    hr = QK_ROPE // 2
    inv_freq = ROPE_THETA ** (-jnp.arange(hr, dtype=F32) / hr)
    invf = jnp.concatenate([inv_freq, inv_freq, jnp.zeros((LANES - QK_ROPE,), F32)]).reshape(1, LANES)

    assert depth == 1, "the final RMSNorm is fused into the (single) layer's ffn kernel"
    l = 0
    w_p, mu_r, mu_k, mu_v, mu_sm = _pack_in(w_in[l], rwkv_mu[l])
    w2p, a2p, g2p = _pack_lora(rwkv_w2[l], rwkv_a2[l], rwkv_g2[l])
    wq1, wq2 = _pack_q(mla_w_uq[l])
    wkv = _pack_kv(mla_w_ukv[l])

    proj = _inproj(xc, row(attn_norm_g[l]), w_p)
    y_r = _rwkv(proj, bsz, seq, row(mu_r), row(mu_k), row(mu_v), row(mu_sm),
                row(rwkv_w0[l]), row(rwkv_a0[l]), row(rwkv_k_k[l]), row(rwkv_k_a[l]),
                row(rwkv_r_k[l]), row(rwkv_gn_w[l]), row(rwkv_gn_b[l]), w2p, a2p, g2p)
    q = _mla_q(proj, pos, invf, row(mla_q_norm_g[l]), wq1, wq2)
    k, v = _mla_kv(proj, pos, invf, row(mla_kv_norm_g[l]), wkv)
    y_m = _attn(q, k, v, bsz, seq)
    wo = w_out[l].astype(BF16)
    x1, h2 = _outproj(y_r, y_m, wo[:RWKV_WIDTH], wo[RWKV_WIDTH:], xc, row(ffn_norm_g[l]))
    out = _ffn(h2, ffn_w_gate[l].astype(BF16), ffn_w_up[l].astype(BF16), ffn_conv_w[l],
               row(ffn_conv_b[l]), ffn_w_down[l].astype(BF16), x1, row(final_norm_g), seq)
    return out.reshape(bsz, seq, d)
```

```python
import functools
import math

import jax
import jax.numpy as jnp
from jax import lax
from jax.experimental import pallas as pl
from jax.experimental.pallas import tpu as pltpu

F32 = jnp.float32
BF16 = jnp.bfloat16

D_MODEL = 2048
RWKV_HEAD = 64
RWKV_WIDTH = 1024
RWKV_HEADS = 16
DECAY_LORA = 64
AAA_LORA = 64
GATE_LORA = 160
GN_EPS = 64e-5
QK_NOPE = 128
QK_ROPE = 64
V_HEAD = 128
MLA_WIDTH = 1024
MLA_HEADS = 8
Q_LORA = 512
KV_LORA = 512
ROPE_THETA = 10000.0
D_FF = 5632
NORM_EPS = 1e-6
NEG_INF = -1e30

LANES = 128
SMALL = 512
D_IN_P = 3 * RWKV_WIDTH + Q_LORA + KV_LORA + SMALL
COL_CQ = 3 * RWKV_WIDTH // Q_LORA
COL_CKV = COL_CQ + 1
COL_SMALL = COL_CKV + 1
CHUNK = 64
PAIRS = RWKV_WIDTH // LANES
VMEM_LIMIT = 56 * 1024 * 1024


def _cparams(sem):
    return pltpu.CompilerParams(dimension_semantics=sem, vmem_limit_bytes=VMEM_LIMIT)


def _rms(x, g):
    ms = jnp.mean(x * x, axis=-1, keepdims=True)
    return x * lax.rsqrt(ms + NORM_EPS) * g


def _sigmoid(x):
    return 1.0 / (1.0 + jnp.exp(-x))


def _dot(a, b):
    return jnp.dot(a, b, preferred_element_type=F32)


def _dot_nt(a, b):
    return lax.dot_general(a, b, (((1,), (1,)), ((), ())), preferred_element_type=F32)


def _dot_tn(a, b):
    return lax.dot_general(a, b, (((0,), (0,)), ((), ())), preferred_element_type=F32)


def _inproj_kernel(x_ref, g_ref, w_ref, o_ref):
    o_ref[...] = _dot(_rms(x_ref[...], g_ref[...]).astype(BF16), w_ref[...])


def _inproj(x2, g, w_p, tm=512):
    t, d = x2.shape
    n = w_p.shape[1]
    return pl.pallas_call(
        _inproj_kernel,
        grid=(t // tm,),
        in_specs=[
            pl.BlockSpec((tm, d), lambda i: (i, 0)),
            pl.BlockSpec((1, d), lambda i: (0, 0)),
            pl.BlockSpec((d, n), lambda i: (0, 0), pipeline_mode=pl.Buffered(1)),
        ],
        out_specs=pl.BlockSpec((tm, n), lambda i: (i, 0)),
        out_shape=jax.ShapeDtypeStruct((t, n), F32),
        compiler_params=_cparams(("parallel",)),
        name="inproj",
    )(x2, g, w_p)


def _rwkv_kernel(r_ref, k_ref, v_ref, sm_ref, mur_ref, muk_ref, muv_ref, musm_ref,
                 w0_ref, a0_ref, kk_ref, ka_ref, rk_ref, gnw_ref, gnb_ref,
                 w2_ref, a2_ref, g2_ref, o_ref,
                 st_scr, pr_scr, pk_scr, pv_scr, psm_scr):
    c = CHUNK
    half = RWKV_HEAD

    @pl.when(pl.program_id(1) == 0)
    def _():
        st_scr[...] = jnp.zeros_like(st_scr)
        pr_scr[...] = jnp.zeros_like(pr_scr)
        pk_scr[...] = jnp.zeros_like(pk_scr)
        pv_scr[...] = jnp.zeros_like(pv_scr)
        psm_scr[...] = jnp.zeros_like(psm_scr)

    def shift(x_ref, p_scr, mu_ref):
        x = x_ref[...]
        row = lax.broadcasted_iota(jnp.int32, x.shape, 0)
        xp = jnp.where(row == 0, p_scr[...], pltpu.roll(x, 1, 0))
        p_scr[...] = x_ref[c - 1:c, :]
        return x + (xp - x) * mu_ref[...]

    rs = shift(r_ref, pr_scr, mur_ref)
    ks = shift(k_ref, pk_scr, muk_ref)
    vs = shift(v_ref, pv_scr, muv_ref)
    sm = shift(sm_ref, psm_scr, musm_ref)

    ri = lax.broadcasted_iota(jnp.int32, (c, LANES), 0)
    li = lax.broadcasted_iota(jnp.int32, (c, LANES), 1)
    lj = li & (half - 1)
    lane_a = li < half
    strict = lj < ri
    incl = lj <= ri
    eye2 = (lj == ri).astype(F32)
    r2 = lax.broadcasted_iota(jnp.int32, (LANES, LANES), 0)
    l2 = lax.broadcasted_iota(jnp.int32, (LANES, LANES), 1)
    bd = (r2 < half) == (l2 < half)
    bd_ones = bd.astype(BF16)
    tr = lax.broadcasted_iota(jnp.int32, (c, c), 0)
    tc = lax.broadcasted_iota(jnp.int32, (c, c), 1)
    tril = (tc <= tr).astype(BF16)

    def bd_stack(x):
        xb = x.astype(BF16)
        z = jnp.zeros_like(xb)
        return jnp.concatenate([jnp.where(lane_a, xb, z), jnp.where(lane_a, z, xb)], axis=0)

    def segsum(x):
        xs = jnp.concatenate([x[:, LANES * g:LANES * (g + 1)] for g in range(PAIRS)], axis=0)
        hi = xs.astype(BF16)
        lo = (xs - hi.astype(F32)).astype(BF16)
        s = _dot(hi, bd_ones) + _dot(lo, bd_ones)
        return jnp.concatenate([s[c * g:c * (g + 1), :] for g in range(PAIRS)], axis=1)

    def cumsum_rows(x):
        hi = x.astype(BF16)
        r1 = x - hi.astype(F32)
        mid = r1.astype(BF16)
        lo = (r1 - mid.astype(F32)).astype(BF16)
        return _dot(tril, hi) + _dot(tril, mid) + _dot(tril, lo)

    g0 = sm[:, 0:LANES]
    g12 = sm[:, LANES:3 * LANES]
    zw = _dot(jnp.tanh(g0).astype(BF16), w2_ref[...])
    za = _dot(g0.astype(BF16), a2_ref[...])
    gate = _dot(_sigmoid(g12).astype(BF16), g2_ref[...])

    ld = (-math.exp(-0.5)) * _sigmoid(w0_ref[...] + zw)
    a = _sigmoid(a0_ref[...] + za)
    kkr = ks * kk_ref[...]
    kk = kkr * lax.rsqrt(jnp.maximum(segsum(kkr * kkr), 1e-24))
    kp = ks * (1.0 + (a - 1.0) * ka_ref[...])
    beta = kk * a
    bonus = segsum(rs * kp * rk_ref[...])

    lg = cumsum_rows(ld)
    lgc = lg[c - 1:c, :]
    einv = jnp.exp(-lg)
    ec = jnp.exp(lgc - lg)
    gc = jnp.exp(lgc)
    rt = rs * jnp.exp(lg)
    at = -kk * jnp.exp(lg - ld)
    bt = beta * einv
    kt = kp * einv
    bh = beta * ec
    kh = kp * ec

    prs = range(PAIRS)
    sls = [slice(LANES * p, LANES * (p + 1)) for p in prs]
    v_p = [vs[:, sl] for sl in sls]
    lhs1 = [jnp.concatenate([at[:, sl], rt[:, sl]], axis=0).astype(BF16) for sl in sls]
    rhs1 = [jnp.concatenate([bd_stack(bt[:, sl]), bd_stack(kt[:, sl])], axis=0) for sl in sls]
    s0 = [st_scr[p] for p in prs]
    pm = [_dot_nt(lhs1[p], rhs1[p]) for p in prs]
    xh = [_dot_nt(lhs1[p], s0[p].astype(BF16)) for p in prs]
    lab = [jnp.where(strict, m[0:c, 0:LANES], 0.0) for m in pm]
    lm = [jnp.concatenate([jnp.where(strict, m[0:c, LANES:2 * LANES], 0.0),
                           jnp.where(incl, m[c:2 * c, LANES:2 * LANES], 0.0)], axis=0).astype(BF16)
          for m in pm]
    mrb = [jnp.where(incl, m[c:2 * c, 0:LANES], 0.0).astype(BF16) for m in pm]
    lv = [_dot(lm[p], bd_stack(v_p[p])) for p in prs]

    pw = [_dot(l.astype(BF16), bd_stack(l)) for l in lab]
    tinv = [eye2 + l for l in lab]
    for _ in range(4):
        res = [_dot(pw[p].astype(BF16),
                    jnp.concatenate([bd_stack(tinv[p]), bd_stack(pw[p])], axis=1)) for p in prs]
        tinv = [tinv[p] + res[p][:, 0:LANES] for p in prs]
        pw = [res[p][:, LANES:2 * LANES] for p in prs]
    tinv = [tinv[p] + _dot(pw[p].astype(BF16), bd_stack(tinv[p])) for p in prs]

    x = [xh[p][0:c] + lv[p][0:c] for p in prs]
    u = [_dot(tinv[p].astype(BF16), bd_stack(x[p])) for p in prs]
    ys = [xh[p][c:2 * c] + lv[p][c:2 * c] + _dot(mrb[p], bd_stack(u[p])) for p in prs]
    uv = [jnp.concatenate([u[p], v_p[p]], axis=0).astype(BF16) for p in prs]
    bk = [jnp.concatenate([bh[:, sl], kh[:, sl]], axis=0).astype(BF16) for sl in sls]
    ds = [_dot_tn(uv[p], bk[p]) for p in prs]
    for p in prs:
        st_scr[p] = s0[p] * gc[:, sls[p]] + jnp.where(bd, ds[p], 0.0)

    y = jnp.concatenate(ys, axis=1)
    inv_n = 1.0 / RWKV_HEAD
    mean = segsum(y) * inv_n
    d = y - mean
    var = segsum(d * d) * inv_n
    yn = d * lax.rsqrt(var + GN_EPS) * gnw_ref[...] + gnb_ref[...]
    o_ref[...] = ((yn + bonus * vs) * gate).astype(o_ref.dtype)


def _rwkv(proj, bsz, seq, mu_r, mu_k, mu_v, mu_sm, w0, a0, k_k, k_a, r_k, gn_w, gn_b, w2p, a2p, g2p):
    t = proj.shape[0]
    w = RWKV_WIDTH
    nc = seq // CHUNK
    row = lambda b, c: b * nc + c
    vec = lambda n: pl.BlockSpec((1, n), lambda b, c: (0, 0))
    full = lambda a: pl.BlockSpec(a.shape, lambda b, c: (0, 0))
    return pl.pallas_call(
        _rwkv_kernel,
        grid=(bsz, nc),
        in_specs=[
            pl.BlockSpec((CHUNK, w), lambda b, c: (row(b, c), 0)),
            pl.BlockSpec((CHUNK, w), lambda b, c: (row(b, c), 1)),
            pl.BlockSpec((CHUNK, w), lambda b, c: (row(b, c), 2)),
            pl.BlockSpec((CHUNK, SMALL), lambda b, c: (row(b, c), COL_SMALL)),
            vec(w), vec(w), vec(w), vec(SMALL),
            vec(w), vec(w), vec(w), vec(w), vec(w), vec(w), vec(w),
            full(w2p), full(a2p), full(g2p),
        ],
        out_specs=pl.BlockSpec((CHUNK, w), lambda b, c: (row(b, c), 0)),
        out_shape=jax.ShapeDtypeStruct((t, w), BF16),
        scratch_shapes=[
            pltpu.VMEM((PAIRS, LANES, LANES), F32),
            pltpu.VMEM((1, w), F32), pltpu.VMEM((1, w), F32), pltpu.VMEM((1, w), F32),
            pltpu.VMEM((1, SMALL), F32),
        ],
        compiler_params=_cparams(("parallel", "arbitrary")),
        name="rwkv",
    )(proj, proj, proj, proj, mu_r, mu_k, mu_v, mu_sm, w0, a0, k_k, k_a, r_k, gn_w, gn_b,
      w2p, a2p, g2p)


def _rope_tables(pos_ref, invf_ref):
    ang = pos_ref[...].astype(F32) * invf_ref[...]
    keep = lax.broadcasted_iota(jnp.int32, ang.shape, 1) < QK_ROPE
    return jnp.where(keep, jnp.cos(ang), 0.0), jnp.where(keep, jnp.sin(ang), 0.0)


def _mla_q_kernel(cq_ref, pos_ref, invf_ref, g_ref, w1_ref, w2_ref, q_ref):
    h = _rms(cq_ref[...], g_ref[...]).astype(BF16)
    z1 = _dot(h, w1_ref[...])
    z2 = _dot(h, w2_ref[...])
    cosm, sinm = _rope_tables(pos_ref, invf_ref)
    hw = QK_NOPE + LANES
    for hd in range(MLA_HEADS):
        q_ref[:, hw * hd:hw * hd + QK_NOPE] = z1[:, hw * hd:hw * hd + QK_NOPE].astype(BF16)
        pe = z1[:, hw * hd + QK_NOPE:hw * (hd + 1)] * cosm + z2[:, LANES * hd:LANES * (hd + 1)] * sinm
        q_ref[:, hw * hd + QK_NOPE:hw * (hd + 1)] = pe.astype(BF16)


def _mla_q(proj, pos, invf, g, w1, w2, tm=512):
    t = proj.shape[0]
    n = MLA_HEADS * (QK_NOPE + LANES)
    return pl.pallas_call(
        _mla_q_kernel,
        grid=(t // tm,),
        in_specs=[
            pl.BlockSpec((tm, Q_LORA), lambda i: (i, COL_CQ)),
            pl.BlockSpec((tm, 1), lambda i: (i, 0)),
            pl.BlockSpec((1, LANES), lambda i: (0, 0)),
            pl.BlockSpec((1, Q_LORA), lambda i: (0, 0)),
            pl.BlockSpec(w1.shape, lambda i: (0, 0)),
            pl.BlockSpec(w2.shape, lambda i: (0, 0)),
        ],
        out_specs=pl.BlockSpec((tm, n), lambda i: (i, 0)),
        out_shape=jax.ShapeDtypeStruct((t, n), BF16),
        compiler_params=_cparams(("parallel",)),
        name="mla_q",
    )(proj, pos, invf, g, w1, w2)


def _mla_kv_kernel(ckv_ref, sm_ref, pos_ref, invf_ref, g_ref, w_ref, k_ref, v_ref):
    h = _rms(ckv_ref[...], g_ref[...]).astype(BF16)
    z = _dot(h, w_ref[...])
    cosm, sinm = _rope_tables(pos_ref, invf_ref)
    kpe = (sm_ref[:, 2 * LANES:3 * LANES] * cosm + sm_ref[:, 3 * LANES:4 * LANES] * sinm).astype(BF16)
    hw = QK_NOPE + LANES
    for hd in range(MLA_HEADS):
        k_ref[:, hw * hd:hw * hd + QK_NOPE] = z[:, QK_NOPE * hd:QK_NOPE * (hd + 1)].astype(BF16)
        k_ref[:, hw * hd + QK_NOPE:hw * (hd + 1)] = kpe
    v_ref[...] = z[:, MLA_HEADS * QK_NOPE:].astype(BF16)


def _mla_kv(proj, pos, invf, g, w, tm=512):
    t = proj.shape[0]
    nk = MLA_HEADS * (QK_NOPE + LANES)
    nv = MLA_HEADS * V_HEAD
    return pl.pallas_call(
        _mla_kv_kernel,
        grid=(t // tm,),
        in_specs=[
            pl.BlockSpec((tm, KV_LORA), lambda i: (i, COL_CKV)),
            pl.BlockSpec((tm, SMALL), lambda i: (i, COL_SMALL)),
            pl.BlockSpec((tm, 1), lambda i: (i, 0)),
            pl.BlockSpec((1, LANES), lambda i: (0, 0)),
            pl.BlockSpec((1, KV_LORA), lambda i: (0, 0)),
            pl.BlockSpec(w.shape, lambda i: (0, 0)),
        ],
        out_specs=[
            pl.BlockSpec((tm, nk), lambda i: (i, 0)),
            pl.BlockSpec((tm, nv), lambda i: (i, 0)),
        ],
        out_shape=[jax.ShapeDtypeStruct((t, nk), BF16), jax.ShapeDtypeStruct((t, nv), BF16)],
        compiler_params=_cparams(("parallel",)),
        name="mla_kv",
    )(proj, proj, pos, invf, g, w)


def _attn_kernel(q_ref, k_ref, v_ref, o_ref, *, blk, scale):
    qi = pl.program_id(2)
    q = q_ref[...]

    def step(j, carry, masked):
        m, l, acc = carry
        start = pl.multiple_of(j * blk, blk)
        s = _dot_nt(q, k_ref[pl.ds(start, blk), :]) * scale
        if masked:
            row = lax.broadcasted_iota(jnp.int32, s.shape, 0)
            col = lax.broadcasted_iota(jnp.int32, s.shape, 1)
            s = jnp.where(row >= col, s, NEG_INF)
        m_new = jnp.maximum(m, jnp.max(s, axis=-1, keepdims=True))
        alpha = jnp.exp(m - m_new)
        p = jnp.exp(s - m_new)
        l = alpha * l + jnp.sum(p, axis=-1, keepdims=True)
        acc = alpha * acc + _dot(p.astype(BF16), v_ref[pl.ds(start, blk), :])
        return m_new, l, acc

    init = (jnp.full((blk, 1), NEG_INF, F32), jnp.zeros((blk, 1), F32), jnp.zeros((blk, V_HEAD), F32))
    carry = lax.fori_loop(0, qi, lambda j, cr: step(j, cr, False), init)
    _, l, acc = step(qi, carry, True)
    o_ref[...] = (acc / l).astype(o_ref.dtype)


def _attn(q, k, v, bsz, seq, blk=512):
    t = q.shape[0]
    nq = seq // blk
    hw = QK_NOPE + LANES
    kern = functools.partial(_attn_kernel, blk=blk, scale=(QK_NOPE + QK_ROPE) ** -0.5)
    return pl.pallas_call(
        kern,
        grid=(bsz, MLA_HEADS, nq),
        in_specs=[
            pl.BlockSpec((blk, hw), lambda b, h, i: (b * nq + i, h)),
            pl.BlockSpec((seq, hw), lambda b, h, i: (b, h)),
            pl.BlockSpec((seq, V_HEAD), lambda b, h, i: (b, h)),
        ],
        out_specs=pl.BlockSpec((blk, V_HEAD), lambda b, h, i: (b * nq + i, h)),
        out_shape=jax.ShapeDtypeStruct((t, MLA_HEADS * V_HEAD), BF16),
        compiler_params=_cparams(("parallel", "parallel", "arbitrary")),
        name="attn",
    )(q, k, v)


def _outproj_kernel(yr_ref, ym_ref, wt_ref, wb_ref, x_ref, g_ref, x1_ref, h2_ref):
    x1 = x_ref[...] + _dot(yr_ref[...], wt_ref[...]) + _dot(ym_ref[...], wb_ref[...])
    x1_ref[...] = x1
    h2_ref[...] = _rms(x1, g_ref[...]).astype(BF16)


def _outproj(yr, ym, wt, wb, x2, g, tm=256):
    t, d = x2.shape
    return pl.pallas_call(
        _outproj_kernel,
        grid=(t // tm,),
        in_specs=[
            pl.BlockSpec((tm, RWKV_WIDTH), lambda i: (i, 0)),
            pl.BlockSpec((tm, MLA_WIDTH), lambda i: (i, 0)),
            pl.BlockSpec(wt.shape, lambda i: (0, 0)),
            pl.BlockSpec(wb.shape, lambda i: (0, 0)),
            pl.BlockSpec((tm, d), lambda i: (i, 0)),
            pl.BlockSpec((1, d), lambda i: (0, 0)),
        ],
        out_specs=[pl.BlockSpec((tm, d), lambda i: (i, 0)), pl.BlockSpec((tm, d), lambda i: (i, 0))],
        out_shape=[jax.ShapeDtypeStruct((t, d), F32), jax.ShapeDtypeStruct((t, d), BF16)],
        compiler_params=_cparams(("parallel",)),
        name="outproj",
    )(yr, ym, wt, wb, x2, g)


HALO = 16


def _ffn_kernel(h_ref, halo_ref, wg_ref, wu_ref, cw_ref, cb_ref, wd_ref, x1_ref, gf_ref, o_ref,
                acc_scr, *, tm, seq):
    i = pl.program_id(0)
    j = pl.program_id(1)
    h = h_ref[...]
    gate = _dot(h, wg_ref[...])
    gh = _dot(halo_ref[...], wg_ref[...])
    keep = jnp.where((i * tm) % seq != 0, 1.0, 0.0)
    gm1 = gh[HALO - 1:HALO, :] * keep
    gm2 = gh[HALO - 2:HALO - 1, :] * keep
    row = lax.broadcasted_iota(jnp.int32, gate.shape, 0)
    g1 = jnp.where(row == 0, gm1, pltpu.roll(gate, 1, 0))
    g2 = jnp.where(row == 0, gm2, jnp.where(row == 1, gm1, pltpu.roll(gate, 2, 0)))
    cw = cw_ref[...]
    conv = cw[0:1, :] * g2 + cw[1:2, :] * g1 + cw[2:3, :] * gate + cb_ref[...]
    up = _dot(h, wu_ref[...])
    act = (conv * _sigmoid(conv) * up).astype(BF16)
    part = _dot(act, wd_ref[...])

    @pl.when(j == 0)
    def _():
        acc_scr[...] = part

    @pl.when(j > 0)
    def _():
        acc_scr[...] += part

    @pl.when(j == pl.num_programs(1) - 1)
    def _():
        o_ref[...] = _rms(x1_ref[...] + acc_scr[...], gf_ref[...])


def _ffn(h2, wg, wu, cw, cb, wd, x1, gf, seq, tm=512, tf=512):
    t, d = h2.shape
    f = wg.shape[1]
    kern = functools.partial(_ffn_kernel, tm=tm, seq=seq)
    hb = tm // HALO
    return pl.pallas_call(
        kern,
        grid=(t // tm, f // tf),
        in_specs=[
            pl.BlockSpec((tm, d), lambda i, j: (i, 0)),
            pl.BlockSpec((HALO, d), lambda i, j: (jnp.maximum(i * hb - 1, 0), 0)),
            pl.BlockSpec((d, tf), lambda i, j: (0, j)),
            pl.BlockSpec((d, tf), lambda i, j: (0, j)),
            pl.BlockSpec((3, tf), lambda i, j: (0, j)),
            pl.BlockSpec((1, tf), lambda i, j: (0, j)),
            pl.BlockSpec((tf, d), lambda i, j: (j, 0)),
            pl.BlockSpec((tm, d), lambda i, j: (i, 0)),
            pl.BlockSpec((1, d), lambda i, j: (0, 0)),
        ],
        out_specs=pl.BlockSpec((tm, d), lambda i, j: (i, 0)),
        out_shape=jax.ShapeDtypeStruct((t, d), F32),
        scratch_shapes=[pltpu.VMEM((tm, d), F32)],
        compiler_params=_cparams(("parallel", "arbitrary")),
        name="ffn",
    )(h2, h2, wg, wu, cw, cb, wd, x1, gf)


def _pack_in(w_in, mu):
    d = w_in.shape[0]
    o = 3 * RWKV_WIDTH
    wl = w_in[:, o:o + DECAY_LORA]
    al = w_in[:, o + DECAY_LORA:o + DECAY_LORA + AAA_LORA]
    og = o + DECAY_LORA + AAA_LORA
    gl = w_in[:, og:og + GATE_LORA]
    oq = og + GATE_LORA
    cq = w_in[:, oq:oq + Q_LORA]
    ckv = w_in[:, oq + Q_LORA:oq + Q_LORA + KV_LORA]
    kpe = w_in[:, oq + Q_LORA + KV_LORA:]
    hr = QK_ROPE // 2
    kpe_rot = jnp.concatenate([-kpe[:, hr:], kpe[:, :hr]], axis=1)
    z = lambda n: jnp.zeros((d, n), w_in.dtype)
    rest = GATE_LORA - LANES
    small = jnp.concatenate(
        [wl, al, gl[:, :LANES], kpe, gl[:, LANES:], z(LANES - QK_ROPE - rest), kpe_rot, z(LANES - QK_ROPE)],
        axis=1)
    w_p = jnp.concatenate([w_in[:, :o], cq, ckv, small], axis=1).astype(BF16)
    zm = lambda n: jnp.zeros((n,), mu.dtype)
    mu_sm = jnp.concatenate(
        [mu[o:og], mu[og:og + LANES], zm(QK_ROPE), mu[og + LANES:og + GATE_LORA],
         zm(LANES - QK_ROPE - rest), zm(LANES)])
    return w_p, mu[:RWKV_WIDTH], mu[RWKV_WIDTH:2 * RWKV_WIDTH], mu[2 * RWKV_WIDTH:o], mu_sm


def _pack_lora(w2, a2, g2):
    n = w2.shape[1]
    z = lambda r: jnp.zeros((r, n), w2.dtype)
    rest = GATE_LORA - LANES
    w2p = jnp.concatenate([w2, z(LANES - DECAY_LORA)], axis=0)
    a2p = jnp.concatenate([z(DECAY_LORA), a2], axis=0)
    g2p = jnp.concatenate([g2[:LANES], z(QK_ROPE), g2[LANES:], z(LANES - QK_ROPE - rest)], axis=0)
    return w2p.astype(BF16), a2p.astype(BF16), g2p.astype(BF16)


def _pack_q(w_uq):
    d = w_uq.shape[0]
    hw = QK_NOPE + QK_ROPE
    hr = QK_ROPE // 2
    w = w_uq.reshape(d, MLA_HEADS, hw)
    pe = w[:, :, QK_NOPE:]
    z = jnp.zeros((d, MLA_HEADS, LANES - QK_ROPE), w_uq.dtype)
    w1 = jnp.concatenate([w[:, :, :QK_NOPE], pe, z], axis=2).reshape(d, -1)
    w2 = jnp.concatenate([-pe[:, :, hr:], pe[:, :, :hr], z], axis=2).reshape(d, -1)
    return w1.astype(BF16), w2.astype(BF16)


def _pack_kv(w_ukv):
    d = w_ukv.shape[0]
    w = w_ukv.reshape(d, MLA_HEADS, QK_NOPE + V_HEAD)
    return jnp.concatenate([w[:, :, :QK_NOPE].reshape(d, -1), w[:, :, QK_NOPE:].reshape(d, -1)],
                           axis=1).astype(BF16)


def kernel(x, positions, attn_norm_g, w_in, rwkv_mu, rwkv_w0, rwkv_w2, rwkv_a0, rwkv_a2, rwkv_g2,
           rwkv_k_k, rwkv_k_a, rwkv_r_k, rwkv_gn_w, rwkv_gn_b, mla_q_norm_g, mla_w_uq,
           mla_kv_norm_g, mla_w_ukv, w_out, ffn_norm_g, ffn_w_gate, ffn_w_up, ffn_conv_w,
           ffn_conv_b, ffn_w_down, final_norm_g):
    bsz, seq, d = x.shape
    t = bsz * seq
    depth = w_in.shape[0]
    row = lambda a: a.reshape(1, -1)
    xc = x.reshape(t, d)
    pos = positions.reshape(t, 1)
    hr = QK_ROPE // 2
    inv_freq = ROPE_THETA ** (-jnp.arange(hr, dtype=F32) / hr)
    invf = jnp.concatenate([inv_freq, inv_freq, jnp.zeros((LANES - QK_ROPE,), F32)]).reshape(1, LANES)

    assert depth == 1, "the final RMSNorm is fused into the (single) layer's ffn kernel"
    l = 0
    w_p, mu_r, mu_k, mu_v, mu_sm = _pack_in(w_in[l], rwkv_mu[l])
    w2p, a2p, g2p = _pack_lora(rwkv_w2[l], rwkv_a2[l], rwkv_g2[l])
    wq1, wq2 = _pack_q(mla_w_uq[l])
    wkv = _pack_kv(mla_w_ukv[l])

    proj = _inproj(xc, row(attn_norm_g[l]), w_p)
    y_r = _rwkv(proj, bsz, seq, row(mu_r), row(mu_k), row(mu_v), row(mu_sm),
                row(rwkv_w0[l]), row(rwkv_a0[l]), row(rwkv_k_k[l]), row(rwkv_k_a[l]),
                row(rwkv_r_k[l]), row(rwkv_gn_w[l]), row(rwkv_gn_b[l]), w2p, a2p, g2p)
    q = _mla_q(proj, pos, invf, row(mla_q_norm_g[l]), wq1, wq2)
    k, v = _mla_kv(proj, pos, invf, row(mla_kv_norm_g[l]), wkv)
    y_m = _attn(q, k, v, bsz, seq)
    wo = w_out[l].astype(BF16)
    x1, h2 = _outproj(y_r, y_m, wo[:RWKV_WIDTH], wo[RWKV_WIDTH:], xc, row(ffn_norm_g[l]))
    out = _ffn(h2, ffn_w_gate[l].astype(BF16), ffn_w_up[l].astype(BF16), ffn_conv_w[l],
               row(ffn_conv_b[l]), ffn_w_down[l].astype(BF16), x1, row(final_norm_g), seq)
    return out.reshape(bsz, seq, d)
```

```python
import functools
import math

import jax
import jax.numpy as jnp
from jax import lax
from jax.experimental import pallas as pl
from jax.experimental.pallas import tpu as pltpu

F32 = jnp.float32
BF16 = jnp.bfloat16

D_MODEL = 2048
RWKV_HEAD = 64
RWKV_WIDTH = 1024
RWKV_HEADS = 16
DECAY_LORA = 64
AAA_LORA = 64
GATE_LORA = 160
GN_EPS = 64e-5
QK_NOPE = 128
QK_ROPE = 64
V_HEAD = 128
MLA_WIDTH = 1024
MLA_HEADS = 8
Q_LORA = 512
KV_LORA = 512
ROPE_THETA = 10000.0
D_FF = 5632
NORM_EPS = 1e-6
NEG_INF = -1e30

LANES = 128
SUBLANES = 8
SMALL = 512
D_IN_P = 3 * RWKV_WIDTH + Q_LORA + KV_LORA + SMALL
COL_CQ = 3 * RWKV_WIDTH // Q_LORA
COL_CKV = COL_CQ + 1
COL_SMALL = COL_CKV + 1
CHUNK = 64
PAIRS = RWKV_WIDTH // LANES
VMEM_LIMIT = 56 * 1024 * 1024


def _cparams(sem):
    return pltpu.CompilerParams(dimension_semantics=sem, vmem_limit_bytes=VMEM_LIMIT)


def _rms(x, g):
    ms = jnp.mean(x * x, axis=-1, keepdims=True)
    return x * lax.rsqrt(ms + NORM_EPS) * g


def _sigmoid(x):
    return 1.0 / (1.0 + jnp.exp(-x))


def _dot(a, b):
    return jnp.dot(a, b, preferred_element_type=F32)


def _dot_nt(a, b):
    return lax.dot_general(a, b, (((1,), (1,)), ((), ())), preferred_element_type=F32)


def _dot_tn(a, b):
    return lax.dot_general(a, b, (((0,), (0,)), ((), ())), preferred_element_type=F32)


def _inproj_kernel(x_ref, g_ref, w_ref, o_ref):
    o_ref[...] = _dot(_rms(x_ref[...], g_ref[...]).astype(BF16), w_ref[...])


def _inproj(x2, g, w_p, tm=512):
    t, d = x2.shape
    n = w_p.shape[1]
    return pl.pallas_call(
        _inproj_kernel,
        grid=(t // tm,),
        in_specs=[
            pl.BlockSpec((tm, d), lambda i: (i, 0)),
            pl.BlockSpec((1, d), lambda i: (0, 0)),
            pl.BlockSpec((d, n), lambda i: (0, 0), pipeline_mode=pl.Buffered(1)),
        ],
        out_specs=pl.BlockSpec((tm, n), lambda i: (i, 0)),
        out_shape=jax.ShapeDtypeStruct((t, n), F32),
        compiler_params=_cparams(("parallel",)),
        name="inproj",
    )(x2, g, w_p)


def _rwkv_kernel(r_ref, k_ref, v_ref, sm_ref, mur_ref, muk_ref, muv_ref, musm_ref,
                 w0_ref, a0_ref, kk_ref, ka_ref, rk_ref, gnw_ref, gnb_ref,
                 w2_ref, a2_ref, g2_ref, o_ref,
                 st_scr, pr_scr, pk_scr, pv_scr, psm_scr):
    c = CHUNK
    half = RWKV_HEAD
    rows = r_ref.shape[0]
    nch = rows // c

    @pl.when(pl.program_id(1) == 0)
    def _():
        st_scr[...] = jnp.zeros_like(st_scr)
        pr_scr[...] = jnp.zeros_like(pr_scr)
        pk_scr[...] = jnp.zeros_like(pk_scr)
        pv_scr[...] = jnp.zeros_like(pv_scr)
        psm_scr[...] = jnp.zeros_like(psm_scr)

    def shift(x_ref, p_scr, mu_ref):
        x = x_ref[...]
        row = lax.broadcasted_iota(jnp.int32, x.shape, 0)
        xp = jnp.where(row == 0, p_scr[...], pltpu.roll(x, 1, 0))
        p_scr[...] = x_ref[rows - 1:rows, :]
        return x + (xp - x) * mu_ref[...]

    rs = shift(r_ref, pr_scr, mur_ref)
    ks = shift(k_ref, pk_scr, muk_ref)
    vs = shift(v_ref, pv_scr, muv_ref)
    sm = shift(sm_ref, psm_scr, musm_ref)

    ri = lax.broadcasted_iota(jnp.int32, (c, LANES), 0)
    li = lax.broadcasted_iota(jnp.int32, (c, LANES), 1)
    lj = li & (half - 1)
    lane_a = li < half
    strict = lj < ri
    incl = lj <= ri
    eye2 = (lj == ri).astype(F32)
    r2 = lax.broadcasted_iota(jnp.int32, (LANES, LANES), 0)
    l2 = lax.broadcasted_iota(jnp.int32, (LANES, LANES), 1)
    bd = (r2 < half) == (l2 < half)
    bd_ones = bd.astype(BF16)
    tr = lax.broadcasted_iota(jnp.int32, (rows, rows), 0)
    tc = lax.broadcasted_iota(jnp.int32, (rows, rows), 1)
    tril = ((tc <= tr) & ((tr & -c) == (tc & -c))).astype(BF16)

    def bd_stack(x):
        xb = x.astype(BF16)
        z = jnp.zeros_like(xb)
        return jnp.concatenate([jnp.where(lane_a, xb, z), jnp.where(lane_a, z, xb)], axis=0)

    def segsum(x):
        xs = jnp.concatenate([x[:, LANES * g:LANES * (g + 1)] for g in range(PAIRS)], axis=0)
        hi = xs.astype(BF16)
        lo = (xs - hi.astype(F32)).astype(BF16)
        s = _dot(hi, bd_ones) + _dot(lo, bd_ones)
        return jnp.concatenate([s[rows * g:rows * (g + 1), :] for g in range(PAIRS)], axis=1)

    def cumsum_rows(x):
        hi = x.astype(BF16)
        r1 = x - hi.astype(F32)
        mid = r1.astype(BF16)
        lo = (r1 - mid.astype(F32)).astype(BF16)
        return _dot(tril, hi) + _dot(tril, mid) + _dot(tril, lo)

    g0 = sm[:, 0:LANES]
    g12 = sm[:, LANES:3 * LANES]
    zw = _dot(jnp.tanh(g0).astype(BF16), w2_ref[...])
    za = _dot(g0.astype(BF16), a2_ref[...])
    gate = _dot(_sigmoid(g12).astype(BF16), g2_ref[...])

    ld = (-math.exp(-0.5)) * _sigmoid(w0_ref[...] + zw)
    a = _sigmoid(a0_ref[...] + za)
    kkr = ks * kk_ref[...]
    kk = kkr * lax.rsqrt(jnp.maximum(segsum(kkr * kkr), 1e-24))
    kp = ks * (1.0 + (a - 1.0) * ka_ref[...])
    beta = kk * a
    bonus = segsum(rs * kp * rk_ref[...])

    lg = cumsum_rows(ld)
    lgc = [lg[c * ci + c - 1:c * (ci + 1), :] for ci in range(nch)]
    lgc_rows = jnp.concatenate([jnp.broadcast_to(l, (c, l.shape[1])) for l in lgc], axis=0)
    einv = jnp.exp(-lg)
    ec = jnp.exp(lgc_rows - lg)
    gc = [jnp.exp(l) for l in lgc]
    rt = rs * jnp.exp(lg)
    at = -kk * jnp.exp(lg - ld)
    bt = beta * einv
    kt = kp * einv
    bh = beta * ec
    kh = kp * ec

    prs = range(PAIRS)
    sls = [slice(LANES * p, LANES * (p + 1)) for p in prs]
    units = [(slice(c * ci, c * (ci + 1)), sl) for ci in range(nch) for sl in sls]
    nu = range(len(units))
    v_u = [vs[rw, sl] for rw, sl in units]
    lhs1 = [jnp.concatenate([at[rw, sl], rt[rw, sl]], axis=0).astype(BF16) for rw, sl in units]
    rhs1 = [jnp.concatenate([bd_stack(bt[rw, sl]), bd_stack(kt[rw, sl])], axis=0) for rw, sl in units]
    pm = [_dot_nt(lhs1[i], rhs1[i]) for i in nu]
    lab = [jnp.where(strict, m[0:c, 0:LANES], 0.0) for m in pm]
    lm = [jnp.concatenate([jnp.where(strict, m[0:c, LANES:2 * LANES], 0.0),
                           jnp.where(incl, m[c:2 * c, LANES:2 * LANES], 0.0)], axis=0).astype(BF16)
          for m in pm]
    mrb = [jnp.where(incl, m[c:2 * c, 0:LANES], 0.0).astype(BF16) for m in pm]
    lv = [_dot(lm[i], bd_stack(v_u[i])) for i in nu]

    pw = [_dot(l.astype(BF16), bd_stack(l)) for l in lab]
    tinv = [eye2 + l for l in lab]
    for _ in range(4):
        res = [_dot(pw[i].astype(BF16),
                    jnp.concatenate([bd_stack(tinv[i]), bd_stack(pw[i])], axis=1)) for i in nu]
        tinv = [tinv[i] + res[i][:, 0:LANES] for i in nu]
        pw = [res[i][:, LANES:2 * LANES] for i in nu]
    tinv = [(tinv[i] + _dot(pw[i].astype(BF16), bd_stack(tinv[i]))).astype(BF16) for i in nu]
    bk = [jnp.concatenate([bh[rw, sl], kh[rw, sl]], axis=0).astype(BF16) for rw, sl in units]

    st = [st_scr[p] for p in prs]
    y_rows = []
    for ci in range(nch):
        ix = [ci * PAIRS + p for p in prs]
        xh = [_dot_nt(lhs1[i], st[p].astype(BF16)) for p, i in zip(prs, ix)]
        x = [xh[p][0:c] + lv[i][0:c] for p, i in zip(prs, ix)]
        u = [_dot(tinv[i], bd_stack(x[p])) for p, i in zip(prs, ix)]
        ys = [xh[p][c:2 * c] + lv[i][c:2 * c] + _dot(mrb[i], bd_stack(u[p])) for p, i in zip(prs, ix)]
        uv = [jnp.concatenate([u[p], v_u[i]], axis=0).astype(BF16) for p, i in zip(prs, ix)]
        ds = [_dot_tn(uv[p], bk[i]) for p, i in zip(prs, ix)]
        st = [st[p] * gc[ci][:, sls[p]] + jnp.where(bd, ds[p], 0.0) for p in prs]
        y_rows.append(jnp.concatenate(ys, axis=1))
    for p in prs:
        st_scr[p] = st[p]

    y = jnp.concatenate(y_rows, axis=0)
    inv_n = 1.0 / RWKV_HEAD
    mean = segsum(y) * inv_n
    d = y - mean
    var = segsum(d * d) * inv_n
    yn = d * lax.rsqrt(var + GN_EPS) * gnw_ref[...] + gnb_ref[...]
    o_ref[...] = ((yn + bonus * vs) * gate).astype(o_ref.dtype)


def _rwkv(proj, bsz, seq, mu_r, mu_k, mu_v, mu_sm, w0, a0, k_k, k_a, r_k, gn_w, gn_b, w2p, a2p, g2p,
          nch=2):
    t = proj.shape[0]
    w = RWKV_WIDTH
    rows = CHUNK * nch
    nc = seq // rows
    row = lambda b, c: b * nc + c
    vec = lambda n: pl.BlockSpec((1, n), lambda b, c: (0, 0))
    full = lambda a: pl.BlockSpec(a.shape, lambda b, c: (0, 0))
    return pl.pallas_call(
        _rwkv_kernel,
        grid=(bsz, nc),
        in_specs=[
            pl.BlockSpec((rows, w), lambda b, c: (row(b, c), 0)),
            pl.BlockSpec((rows, w), lambda b, c: (row(b, c), 1)),
            pl.BlockSpec((rows, w), lambda b, c: (row(b, c), 2)),
            pl.BlockSpec((rows, SMALL), lambda b, c: (row(b, c), COL_SMALL)),
            vec(w), vec(w), vec(w), vec(SMALL),
            vec(w), vec(w), vec(w), vec(w), vec(w), vec(w), vec(w),
            full(w2p), full(a2p), full(g2p),
        ],
        out_specs=pl.BlockSpec((rows, w), lambda b, c: (row(b, c), 0)),
        out_shape=jax.ShapeDtypeStruct((t, w), BF16),
        scratch_shapes=[
            pltpu.VMEM((PAIRS, LANES, LANES), F32),
            pltpu.VMEM((1, w), F32), pltpu.VMEM((1, w), F32), pltpu.VMEM((1, w), F32),
            pltpu.VMEM((1, SMALL), F32),
        ],
        compiler_params=_cparams(("parallel", "arbitrary")),
        name="rwkv",
    )(proj, proj, proj, proj, mu_r, mu_k, mu_v, mu_sm, w0, a0, k_k, k_a, r_k, gn_w, gn_b,
      w2p, a2p, g2p)


def _rope_tables(pos_ref, invf_ref):
    ang = pos_ref[...].astype(F32) * invf_ref[...]
    keep = lax.broadcasted_iota(jnp.int32, ang.shape, 1) < QK_ROPE
    return jnp.where(keep, jnp.cos(ang), 0.0), jnp.where(keep, jnp.sin(ang), 0.0)


def _mla_q_kernel(cq_ref, pos_ref, invf_ref, g_ref, w1_ref, w2_ref, q_ref):
    h = _rms(cq_ref[...], g_ref[...]).astype(BF16)
    z1 = _dot(h, w1_ref[...])
    z2 = _dot(h, w2_ref[...])
    cosm, sinm = _rope_tables(pos_ref, invf_ref)
    hw = QK_NOPE + LANES
    for hd in range(MLA_HEADS):
        q_ref[:, hw * hd:hw * hd + QK_NOPE] = z1[:, hw * hd:hw * hd + QK_NOPE].astype(BF16)
        pe = z1[:, hw * hd + QK_NOPE:hw * (hd + 1)] * cosm + z2[:, LANES * hd:LANES * (hd + 1)] * sinm
        q_ref[:, hw * hd + QK_NOPE:hw * (hd + 1)] = pe.astype(BF16)


def _mla_q(proj, pos, invf, g, w1, w2, tm=512):
    t = proj.shape[0]
    n = MLA_HEADS * (QK_NOPE + LANES)
    return pl.pallas_call(
        _mla_q_kernel,
        grid=(t // tm,),
        in_specs=[
            pl.BlockSpec((tm, Q_LORA), lambda i: (i, COL_CQ)),
            pl.BlockSpec((tm, 1), lambda i: (i, 0)),
            pl.BlockSpec((1, LANES), lambda i: (0, 0)),
            pl.BlockSpec((1, Q_LORA), lambda i: (0, 0)),
            pl.BlockSpec(w1.shape, lambda i: (0, 0)),
            pl.BlockSpec(w2.shape, lambda i: (0, 0)),
        ],
        out_specs=pl.BlockSpec((tm, n), lambda i: (i, 0)),
        out_shape=jax.ShapeDtypeStruct((t, n), BF16),
        compiler_params=_cparams(("parallel",)),
        name="mla_q",
    )(proj, pos, invf, g, w1, w2)


def _mla_kv_kernel(ckv_ref, sm_ref, pos_ref, invf_ref, g_ref, w_ref, k_ref, v_ref):
    h = _rms(ckv_ref[...], g_ref[...]).astype(BF16)
    z = _dot(h, w_ref[...])
    cosm, sinm = _rope_tables(pos_ref, invf_ref)
    kpe = (sm_ref[:, 2 * LANES:3 * LANES] * cosm + sm_ref[:, 3 * LANES:4 * LANES] * sinm).astype(BF16)
    hw = QK_NOPE + LANES
    for hd in range(MLA_HEADS):
        k_ref[:, hw * hd:hw * hd + QK_NOPE] = z[:, QK_NOPE * hd:QK_NOPE * (hd + 1)].astype(BF16)
        k_ref[:, hw * hd + QK_NOPE:hw * (hd + 1)] = kpe
    v_ref[...] = z[:, MLA_HEADS * QK_NOPE:].T.astype(BF16)


def _mla_kv(proj, pos, invf, g, w, tm=512):
    t = proj.shape[0]
    nk = MLA_HEADS * (QK_NOPE + LANES)
    nv = MLA_HEADS * V_HEAD
    return pl.pallas_call(
        _mla_kv_kernel,
        grid=(t // tm,),
        in_specs=[
            pl.BlockSpec((tm, KV_LORA), lambda i: (i, COL_CKV)),
            pl.BlockSpec((tm, SMALL), lambda i: (i, COL_SMALL)),
            pl.BlockSpec((tm, 1), lambda i: (i, 0)),
            pl.BlockSpec((1, LANES), lambda i: (0, 0)),
            pl.BlockSpec((1, KV_LORA), lambda i: (0, 0)),
            pl.BlockSpec(w.shape, lambda i: (0, 0)),
        ],
        out_specs=[
            pl.BlockSpec((tm, nk), lambda i: (i, 0)),
            pl.BlockSpec((nv, tm), lambda i: (0, i)),
        ],
        out_shape=[jax.ShapeDtypeStruct((t, nk), BF16), jax.ShapeDtypeStruct((nv, t), BF16)],
        compiler_params=_cparams(("parallel",)),
        name="mla_kv",
    )(proj, proj, pos, invf, g, w)


ATTN_STRIP = 64


def _attn_kernel(q_ref, k_ref, vt_ref, o_ref, *, blk, scale, nh):
    qi = pl.program_id(2)
    hw = QK_NOPE + LANES
    hds = range(nh)
    c2 = scale * math.log2(math.e)
    nst = blk // ATTN_STRIP

    def step(j, carry, masked):
        m, l, acc = carry
        start = pl.multiple_of(j * blk, blk)

        def scores(h):
            return _dot_nt(k_ref[pl.ds(start, blk), hw * h:hw * (h + 1)], q_ref[:, hw * h:hw * (h + 1)])

        def softmax(h, s):
            if masked:
                key = lax.broadcasted_iota(jnp.int32, s.shape, 0)
                qry = lax.broadcasted_iota(jnp.int32, s.shape, 1)
                s = jnp.where(key <= qry, s, NEG_INF)
            m_new = jnp.maximum(m[h], jnp.max(s, axis=0, keepdims=True))
            alpha = jnp.exp2((m[h] - m_new) * c2)
            part = jnp.zeros((SUBLANES, blk), F32)
            strips = []
            for r in range(nst):
                p = jnp.exp2((s[ATTN_STRIP * r:ATTN_STRIP * (r + 1)] - m_new) * c2)
                part = part + jnp.sum(p.reshape(ATTN_STRIP // SUBLANES, SUBLANES, blk), axis=0)
                strips.append(p.astype(BF16))
            l_new = alpha * l[h] + jnp.sum(part, axis=0, keepdims=True)
            return m_new, l_new, alpha, jnp.concatenate(strips, axis=0)

        def values(h, alpha, p):
            return alpha * acc[h] + _dot(vt_ref[V_HEAD * h:V_HEAD * (h + 1), pl.ds(start, blk)], p)

        m_out, l_out, acc_out = [], [], []
        s_next = scores(0)
        prev = None
        for h in hds:
            s_cur = s_next
            if h + 1 < nh:
                s_next = scores(h + 1)
            m_new, l_new, alpha, p = softmax(h, s_cur)
            m_out.append(m_new)
            l_out.append(l_new)
            if prev is not None:
                acc_out.append(values(*prev))
            prev = (h, alpha, p)
        acc_out.append(values(*prev))
        return m_out, l_out, acc_out

    init = ([jnp.full((1, blk), NEG_INF, F32) for _ in hds], [jnp.zeros((1, blk), F32) for _ in hds],
            [jnp.zeros((V_HEAD, blk), F32) for _ in hds])
    carry = lax.fori_loop(0, qi, lambda j, cr: step(j, cr, False), init)
    _, l, acc = step(qi, carry, True)
    for h in hds:
        o_ref[:, V_HEAD * h:V_HEAD * (h + 1)] = (acc[h] / l[h]).T.astype(o_ref.dtype)


def _attn(q, k, v, bsz, seq, blk=512, nh=4):
    t = q.shape[0]
    nq = seq // blk
    hw = QK_NOPE + LANES
    kern = functools.partial(_attn_kernel, blk=blk, scale=(QK_NOPE + QK_ROPE) ** -0.5, nh=nh)
    return pl.pallas_call(
        kern,
        grid=(bsz, MLA_HEADS // nh, nq),
        in_specs=[
            pl.BlockSpec((blk, hw * nh), lambda b, h, i: (b * nq + i, h)),
            pl.BlockSpec((seq, hw * nh), lambda b, h, i: (b, h)),
            pl.BlockSpec((V_HEAD * nh, seq), lambda b, h, i: (h, b)),
        ],
        out_specs=pl.BlockSpec((blk, V_HEAD * nh), lambda b, h, i: (b * nq + i, h)),
        out_shape=jax.ShapeDtypeStruct((t, MLA_HEADS * V_HEAD), BF16),
        compiler_params=_cparams(("parallel", "parallel", "arbitrary")),
        name="attn",
    )(q, k, v)


def _outproj_kernel(yr_ref, ym_ref, wt_ref, wb_ref, x_ref, g_ref, x1_ref, h2_ref):
    x1 = x_ref[...] + _dot(yr_ref[...], wt_ref[...]) + _dot(ym_ref[...], wb_ref[...])
    x1_ref[...] = x1
    h2_ref[...] = _rms(x1, g_ref[...]).astype(BF16)


def _outproj(yr, ym, wt, wb, x2, g, tm=512):
    t, d = x2.shape
    return pl.pallas_call(
        _outproj_kernel,
        grid=(t // tm,),
        in_specs=[
            pl.BlockSpec((tm, RWKV_WIDTH), lambda i: (i, 0)),
            pl.BlockSpec((tm, MLA_WIDTH), lambda i: (i, 0)),
            pl.BlockSpec(wt.shape, lambda i: (0, 0), pipeline_mode=pl.Buffered(1)),
            pl.BlockSpec(wb.shape, lambda i: (0, 0), pipeline_mode=pl.Buffered(1)),
            pl.BlockSpec((tm, d), lambda i: (i, 0)),
            pl.BlockSpec((1, d), lambda i: (0, 0)),
        ],
        out_specs=[pl.BlockSpec((tm, d), lambda i: (i, 0)), pl.BlockSpec((tm, d), lambda i: (i, 0))],
        out_shape=[jax.ShapeDtypeStruct((t, d), F32), jax.ShapeDtypeStruct((t, d), BF16)],
        compiler_params=_cparams(("parallel",)),
        name="outproj",
    )(yr, ym, wt, wb, x2, g)


HALO = 16


MXU_N = 256


def _shift_down(x, first):
    r = pltpu.roll(x, 1, 0)
    top = r[0:SUBLANES]
    row = lax.broadcasted_iota(jnp.int32, top.shape, 0)
    return jnp.concatenate([jnp.where(row == 0, first, top), r[SUBLANES:]], axis=0)


def _ffn_act_kernel(h_ref, halo_ref, wg_ref, wu_ref, cw_ref, cb_ref, a_ref, hh_scr, *, tm, seq):
    i = pl.program_id(1)
    keep = jnp.where((i * tm) % seq != 0, 1.0, 0.0)
    hh_scr[0:HALO, :] = (halo_ref[...] * keep.astype(BF16)).astype(BF16)
    hh_scr[HALO:, :] = h_ref[...]
    n_sub = wg_ref.shape[1] // MXU_N

    def dots(s):
        cs = slice(MXU_N * s, MXU_N * (s + 1))
        return _dot(hh_scr[...], wg_ref[:, cs]), _dot(h_ref[...], wu_ref[:, cs])

    nxt = dots(0)
    for s in range(n_sub):
        cs = slice(MXU_N * s, MXU_N * (s + 1))
        ge, u = nxt
        if s + 1 < n_sub:
            nxt = dots(s + 1)
        c0, c1, c2 = cw_ref[0:1, cs], cw_ref[1:2, cs], cw_ref[2:3, cs]
        g = ge[HALO:]
        gm1 = ge[HALO - 1:HALO]
        gm2 = ge[HALO - 2:HALO - 1]
        w = c1 * g + _shift_down(c0 * g, c0 * gm1)
        conv = c2 * g + _shift_down(w, c1 * gm1 + c0 * gm2) + cb_ref[:, cs]
        a_ref[:, cs] = (conv * _sigmoid(conv) * u).astype(BF16)


def _ffn_act(h2, wg, wu, cw, cb, seq, tm=512, n_col=2):
    t, d = h2.shape
    f = wg.shape[1]
    tf = f // n_col
    kern = functools.partial(_ffn_act_kernel, tm=tm, seq=seq)
    hb = tm // HALO
    once = pl.Buffered(1)
    return pl.pallas_call(
        kern,
        grid=(n_col, t // tm),
        in_specs=[
            pl.BlockSpec((tm, d), lambda j, i: (i, 0)),
            pl.BlockSpec((HALO, d), lambda j, i: (jnp.maximum(i * hb - 1, 0), 0)),
            pl.BlockSpec((d, tf), lambda j, i: (0, j), pipeline_mode=once),
            pl.BlockSpec((d, tf), lambda j, i: (0, j), pipeline_mode=once),
            pl.BlockSpec((3, tf), lambda j, i: (0, j)),
            pl.BlockSpec((1, tf), lambda j, i: (0, j)),
        ],
        out_specs=pl.BlockSpec((tm, tf), lambda j, i: (i, j)),
        out_shape=jax.ShapeDtypeStruct((t, f), BF16),
        scratch_shapes=[pltpu.VMEM((tm + HALO, d), BF16)],
        compiler_params=_cparams(("parallel", "parallel")),
        name="ffn_act",
    )(h2, h2, wg, wu, cw, cb)


def _ffn_down_kernel(a_ref, wd_ref, x1_ref, gf_ref, o_ref):
    o_ref[...] = _rms(x1_ref[...] + _dot(a_ref[...], wd_ref[...]), gf_ref[...])


def _ffn_down(act, wd, x1, gf, tm=256):
    t, d = x1.shape
    f = act.shape[1]
    return pl.pallas_call(
        _ffn_down_kernel,
        grid=(t // tm,),
        in_specs=[
            pl.BlockSpec((tm, f), lambda i: (i, 0)),
            pl.BlockSpec((f, d), lambda i: (0, 0), pipeline_mode=pl.Buffered(1)),
            pl.BlockSpec((tm, d), lambda i: (i, 0)),
            pl.BlockSpec((1, d), lambda i: (0, 0)),
        ],
        out_specs=pl.BlockSpec((tm, d), lambda i: (i, 0)),
        out_shape=jax.ShapeDtypeStruct((t, d), F32),
        compiler_params=_cparams(("parallel",)),
        name="ffn_down",
    )(act, wd, x1, gf)


def _pack_in(w_in, mu):
    d = w_in.shape[0]
    o = 3 * RWKV_WIDTH
    wl = w_in[:, o:o + DECAY_LORA]
    al = w_in[:, o + DECAY_LORA:o + DECAY_LORA + AAA_LORA]
    og = o + DECAY_LORA + AAA_LORA
    gl = w_in[:, og:og + GATE_LORA]
    oq = og + GATE_LORA
    cq = w_in[:, oq:oq + Q_LORA]
    ckv = w_in[:, oq + Q_LORA:oq + Q_LORA + KV_LORA]
    kpe = w_in[:, oq + Q_LORA + KV_LORA:]
    hr = QK_ROPE // 2
    kpe_rot = jnp.concatenate([-kpe[:, hr:], kpe[:, :hr]], axis=1)
    z = lambda n: jnp.zeros((d, n), w_in.dtype)
    rest = GATE_LORA - LANES
    small = jnp.concatenate(
        [wl, al, gl[:, :LANES], kpe, gl[:, LANES:], z(LANES - QK_ROPE - rest), kpe_rot, z(LANES - QK_ROPE)],
        axis=1)
    w_p = jnp.concatenate([w_in[:, :o], cq, ckv, small], axis=1).astype(BF16)
    zm = lambda n: jnp.zeros((n,), mu.dtype)
    mu_sm = jnp.concatenate(
        [mu[o:og], mu[og:og + LANES], zm(QK_ROPE), mu[og + LANES:og + GATE_LORA],
         zm(LANES - QK_ROPE - rest), zm(LANES)])
    return w_p, mu[:RWKV_WIDTH], mu[RWKV_WIDTH:2 * RWKV_WIDTH], mu[2 * RWKV_WIDTH:o], mu_sm


def _pack_lora(w2, a2, g2):
    n = w2.shape[1]
    z = lambda r: jnp.zeros((r, n), w2.dtype)
    rest = GATE_LORA - LANES
    w2p = jnp.concatenate([w2, z(LANES - DECAY_LORA)], axis=0)
    a2p = jnp.concatenate([z(DECAY_LORA), a2], axis=0)
    g2p = jnp.concatenate([g2[:LANES], z(QK_ROPE), g2[LANES:], z(LANES - QK_ROPE - rest)], axis=0)
    return w2p.astype(BF16), a2p.astype(BF16), g2p.astype(BF16)


def _pack_q(w_uq):
    d = w_uq.shape[0]
    hw = QK_NOPE + QK_ROPE
    hr = QK_ROPE // 2
    w = w_uq.reshape(d, MLA_HEADS, hw)
    pe = w[:, :, QK_NOPE:]
    z = jnp.zeros((d, MLA_HEADS, LANES - QK_ROPE), w_uq.dtype)
    w1 = jnp.concatenate([w[:, :, :QK_NOPE], pe, z], axis=2).reshape(d, -1)
    w2 = jnp.concatenate([-pe[:, :, hr:], pe[:, :, :hr], z], axis=2).reshape(d, -1)
    return w1.astype(BF16), w2.astype(BF16)


def _pack_kv(w_ukv):
    d = w_ukv.shape[0]
    w = w_ukv.reshape(d, MLA_HEADS, QK_NOPE + V_HEAD)
    return jnp.concatenate([w[:, :, :QK_NOPE].reshape(d, -1), w[:, :, QK_NOPE:].reshape(d, -1)],
                           axis=1).astype(BF16)


def kernel(x, positions, attn_norm_g, w_in, rwkv_mu, rwkv_w0, rwkv_w2, rwkv_a0, rwkv_a2, rwkv_g2,
           rwkv_k_k, rwkv_k_a, rwkv_r_k, rwkv_gn_w, rwkv_gn_b, mla_q_norm_g, mla_w_uq,
           mla_kv_norm_g, mla_w_ukv, w_out, ffn_norm_g, ffn_w_gate, ffn_w_up, ffn_conv_w,
           ffn_conv_b, ffn_w_down, final_norm_g):
    bsz, seq, d = x.shape
    t = bsz * seq
    depth = w_in.shape[0]
    row = lambda a: a.reshape(1, -1)
    xc = x.reshape(t, d)
    pos = positions.reshape(t, 1)
    hr = QK_ROPE // 2
    inv_freq = ROPE_THETA ** (-jnp.arange(hr, dtype=F32) / hr)
    invf = jnp.concatenate([inv_freq, inv_freq, jnp.zeros((LANES - QK_ROPE,), F32)]).reshape(1, LANES)

    assert depth == 1, "the final RMSNorm is fused into the (single) layer's ffn kernel"
    l = 0
    w_p, mu_r, mu_k, mu_v, mu_sm = _pack_in(w_in[l], rwkv_mu[l])
    w2p, a2p, g2p = _pack_lora(rwkv_w2[l], rwkv_a2[l], rwkv_g2[l])
    wq1, wq2 = _pack_q(mla_w_uq[l])
    wkv = _pack_kv(mla_w_ukv[l])

    proj = _inproj(xc, row(attn_norm_g[l]), w_p)
    y_r = _rwkv(proj, bsz, seq, row(mu_r), row(mu_k), row(mu_v), row(mu_sm),
                row(rwkv_w0[l]), row(rwkv_a0[l]), row(rwkv_k_k[l]), row(rwkv_k_a[l]),
                row(rwkv_r_k[l]), row(rwkv_gn_w[l]), row(rwkv_gn_b[l]), w2p, a2p, g2p)
    q = _mla_q(proj, pos, invf, row(mla_q_norm_g[l]), wq1, wq2)
    k, v = _mla_kv(proj, pos, invf, row(mla_kv_norm_g[l]), wkv)
    y_m = _attn(q, k, v, bsz, seq)
    wo = w_out[l].astype(BF16)
    x1, h2 = _outproj(y_r, y_m, wo[:RWKV_WIDTH], wo[RWKV_WIDTH:], xc, row(ffn_norm_g[l]))
    act = _ffn_act(h2, ffn_w_gate[l].astype(BF16), ffn_w_up[l].astype(BF16), ffn_conv_w[l],
                   row(ffn_conv_b[l]), seq)
    out = _ffn_down(act, ffn_w_down[l].astype(BF16), x1, row(final_norm_g))
    return out.reshape(bsz, seq, d)
```

```python
import functools
import math

import jax
import jax.numpy as jnp
from jax import lax
from jax.experimental import pallas as pl
from jax.experimental.pallas import tpu as pltpu

F32 = jnp.float32
BF16 = jnp.bfloat16

D_MODEL = 2048
RWKV_HEAD = 64
RWKV_WIDTH = 1024
RWKV_HEADS = 16
DECAY_LORA = 64
AAA_LORA = 64
GATE_LORA = 160
GN_EPS = 64e-5
QK_NOPE = 128
QK_ROPE = 64
V_HEAD = 128
MLA_WIDTH = 1024
MLA_HEADS = 8
Q_LORA = 512
KV_LORA = 512
ROPE_THETA = 10000.0
D_FF = 5632
NORM_EPS = 1e-6
NEG_INF = -1e30

LANES = 128
SUBLANES = 8
SMALL = 512
D_IN_P = 3 * RWKV_WIDTH + Q_LORA + KV_LORA + SMALL
COL_CQ = 3 * RWKV_WIDTH // Q_LORA
COL_CKV = COL_CQ + 1
COL_SMALL = COL_CKV + 1
CHUNK = 64
PAIRS = RWKV_WIDTH // LANES
VMEM_LIMIT = 56 * 1024 * 1024


def _cparams(sem):
    return pltpu.CompilerParams(dimension_semantics=sem, vmem_limit_bytes=VMEM_LIMIT)


def _rms(x, g):
    ms = jnp.mean(x * x, axis=-1, keepdims=True)
    return x * lax.rsqrt(ms + NORM_EPS) * g


def _sigmoid(x):
    return 1.0 / (1.0 + jnp.exp(-x))


def _dot(a, b):
    return jnp.dot(a, b, preferred_element_type=F32)


def _dot_nt(a, b):
    return lax.dot_general(a, b, (((1,), (1,)), ((), ())), preferred_element_type=F32)


def _dot_tn(a, b):
    return lax.dot_general(a, b, (((0,), (0,)), ((), ())), preferred_element_type=F32)


def _inproj_kernel(x_ref, g_ref, wa_ref, wb_ref, o_ref):
    h = _rms(x_ref[...], g_ref[...]).astype(BF16)
    na = wa_ref.shape[1]
    o_ref[:, :na] = _dot(h, wa_ref[...])
    o_ref[:, na:] = _dot(h, wb_ref[...])


def _inproj(x2, g, w_a, w_b, tm=512):
    t, d = x2.shape
    n = w_a.shape[1] + w_b.shape[1]
    once = pl.Buffered(1)
    return pl.pallas_call(
        _inproj_kernel,
        grid=(t // tm,),
        in_specs=[
            pl.BlockSpec((tm, d), lambda i: (i, 0)),
            pl.BlockSpec((1, d), lambda i: (0, 0)),
            pl.BlockSpec(w_a.shape, lambda i: (0, 0), pipeline_mode=once),
            pl.BlockSpec(w_b.shape, lambda i: (0, 0), pipeline_mode=once),
        ],
        out_specs=pl.BlockSpec((tm, n), lambda i: (i, 0)),
        out_shape=jax.ShapeDtypeStruct((t, n), F32),
        compiler_params=_cparams(("parallel",)),
        name="inproj",
    )(x2, g, w_a, w_b)


def _rwkv_kernel(r_ref, k_ref, v_ref, sm_ref, mur_ref, muk_ref, muv_ref, musm_ref,
                 w0_ref, a0_ref, kk_ref, ka_ref, rk_ref, gnw_ref, gnb_ref,
                 w2_ref, a2_ref, g2_ref, o_ref,
                 st_scr, pr_scr, pk_scr, pv_scr, psm_scr):
    c = CHUNK
    half = RWKV_HEAD
    rows = r_ref.shape[0]
    nch = rows // c

    @pl.when(pl.program_id(1) == 0)
    def _():
        st_scr[...] = jnp.zeros_like(st_scr)
        pr_scr[...] = jnp.zeros_like(pr_scr)
        pk_scr[...] = jnp.zeros_like(pk_scr)
        pv_scr[...] = jnp.zeros_like(pv_scr)
        psm_scr[...] = jnp.zeros_like(psm_scr)

    def shift(x_ref, p_scr, mu_ref, ci):
        x = x_ref[c * ci:c * (ci + 1), :]
        prev = p_scr[...] if ci == 0 else x_ref[c * ci - 1:c * ci, :]
        row = lax.broadcasted_iota(jnp.int32, x.shape, 0)
        xp = jnp.where(row == 0, prev, pltpu.roll(x, 1, 0))
        return x + (xp - x) * mu_ref[...]

    ri = lax.broadcasted_iota(jnp.int32, (c, LANES), 0)
    li = lax.broadcasted_iota(jnp.int32, (c, LANES), 1)
    lj = li & (half - 1)
    lane_a = li < half
    strict = lj < ri
    incl = lj <= ri
    eye2 = (lj == ri).astype(F32)
    r2 = lax.broadcasted_iota(jnp.int32, (LANES, LANES), 0)
    l2 = lax.broadcasted_iota(jnp.int32, (LANES, LANES), 1)
    bd = (r2 < half) == (l2 < half)
    bd_ones = bd.astype(BF16)
    tr = lax.broadcasted_iota(jnp.int32, (c, c), 0)
    tc = lax.broadcasted_iota(jnp.int32, (c, c), 1)
    tril = (tc <= tr).astype(BF16)

    def bd_stack(x):
        xb = x.astype(BF16)
        z = jnp.zeros_like(xb)
        return jnp.concatenate([jnp.where(lane_a, xb, z), jnp.where(lane_a, z, xb)], axis=0)

    def segsum(x):
        xs = jnp.concatenate([x[:, LANES * g:LANES * (g + 1)] for g in range(PAIRS)], axis=0)
        hi = xs.astype(BF16)
        lo = (xs - hi.astype(F32)).astype(BF16)
        s = _dot(hi, bd_ones) + _dot(lo, bd_ones)
        return jnp.concatenate([s[c * g:c * (g + 1), :] for g in range(PAIRS)], axis=1)

    def cumsum_rows(x):
        hi = x.astype(BF16)
        r1 = x - hi.astype(F32)
        mid = r1.astype(BF16)
        lo = (r1 - mid.astype(F32)).astype(BF16)
        return _dot(tril, hi) + _dot(tril, mid) + _dot(tril, lo)

    prs = range(PAIRS)
    sls = [slice(LANES * p, LANES * (p + 1)) for p in prs]
    inv_n = 1.0 / RWKV_HEAD
    state = {"s": [st_scr[p] for p in prs]}
    pre, ind, dep = {}, {}, {}

    def prologue(ci):
        rs = shift(r_ref, pr_scr, mur_ref, ci)
        ks = shift(k_ref, pk_scr, muk_ref, ci)
        vs = shift(v_ref, pv_scr, muv_ref, ci)
        sm = shift(sm_ref, psm_scr, musm_ref, ci)
        g0 = sm[:, 0:LANES]
        zw = _dot(jnp.tanh(g0).astype(BF16), w2_ref[...])
        za = _dot(g0.astype(BF16), a2_ref[...])
        kkr = ks * kk_ref[...]
        nrm = segsum(kkr * kkr)
        gate = _dot(_sigmoid(sm[:, LANES:3 * LANES]).astype(BF16), g2_ref[...])
        yield
        ld = (-math.exp(-0.5)) * _sigmoid(w0_ref[...] + zw)
        lg = cumsum_rows(ld)
        a = _sigmoid(a0_ref[...] + za)
        kk = kkr * lax.rsqrt(jnp.maximum(nrm, 1e-24))
        kp = ks * (1.0 + (a - 1.0) * ka_ref[...])
        beta = kk * a
        bonus = segsum(rs * kp * rk_ref[...])
        yield
        lgc = lg[c - 1:c, :]
        einv = jnp.exp(-lg)
        ec = jnp.exp(lgc - lg)
        rt = rs * jnp.exp(lg)
        at = -kk * jnp.exp(lg - ld)
        pre[ci] = dict(vs=vs, gate=gate, bonus=bonus, gc=jnp.exp(lgc), rt=rt, at=at,
                       bt=beta * einv, kt=kp * einv, bh=beta * ec, kh=kp * ec)
        yield

    def independent(ci):
        q = pre[ci]
        v_p = [q["vs"][:, sl] for sl in sls]
        lhs1 = [jnp.concatenate([q["at"][:, sl], q["rt"][:, sl]], axis=0).astype(BF16) for sl in sls]
        rhs1 = [jnp.concatenate([bd_stack(q["bt"][:, sl]), bd_stack(q["kt"][:, sl])], axis=0) for sl in sls]
        pm = [_dot_nt(lhs1[p], rhs1[p]) for p in prs]
        bk = [jnp.concatenate([q["bh"][:, sl], q["kh"][:, sl]], axis=0).astype(BF16) for sl in sls]
        yield
        lab = [jnp.where(strict, m[0:c, 0:LANES], 0.0) for m in pm]
        lm = [jnp.concatenate([jnp.where(strict, m[0:c, LANES:2 * LANES], 0.0),
                               jnp.where(incl, m[c:2 * c, LANES:2 * LANES], 0.0)], axis=0).astype(BF16)
              for m in pm]
        mrb = [jnp.where(incl, m[c:2 * c, 0:LANES], 0.0).astype(BF16) for m in pm]
        lv = [_dot(lm[p], bd_stack(v_p[p])) for p in prs]
        pw = [_dot(l.astype(BF16), bd_stack(l)) for l in lab]
        tinv = [eye2 + l for l in lab]
        yield
        for _ in range(4):
            res = [_dot(pw[p].astype(BF16),
                        jnp.concatenate([bd_stack(tinv[p]), bd_stack(pw[p])], axis=1)) for p in prs]
            tinv = [tinv[p] + res[p][:, 0:LANES] for p in prs]
            pw = [res[p][:, LANES:2 * LANES] for p in prs]
            yield
        tinv = [(tinv[p] + _dot(pw[p].astype(BF16), bd_stack(tinv[p]))).astype(BF16) for p in prs]
        ind[ci] = dict(v_p=v_p, lhs1=lhs1, bk=bk, mrb=mrb, lv=lv, tinv=tinv)
        yield

    def dependent(ci):
        q, st = ind[ci], state["s"]
        xh = [_dot_nt(q["lhs1"][p], st[p].astype(BF16)) for p in prs]
        yield
        x = [xh[p][0:c] + q["lv"][p][0:c] for p in prs]
        u = [_dot(q["tinv"][p], bd_stack(x[p])) for p in prs]
        yield
        ys = [xh[p][c:2 * c] + q["lv"][p][c:2 * c] + _dot(q["mrb"][p], bd_stack(u[p])) for p in prs]
        uv = [jnp.concatenate([u[p], q["v_p"][p]], axis=0).astype(BF16) for p in prs]
        ds = [_dot_tn(uv[p], q["bk"][p]) for p in prs]
        yield
        gc = pre[ci]["gc"]
        state["s"] = [st[p] * gc[:, sls[p]] + jnp.where(bd, ds[p], 0.0) for p in prs]
        dep[ci] = jnp.concatenate(ys, axis=1)
        yield

    def epilogue(ci):
        y, q = dep[ci], pre[ci]
        mean = segsum(y) * inv_n
        yield
        d = y - mean
        var = segsum(d * d) * inv_n
        yield
        yn = d * lax.rsqrt(var + GN_EPS) * gnw_ref[...] + gnb_ref[...]
        o_ref[c * ci:c * (ci + 1), :] = ((yn + q["bonus"] * q["vs"]) * q["gate"]).astype(o_ref.dtype)
        yield

    phases = (prologue, independent, dependent, epilogue)
    done = object()
    for tick in range(nch + len(phases) - 1):
        live = [ph(tick - k) for k, ph in enumerate(phases) if 0 <= tick - k < nch]
        while live:
            live = [g for g in live if next(g, done) is not done]

    for p in prs:
        st_scr[p] = state["s"][p]
    for x_ref, p_scr in ((r_ref, pr_scr), (k_ref, pk_scr), (v_ref, pv_scr), (sm_ref, psm_scr)):
        p_scr[...] = x_ref[rows - 1:rows, :]


def _rwkv(proj, bsz, seq, mu_r, mu_k, mu_v, mu_sm, w0, a0, k_k, k_a, r_k, gn_w, gn_b, w2p, a2p, g2p,
          nch=4):
    t = proj.shape[0]
    w = RWKV_WIDTH
    rows = CHUNK * nch
    nc = seq // rows
    row = lambda b, c: b * nc + c
    vec = lambda n: pl.BlockSpec((1, n), lambda b, c: (0, 0))
    full = lambda a: pl.BlockSpec(a.shape, lambda b, c: (0, 0))
    return pl.pallas_call(
        _rwkv_kernel,
        grid=(bsz, nc),
        in_specs=[
            pl.BlockSpec((rows, w), lambda b, c: (row(b, c), 0)),
            pl.BlockSpec((rows, w), lambda b, c: (row(b, c), 1)),
            pl.BlockSpec((rows, w), lambda b, c: (row(b, c), 2)),
            pl.BlockSpec((rows, SMALL), lambda b, c: (row(b, c), COL_SMALL)),
            vec(w), vec(w), vec(w), vec(SMALL),
            vec(w), vec(w), vec(w), vec(w), vec(w), vec(w), vec(w),
            full(w2p), full(a2p), full(g2p),
        ],
        out_specs=pl.BlockSpec((rows, w), lambda b, c: (row(b, c), 0)),
        out_shape=jax.ShapeDtypeStruct((t, w), BF16),
        scratch_shapes=[
            pltpu.VMEM((PAIRS, LANES, LANES), F32),
            pltpu.VMEM((1, w), F32), pltpu.VMEM((1, w), F32), pltpu.VMEM((1, w), F32),
            pltpu.VMEM((1, SMALL), F32),
        ],
        compiler_params=_cparams(("parallel", "arbitrary")),
        name="rwkv",
    )(proj, proj, proj, proj, mu_r, mu_k, mu_v, mu_sm, w0, a0, k_k, k_a, r_k, gn_w, gn_b,
      w2p, a2p, g2p)


def _rope_kernel(pos_ref, invf_ref, cos_ref, sin_ref):
    ang = pos_ref[...].astype(F32) * invf_ref[...]
    cos_ref[...] = jnp.cos(ang)
    sin_ref[...] = jnp.sin(ang)


def _rope(pos_d, invf_d, tr=1024):
    n = pos_d.shape[0]
    spec = pl.BlockSpec((tr, LANES), lambda i: (i, 0))
    return pl.pallas_call(
        _rope_kernel,
        grid=(n // tr,),
        in_specs=[spec, pl.BlockSpec((1, LANES), lambda i: (0, 0))],
        out_specs=[spec, spec],
        out_shape=[jax.ShapeDtypeStruct((n, LANES), F32)] * 2,
        compiler_params=_cparams(("parallel",)),
        name="rope",
    )(pos_d, invf_d)


def _mla_q_kernel(cq_ref, cos_ref, sin_ref, g_ref, w1_ref, w2_ref, q_ref):
    h = _rms(cq_ref[...], g_ref[...]).astype(BF16)
    z1 = _dot(h, w1_ref[...])
    z2 = _dot(h, w2_ref[...])
    cosm, sinm = cos_ref[...], sin_ref[...]
    hw = QK_NOPE + LANES
    for hd in range(MLA_HEADS):
        q_ref[:, hw * hd:hw * hd + QK_NOPE] = z1[:, hw * hd:hw * hd + QK_NOPE].astype(BF16)
        pe = z1[:, hw * hd + QK_NOPE:hw * (hd + 1)] * cosm + z2[:, LANES * hd:LANES * (hd + 1)] * sinm
        q_ref[:, hw * hd + QK_NOPE:hw * (hd + 1)] = pe.astype(BF16)


def _mla_q(proj, cosm, sinm, g, w1, w2, tm=512):
    t = proj.shape[0]
    n = MLA_HEADS * (QK_NOPE + LANES)
    return pl.pallas_call(
        _mla_q_kernel,
        grid=(t // tm,),
        in_specs=[
            pl.BlockSpec((tm, Q_LORA), lambda i: (i, COL_CQ)),
            pl.BlockSpec((tm, LANES), lambda i: (i, 0)),
            pl.BlockSpec((tm, LANES), lambda i: (i, 0)),
            pl.BlockSpec((1, Q_LORA), lambda i: (0, 0)),
            pl.BlockSpec(w1.shape, lambda i: (0, 0)),
            pl.BlockSpec(w2.shape, lambda i: (0, 0)),
        ],
        out_specs=pl.BlockSpec((tm, n), lambda i: (i, 0)),
        out_shape=jax.ShapeDtypeStruct((t, n), BF16),
        compiler_params=_cparams(("parallel",)),
        name="mla_q",
    )(proj, cosm, sinm, g, w1, w2)


def _mla_kv_kernel(ckv_ref, sm_ref, cos_ref, sin_ref, g_ref, w_ref, k_ref, v_ref):
    h = _rms(ckv_ref[...], g_ref[...]).astype(BF16)
    z = _dot(h, w_ref[...])
    cosm, sinm = cos_ref[...], sin_ref[...]
    kpe = (sm_ref[:, 2 * LANES:3 * LANES] * cosm + sm_ref[:, 3 * LANES:4 * LANES] * sinm).astype(BF16)
    hw = QK_NOPE + LANES
    for hd in range(MLA_HEADS):
        k_ref[:, hw * hd:hw * hd + QK_NOPE] = z[:, QK_NOPE * hd:QK_NOPE * (hd + 1)].astype(BF16)
        k_ref[:, hw * hd + QK_NOPE:hw * (hd + 1)] = kpe
    v_ref[...] = z[:, MLA_HEADS * QK_NOPE:].T.astype(BF16)


def _mla_kv(proj, cosm, sinm, g, w, tm=512):
    t = proj.shape[0]
    nk = MLA_HEADS * (QK_NOPE + LANES)
    nv = MLA_HEADS * V_HEAD
    return pl.pallas_call(
        _mla_kv_kernel,
        grid=(t // tm,),
        in_specs=[
            pl.BlockSpec((tm, KV_LORA), lambda i: (i, COL_CKV)),
            pl.BlockSpec((tm, SMALL), lambda i: (i, COL_SMALL)),
            pl.BlockSpec((tm, LANES), lambda i: (i, 0)),
            pl.BlockSpec((tm, LANES), lambda i: (i, 0)),
            pl.BlockSpec((1, KV_LORA), lambda i: (0, 0)),
            pl.BlockSpec(w.shape, lambda i: (0, 0)),
        ],
        out_specs=[
            pl.BlockSpec((tm, nk), lambda i: (i, 0)),
            pl.BlockSpec((nv, tm), lambda i: (0, i)),
        ],
        out_shape=[jax.ShapeDtypeStruct((t, nk), BF16), jax.ShapeDtypeStruct((nv, t), BF16)],
        compiler_params=_cparams(("parallel",)),
        name="mla_kv",
    )(proj, proj, cosm, sinm, g, w)


ATTN_STRIP = 64


def _attn_kernel(q_ref, k_ref, vt_ref, o_ref, *, blk, scale, nh):
    qi = pl.program_id(2)
    hw = QK_NOPE + LANES
    hds = range(nh)
    c2 = scale * math.log2(math.e)
    nst = blk // ATTN_STRIP

    def step(j, carry, masked):
        m, l, acc = carry
        start = pl.multiple_of(j * blk, blk)

        def scores(h):
            return _dot_nt(k_ref[pl.ds(start, blk), hw * h:hw * (h + 1)], q_ref[:, hw * h:hw * (h + 1)])

        def softmax(h, s):
            if masked:
                key = lax.broadcasted_iota(jnp.int32, s.shape, 0)
                qry = lax.broadcasted_iota(jnp.int32, s.shape, 1)
                s = jnp.where(key <= qry, s, NEG_INF)
            m_new = jnp.maximum(m[h], jnp.max(s, axis=0, keepdims=True))
            alpha = jnp.exp2((m[h] - m_new) * c2)
            part = jnp.zeros((SUBLANES, blk), F32)
            strips = []
            for r in range(nst):
                p = jnp.exp2((s[ATTN_STRIP * r:ATTN_STRIP * (r + 1)] - m_new) * c2)
                part = part + jnp.sum(p.reshape(ATTN_STRIP // SUBLANES, SUBLANES, blk), axis=0)
                strips.append(p.astype(BF16))
            l_new = alpha * l[h] + jnp.sum(part, axis=0, keepdims=True)
            return m_new, l_new, alpha, jnp.concatenate(strips, axis=0)

        def values(h, alpha, p):
            return alpha * acc[h] + _dot(vt_ref[V_HEAD * h:V_HEAD * (h + 1), pl.ds(start, blk)], p)

        m_out, l_out, acc_out = [], [], []
        s_next = scores(0)
        prev = None
        for h in hds:
            s_cur = s_next
            if h + 1 < nh:
                s_next = scores(h + 1)
            m_new, l_new, alpha, p = softmax(h, s_cur)
            m_out.append(m_new)
            l_out.append(l_new)
            if prev is not None:
                acc_out.append(values(*prev))
            prev = (h, alpha, p)
        acc_out.append(values(*prev))
        return m_out, l_out, acc_out

    init = ([jnp.full((1, blk), NEG_INF, F32) for _ in hds], [jnp.zeros((1, blk), F32) for _ in hds],
            [jnp.zeros((V_HEAD, blk), F32) for _ in hds])
    carry = lax.fori_loop(0, qi, lambda j, cr: step(j, cr, False), init)
    _, l, acc = step(qi, carry, True)
    for h in hds:
        o_ref[:, V_HEAD * h:V_HEAD * (h + 1)] = (acc[h] / l[h]).T.astype(o_ref.dtype)


def _attn(q, k, v, bsz, seq, blk=512, nh=4):
    t = q.shape[0]
    nq = seq // blk
    hw = QK_NOPE + LANES
    kern = functools.partial(_attn_kernel, blk=blk, scale=(QK_NOPE + QK_ROPE) ** -0.5, nh=nh)
    return pl.pallas_call(
        kern,
        grid=(bsz, MLA_HEADS // nh, nq),
        in_specs=[
            pl.BlockSpec((blk, hw * nh), lambda b, h, i: (b * nq + i, h)),
            pl.BlockSpec((seq, hw * nh), lambda b, h, i: (b, h)),
            pl.BlockSpec((V_HEAD * nh, seq), lambda b, h, i: (h, b)),
        ],
        out_specs=pl.BlockSpec((blk, V_HEAD * nh), lambda b, h, i: (b * nq + i, h)),
        out_shape=jax.ShapeDtypeStruct((t, MLA_HEADS * V_HEAD), BF16),
        compiler_params=_cparams(("parallel", "parallel", "arbitrary")),
        name="attn",
    )(q, k, v)


def _outproj_kernel(yr_ref, ym_ref, wt_ref, wb_ref, x_ref, g_ref, x1_ref, h2_ref):
    x1 = x_ref[...] + _dot(yr_ref[...], wt_ref[...]) + _dot(ym_ref[...], wb_ref[...])
    x1_ref[...] = x1
    h2_ref[...] = _rms(x1, g_ref[...]).astype(BF16)


def _outproj(yr, ym, wt, wb, x2, g, tm=512):
    t, d = x2.shape
    return pl.pallas_call(
        _outproj_kernel,
        grid=(t // tm,),
        in_specs=[
            pl.BlockSpec((tm, RWKV_WIDTH), lambda i: (i, 0)),
            pl.BlockSpec((tm, MLA_WIDTH), lambda i: (i, 0)),
            pl.BlockSpec(wt.shape, lambda i: (0, 0), pipeline_mode=pl.Buffered(1)),
            pl.BlockSpec(wb.shape, lambda i: (0, 0), pipeline_mode=pl.Buffered(1)),
            pl.BlockSpec((tm, d), lambda i: (i, 0)),
            pl.BlockSpec((1, d), lambda i: (0, 0)),
        ],
        out_specs=[pl.BlockSpec((tm, d), lambda i: (i, 0)), pl.BlockSpec((tm, d), lambda i: (i, 0))],
        out_shape=[jax.ShapeDtypeStruct((t, d), F32), jax.ShapeDtypeStruct((t, d), BF16)],
        compiler_params=_cparams(("parallel",)),
        name="outproj",
    )(yr, ym, wt, wb, x2, g)


HALO = 16


MXU_N = 256


def _shift_down(x, first):
    r = pltpu.roll(x, 1, 0)
    top = r[0:SUBLANES]
    row = lax.broadcasted_iota(jnp.int32, top.shape, 0)
    return jnp.concatenate([jnp.where(row == 0, first, top), r[SUBLANES:]], axis=0)


def _ffn_act_kernel(h_ref, halo_ref, wg_ref, wu_ref, cw_ref, cb_ref, a_ref, hh_scr, *, tm, seq):
    i = pl.program_id(1)
    keep = jnp.where((i * tm) % seq != 0, 1.0, 0.0)
    hh_scr[0:HALO, :] = (halo_ref[...] * keep.astype(BF16)).astype(BF16)
    hh_scr[HALO:, :] = h_ref[...]
    n_sub = wg_ref.shape[1] // MXU_N

    def dots(s):
        cs = slice(MXU_N * s, MXU_N * (s + 1))
        return _dot(hh_scr[...], wg_ref[:, cs]), _dot(h_ref[...], wu_ref[:, cs])

    nxt = dots(0)
    for s in range(n_sub):
        cs = slice(MXU_N * s, MXU_N * (s + 1))
        ge, u = nxt
        if s + 1 < n_sub:
            nxt = dots(s + 1)
        c0, c1, c2 = cw_ref[0:1, cs], cw_ref[1:2, cs], cw_ref[2:3, cs]
        g = ge[HALO:]
        gm1 = ge[HALO - 1:HALO]
        gm2 = ge[HALO - 2:HALO - 1]
        w = c1 * g + _shift_down(c0 * g, c0 * gm1)
        conv = c2 * g + _shift_down(w, c1 * gm1 + c0 * gm2) + cb_ref[:, cs]
        a_ref[:, cs] = (conv * _sigmoid(conv) * u).astype(BF16)


def _ffn_act(h2, wg, wu, cw, cb, seq, tm=512, n_col=2):
    t, d = h2.shape
    f = wg.shape[1]
    tf = f // n_col
    kern = functools.partial(_ffn_act_kernel, tm=tm, seq=seq)
    hb = tm // HALO
    once = pl.Buffered(1)
    return pl.pallas_call(
        kern,
        grid=(n_col, t // tm),
        in_specs=[
            pl.BlockSpec((tm, d), lambda j, i: (i, 0)),
            pl.BlockSpec((HALO, d), lambda j, i: (jnp.maximum(i * hb - 1, 0), 0)),
            pl.BlockSpec((d, tf), lambda j, i: (0, j), pipeline_mode=once),
            pl.BlockSpec((d, tf), lambda j, i: (0, j), pipeline_mode=once),
            pl.BlockSpec((3, tf), lambda j, i: (0, j)),
            pl.BlockSpec((1, tf), lambda j, i: (0, j)),
        ],
        out_specs=pl.BlockSpec((tm, tf), lambda j, i: (i, j)),
        out_shape=jax.ShapeDtypeStruct((t, f), BF16),
        scratch_shapes=[pltpu.VMEM((tm + HALO, d), BF16)],
        compiler_params=_cparams(("parallel", "parallel")),
        name="ffn_act",
    )(h2, h2, wg, wu, cw, cb)


def _ffn_down_kernel(a_ref, wd_ref, x1_ref, gf_ref, o_ref):
    o_ref[...] = _rms(x1_ref[...] + _dot(a_ref[...], wd_ref[...]), gf_ref[...])


def _ffn_down(act, wd, x1, gf, tm=256):
    t, d = x1.shape
    f = act.shape[1]
    return pl.pallas_call(
        _ffn_down_kernel,
        grid=(t // tm,),
        in_specs=[
            pl.BlockSpec((tm, f), lambda i: (i, 0)),
            pl.BlockSpec((f, d), lambda i: (0, 0), pipeline_mode=pl.Buffered(1)),
            pl.BlockSpec((tm, d), lambda i: (i, 0)),
            pl.BlockSpec((1, d), lambda i: (0, 0)),
        ],
        out_specs=pl.BlockSpec((tm, d), lambda i: (i, 0)),
        out_shape=jax.ShapeDtypeStruct((t, d), F32),
        compiler_params=_cparams(("parallel",)),
        name="ffn_down",
    )(act, wd, x1, gf)


def _pack_in(w_in, mu):
    d = w_in.shape[0]
    o = 3 * RWKV_WIDTH
    wl = w_in[:, o:o + DECAY_LORA]
    al = w_in[:, o + DECAY_LORA:o + DECAY_LORA + AAA_LORA]
    og = o + DECAY_LORA + AAA_LORA
    gl = w_in[:, og:og + GATE_LORA]
    oq = og + GATE_LORA
    cq = w_in[:, oq:oq + Q_LORA]
    ckv = w_in[:, oq + Q_LORA:oq + Q_LORA + KV_LORA]
    kpe = w_in[:, oq + Q_LORA + KV_LORA:]
    hr = QK_ROPE // 2
    kpe_rot = jnp.concatenate([-kpe[:, hr:], kpe[:, :hr]], axis=1)
    z = lambda n: jnp.zeros((d, n), w_in.dtype)
    rest = GATE_LORA - LANES
    small = jnp.concatenate(
        [wl, al, gl[:, :LANES], kpe, gl[:, LANES:], z(LANES - QK_ROPE - rest), kpe_rot, z(LANES - QK_ROPE)],
        axis=1)
    w_a = w_in[:, :o].astype(BF16)
    w_b = jnp.concatenate([cq, ckv, small], axis=1).astype(BF16)
    zm = lambda n: jnp.zeros((n,), mu.dtype)
    mu_sm = jnp.concatenate(
        [mu[o:og], mu[og:og + LANES], zm(QK_ROPE), mu[og + LANES:og + GATE_LORA],
         zm(LANES - QK_ROPE - rest), zm(LANES)])
    return w_a, w_b, mu[:RWKV_WIDTH], mu[RWKV_WIDTH:2 * RWKV_WIDTH], mu[2 * RWKV_WIDTH:o], mu_sm


def _pack_lora(w2, a2, g2):
    n = w2.shape[1]
    z = lambda r: jnp.zeros((r, n), w2.dtype)
    rest = GATE_LORA - LANES
    w2p = jnp.concatenate([w2, z(LANES - DECAY_LORA)], axis=0)
    a2p = jnp.concatenate([z(DECAY_LORA), a2], axis=0)
    g2p = jnp.concatenate([g2[:LANES], z(QK_ROPE), g2[LANES:], z(LANES - QK_ROPE - rest)], axis=0)
    return w2p.astype(BF16), a2p.astype(BF16), g2p.astype(BF16)


def _pack_q(w_uq):
    d = w_uq.shape[0]
    hw = QK_NOPE + QK_ROPE
    hr = QK_ROPE // 2
    w = w_uq.reshape(d, MLA_HEADS, hw)
    pe = w[:, :, QK_NOPE:]
    z = jnp.zeros((d, MLA_HEADS, LANES - QK_ROPE), w_uq.dtype)
    w1 = jnp.concatenate([w[:, :, :QK_NOPE], pe, z], axis=2).reshape(d, -1)
    w2 = jnp.concatenate([-pe[:, :, hr:], pe[:, :, :hr], z], axis=2).reshape(d, -1)
    return w1.astype(BF16), w2.astype(BF16)


def _pack_kv(w_ukv):
    d = w_ukv.shape[0]
    w = w_ukv.reshape(d, MLA_HEADS, QK_NOPE + V_HEAD)
    return jnp.concatenate([w[:, :, :QK_NOPE].reshape(d, -1), w[:, :, QK_NOPE:].reshape(d, -1)],
                           axis=1).astype(BF16)


def kernel(x, positions, attn_norm_g, w_in, rwkv_mu, rwkv_w0, rwkv_w2, rwkv_a0, rwkv_a2, rwkv_g2,
           rwkv_k_k, rwkv_k_a, rwkv_r_k, rwkv_gn_w, rwkv_gn_b, mla_q_norm_g, mla_w_uq,
           mla_kv_norm_g, mla_w_ukv, w_out, ffn_norm_g, ffn_w_gate, ffn_w_up, ffn_conv_w,
           ffn_conv_b, ffn_w_down, final_norm_g):
    bsz, seq, d = x.shape
    t = bsz * seq
    depth = w_in.shape[0]
    row = lambda a: a.reshape(1, -1)
    xc = x.reshape(t, d)
    hr = QK_ROPE // 2
    inv_freq = ROPE_THETA ** (-jnp.arange(hr, dtype=F32) / hr)
    per = LANES // hr
    pos_d = jnp.repeat(positions.reshape(t // per, per), hr, axis=1)
    cos_d, sin_d = _rope(pos_d, jnp.tile(inv_freq, per).reshape(1, LANES))
    pad = jnp.zeros((t, LANES - QK_ROPE), F32)
    cosm = jnp.concatenate([cos_d.reshape(t, hr)] * 2 + [pad], axis=1)
    sinm = jnp.concatenate([sin_d.reshape(t, hr)] * 2 + [pad], axis=1)

    assert depth == 1, "the final RMSNorm is fused into the (single) layer's ffn kernel"
    l = 0
    w_a, w_b, mu_r, mu_k, mu_v, mu_sm = _pack_in(w_in[l], rwkv_mu[l])
    w2p, a2p, g2p = _pack_lora(rwkv_w2[l], rwkv_a2[l], rwkv_g2[l])
    wq1, wq2 = _pack_q(mla_w_uq[l])
    wkv = _pack_kv(mla_w_ukv[l])

    proj = _inproj(xc, row(attn_norm_g[l]), w_a, w_b)
    y_r = _rwkv(proj, bsz, seq, row(mu_r), row(mu_k), row(mu_v), row(mu_sm),
                row(rwkv_w0[l]), row(rwkv_a0[l]), row(rwkv_k_k[l]), row(rwkv_k_a[l]),
                row(rwkv_r_k[l]), row(rwkv_gn_w[l]), row(rwkv_gn_b[l]), w2p, a2p, g2p)
    q = _mla_q(proj, cosm, sinm, row(mla_q_norm_g[l]), wq1, wq2)
    k, v = _mla_kv(proj, cosm, sinm, row(mla_kv_norm_g[l]), wkv)
    y_m = _attn(q, k, v, bsz, seq)
    wo = w_out[l].astype(BF16)
    x1, h2 = _outproj(y_r, y_m, wo[:RWKV_WIDTH], wo[RWKV_WIDTH:], xc, row(ffn_norm_g[l]))
    act = _ffn_act(h2, ffn_w_gate[l].astype(BF16), ffn_w_up[l].astype(BF16), ffn_conv_w[l],
                   row(ffn_conv_b[l]), seq)
    out = _ffn_down(act, ffn_w_down[l].astype(BF16), x1, row(final_norm_g))
    return out.reshape(bsz, seq, d)
```

```python
import functools
import math

import jax
import jax.numpy as jnp
from jax import lax
from jax.experimental import pallas as pl
from jax.experimental.pallas import tpu as pltpu

F32 = jnp.float32
BF16 = jnp.bfloat16

D_MODEL = 2048
RWKV_HEAD = 64
RWKV_WIDTH = 1024
RWKV_HEADS = 16
DECAY_LORA = 64
AAA_LORA = 64
GATE_LORA = 160
GN_EPS = 64e-5
QK_NOPE = 128
QK_ROPE = 64
V_HEAD = 128
MLA_WIDTH = 1024
MLA_HEADS = 8
Q_LORA = 512
KV_LORA = 512
ROPE_THETA = 10000.0
D_FF = 5632
NORM_EPS = 1e-6
NEG_INF = -1e30

LANES = 128
SUBLANES = 8
SMALL = 512
D_IN_P = 3 * RWKV_WIDTH + Q_LORA + KV_LORA + SMALL
COL_CQ = 3 * RWKV_WIDTH // Q_LORA
COL_CKV = COL_CQ + 1
COL_SMALL = COL_CKV + 1
CHUNK = 64
PAIRS = RWKV_WIDTH // LANES
VMEM_LIMIT = 56 * 1024 * 1024
V_EXT = V_HEAD + 16
SCORE_SCALE = (QK_NOPE + QK_ROPE) ** -0.5 * math.log2(math.e)


def _cparams(sem):
    return pltpu.CompilerParams(dimension_semantics=sem, vmem_limit_bytes=VMEM_LIMIT)


def _rms(x, g):
    ms = jnp.mean(x * x, axis=-1, keepdims=True)
    return x * lax.rsqrt(ms + NORM_EPS) * g


def _sigmoid(x):
    return 1.0 / (1.0 + jnp.exp(-x))


def _dot(a, b):
    return jnp.dot(a, b, preferred_element_type=F32)


def _dot_nt(a, b):
    return lax.dot_general(a, b, (((1,), (1,)), ((), ())), preferred_element_type=F32)


def _dot_tn(a, b):
    return lax.dot_general(a, b, (((0,), (0,)), ((), ())), preferred_element_type=F32)


def _inproj_kernel(x_ref, g_ref, wa_ref, wbt_ref, o_ref):
    h = _rms(x_ref[...], g_ref[...]).astype(BF16)
    na = wa_ref.shape[1]
    o_ref[:, :na] = _dot(h, wa_ref[...])
    o_ref[:, na:] = _dot_nt(h, wbt_ref[...])


def _inproj(x2, g, w_a, w_b, tm=512):
    t, d = x2.shape
    n = w_a.shape[1] + w_b.shape[0]
    once = pl.Buffered(1)
    return pl.pallas_call(
        _inproj_kernel,
        grid=(t // tm,),
        in_specs=[
            pl.BlockSpec((tm, d), lambda i: (i, 0)),
            pl.BlockSpec((1, d), lambda i: (0, 0)),
            pl.BlockSpec(w_a.shape, lambda i: (0, 0), pipeline_mode=once),
            pl.BlockSpec(w_b.shape, lambda i: (0, 0), pipeline_mode=once),
        ],
        out_specs=pl.BlockSpec((tm, n), lambda i: (i, 0)),
        out_shape=jax.ShapeDtypeStruct((t, n), F32),
        compiler_params=_cparams(("parallel",)),
        name="inproj",
    )(x2, g, w_a, w_b)


def _rwkv_kernel(r_ref, k_ref, v_ref, sm_ref, mur_ref, muk_ref, muv_ref, musm_ref,
                 w0_ref, a0_ref, kk_ref, ka_ref, rk_ref, gnw_ref, gnb_ref,
                 w2_ref, a2_ref, g2_ref, o_ref,
                 st_scr, pr_scr, pk_scr, pv_scr, psm_scr):
    c = CHUNK
    half = RWKV_HEAD
    rows = r_ref.shape[0]
    nch = rows // c

    @pl.when(pl.program_id(1) == 0)
    def _():
        st_scr[...] = jnp.zeros_like(st_scr)
        pr_scr[...] = jnp.zeros_like(pr_scr)
        pk_scr[...] = jnp.zeros_like(pk_scr)
        pv_scr[...] = jnp.zeros_like(pv_scr)
        psm_scr[...] = jnp.zeros_like(psm_scr)

    def shift(x_ref, p_scr, mu_ref, ci):
        x = x_ref[c * ci:c * (ci + 1), :]
        prev = p_scr[...] if ci == 0 else x_ref[c * ci - 1:c * ci, :]
        row = lax.broadcasted_iota(jnp.int32, x.shape, 0)
        xp = jnp.where(row == 0, prev, pltpu.roll(x, 1, 0))
        return x + (xp - x) * mu_ref[...]

    ri = lax.broadcasted_iota(jnp.int32, (c, LANES), 0)
    li = lax.broadcasted_iota(jnp.int32, (c, LANES), 1)
    lj = li & (half - 1)
    lane_a = li < half
    strict = lj < ri
    incl = lj <= ri
    eye2 = (lj == ri).astype(F32)
    r2 = lax.broadcasted_iota(jnp.int32, (LANES, LANES), 0)
    l2 = lax.broadcasted_iota(jnp.int32, (LANES, LANES), 1)
    bd = (r2 < half) == (l2 < half)
    bd_ones = bd.astype(BF16)
    tr = lax.broadcasted_iota(jnp.int32, (c, c), 0)
    tc = lax.broadcasted_iota(jnp.int32, (c, c), 1)
    tril = (tc <= tr).astype(BF16)

    def bd_stack(x):
        xb = x.astype(BF16)
        z = jnp.zeros_like(xb)
        return jnp.concatenate([jnp.where(lane_a, xb, z), jnp.where(lane_a, z, xb)], axis=0)

    def segsum(x):
        xs = jnp.concatenate([x[:, LANES * g:LANES * (g + 1)] for g in range(PAIRS)], axis=0)
        hi = xs.astype(BF16)
        lo = (xs - hi.astype(F32)).astype(BF16)
        s = _dot(hi, bd_ones) + _dot(lo, bd_ones)
        return jnp.concatenate([s[c * g:c * (g + 1), :] for g in range(PAIRS)], axis=1)

    def cumsum_rows(x):
        hi = x.astype(BF16)
        r1 = x - hi.astype(F32)
        mid = r1.astype(BF16)
        lo = (r1 - mid.astype(F32)).astype(BF16)
        return _dot(tril, hi) + _dot(tril, mid) + _dot(tril, lo)

    prs = range(PAIRS)
    sls = [slice(LANES * p, LANES * (p + 1)) for p in prs]
    inv_n = 1.0 / RWKV_HEAD
    state = {"s": [st_scr[p] for p in prs]}
    pre, ind, dep = {}, {}, {}

    def prologue(ci):
        rs = shift(r_ref, pr_scr, mur_ref, ci)
        ks = shift(k_ref, pk_scr, muk_ref, ci)
        vs = shift(v_ref, pv_scr, muv_ref, ci)
        sm = shift(sm_ref, psm_scr, musm_ref, ci)
        g0 = sm[:, 0:LANES]
        zw = _dot(jnp.tanh(g0).astype(BF16), w2_ref[...])
        za = _dot(g0.astype(BF16), a2_ref[...])
        kkr = ks * kk_ref[...]
        nrm = segsum(kkr * kkr)
        gate = _dot(_sigmoid(sm[:, LANES:3 * LANES]).astype(BF16), g2_ref[...])
        yield
        ld = (-math.exp(-0.5)) * _sigmoid(w0_ref[...] + zw)
        lg = cumsum_rows(ld)
        a = _sigmoid(a0_ref[...] + za)
        kk = kkr * lax.rsqrt(jnp.maximum(nrm, 1e-24))
        kp = ks * (1.0 + (a - 1.0) * ka_ref[...])
        beta = kk * a
        bonus = segsum(rs * kp * rk_ref[...])
        yield
        lgc = lg[c - 1:c, :]
        einv = jnp.exp(-lg)
        ec = jnp.exp(lgc - lg)
        rt = rs * jnp.exp(lg)
        at = -kk * jnp.exp(lg - ld)
        pre[ci] = dict(vs=vs, gate=gate, bonus=bonus, gc=jnp.exp(lgc), rt=rt, at=at,
                       bt=beta * einv, kt=kp * einv, bh=beta * ec, kh=kp * ec)
        yield

    def independent(ci):
        q = pre[ci]
        v_p = [q["vs"][:, sl] for sl in sls]
        lhs1 = [jnp.concatenate([q["at"][:, sl], q["rt"][:, sl]], axis=0).astype(BF16) for sl in sls]
        rhs1 = [jnp.concatenate([bd_stack(q["bt"][:, sl]), bd_stack(q["kt"][:, sl])], axis=0) for sl in sls]
        pm = [_dot_nt(lhs1[p], rhs1[p]) for p in prs]
        bk = [jnp.concatenate([q["bh"][:, sl], q["kh"][:, sl]], axis=0).astype(BF16) for sl in sls]
        yield
        lab = [jnp.where(strict, m[0:c, 0:LANES], 0.0) for m in pm]
        lm = [jnp.concatenate([jnp.where(strict, m[0:c, LANES:2 * LANES], 0.0),
                               jnp.where(incl, m[c:2 * c, LANES:2 * LANES], 0.0)], axis=0).astype(BF16)
              for m in pm]
        mrb = [jnp.where(incl, m[c:2 * c, 0:LANES], 0.0).astype(BF16) for m in pm]
        lv = [_dot(lm[p], bd_stack(v_p[p])) for p in prs]
        pw = [_dot(l.astype(BF16), bd_stack(l)) for l in lab]
        tinv = [eye2 + l for l in lab]
        yield
        for _ in range(4):
            res = [_dot(pw[p].astype(BF16),
                        jnp.concatenate([bd_stack(tinv[p]), bd_stack(pw[p])], axis=1)) for p in prs]
            tinv = [tinv[p] + res[p][:, 0:LANES] for p in prs]
            pw = [res[p][:, LANES:2 * LANES] for p in prs]
            yield
        tinv = [(tinv[p] + _dot(pw[p].astype(BF16), bd_stack(tinv[p]))).astype(BF16) for p in prs]
        ind[ci] = dict(v_p=v_p, lhs1=lhs1, bk=bk, mrb=mrb, lv=lv, tinv=tinv)
        yield

    def dependent(ci):
        q, st = ind[ci], state["s"]
        xh = [_dot_nt(q["lhs1"][p], st[p].astype(BF16)) for p in prs]
        yield
        x = [xh[p][0:c] + q["lv"][p][0:c] for p in prs]
        u = [_dot(q["tinv"][p], bd_stack(x[p])) for p in prs]
        yield
        ys = [xh[p][c:2 * c] + q["lv"][p][c:2 * c] + _dot(q["mrb"][p], bd_stack(u[p])) for p in prs]
        uv = [jnp.concatenate([u[p], q["v_p"][p]], axis=0).astype(BF16) for p in prs]
        ds = [_dot_tn(uv[p], q["bk"][p]) for p in prs]
        yield
        gc = pre[ci]["gc"]
        state["s"] = [st[p] * gc[:, sls[p]] + jnp.where(bd, ds[p], 0.0) for p in prs]
        dep[ci] = jnp.concatenate(ys, axis=1)
        yield

    def epilogue(ci):
        y, q = dep[ci], pre[ci]
        mean = segsum(y) * inv_n
        yield
        d = y - mean
        var = segsum(d * d) * inv_n
        yield
        yn = d * lax.rsqrt(var + GN_EPS) * gnw_ref[...] + gnb_ref[...]
        o_ref[c * ci:c * (ci + 1), :] = ((yn + q["bonus"] * q["vs"]) * q["gate"]).astype(o_ref.dtype)
        yield

    phases = (prologue, independent, dependent, epilogue)
    done = object()
    for tick in range(nch + len(phases) - 1):
        live = [ph(tick - k) for k, ph in enumerate(phases) if 0 <= tick - k < nch]
        while live:
            live = [g for g in live if next(g, done) is not done]

    for p in prs:
        st_scr[p] = state["s"][p]
    for x_ref, p_scr in ((r_ref, pr_scr), (k_ref, pk_scr), (v_ref, pv_scr), (sm_ref, psm_scr)):
        p_scr[...] = x_ref[rows - 1:rows, :]


def _rwkv(proj, bsz, seq, mu_r, mu_k, mu_v, mu_sm, w0, a0, k_k, k_a, r_k, gn_w, gn_b, w2p, a2p, g2p,
          nch=8):
    t = proj.shape[0]
    w = RWKV_WIDTH
    rows = CHUNK * nch
    nc = seq // rows
    row = lambda b, c: b * nc + c
    vec = lambda n: pl.BlockSpec((1, n), lambda b, c: (0, 0))
    full = lambda a: pl.BlockSpec(a.shape, lambda b, c: (0, 0))
    return pl.pallas_call(
        _rwkv_kernel,
        grid=(bsz, nc),
        in_specs=[
            pl.BlockSpec((rows, w), lambda b, c: (row(b, c), 0)),
            pl.BlockSpec((rows, w), lambda b, c: (row(b, c), 1)),
            pl.BlockSpec((rows, w), lambda b, c: (row(b, c), 2)),
            pl.BlockSpec((rows, SMALL), lambda b, c: (row(b, c), COL_SMALL)),
            vec(w), vec(w), vec(w), vec(SMALL),
            vec(w), vec(w), vec(w), vec(w), vec(w), vec(w), vec(w),
            full(w2p), full(a2p), full(g2p),
        ],
        out_specs=pl.BlockSpec((rows, w), lambda b, c: (row(b, c), 0)),
        out_shape=jax.ShapeDtypeStruct((t, w), BF16),
        scratch_shapes=[
            pltpu.VMEM((PAIRS, LANES, LANES), F32),
            pltpu.VMEM((1, w), F32), pltpu.VMEM((1, w), F32), pltpu.VMEM((1, w), F32),
            pltpu.VMEM((1, SMALL), F32),
        ],
        compiler_params=_cparams(("parallel", "arbitrary")),
        name="rwkv",
    )(proj, proj, proj, proj, mu_r, mu_k, mu_v, mu_sm, w0, a0, k_k, k_a, r_k, gn_w, gn_b,
      w2p, a2p, g2p)


def _rope_tables(pos_ref, invf_ref):
    ang = pos_ref[...].astype(F32) * invf_ref[...]
    keep = lax.broadcasted_iota(jnp.int32, ang.shape, 1) < QK_ROPE
    return jnp.where(keep, jnp.cos(ang), 0.0), jnp.where(keep, jnp.sin(ang), 0.0)


def _mla_q_kernel(cq_ref, pos_ref, invf_ref, g_ref, w1_ref, w2_ref, q_ref):
    h = _rms(cq_ref[...], g_ref[...]).astype(BF16)
    z1 = _dot(h, w1_ref[...])
    z2 = _dot(h, w2_ref[...])
    cosm, sinm = _rope_tables(pos_ref, invf_ref)
    cosm, sinm = cosm * SCORE_SCALE, sinm * SCORE_SCALE
    hw = QK_NOPE + LANES
    for hd in range(MLA_HEADS):
        q_ref[:, hw * hd:hw * hd + QK_NOPE] = (z1[:, hw * hd:hw * hd + QK_NOPE] * SCORE_SCALE).astype(BF16)
        pe = z1[:, hw * hd + QK_NOPE:hw * (hd + 1)] * cosm + z2[:, LANES * hd:LANES * (hd + 1)] * sinm
        q_ref[:, hw * hd + QK_NOPE:hw * (hd + 1)] = pe.astype(BF16)


def _mla_q(proj, pos, invf, g, w1, w2, tm=512):
    t = proj.shape[0]
    n = MLA_HEADS * (QK_NOPE + LANES)
    return pl.pallas_call(
        _mla_q_kernel,
        grid=(t // tm,),
        in_specs=[
            pl.BlockSpec((tm, Q_LORA), lambda i: (i, COL_CQ)),
            pl.BlockSpec((tm, 1), lambda i: (i, 0)),
            pl.BlockSpec((1, LANES), lambda i: (0, 0)),
            pl.BlockSpec((1, Q_LORA), lambda i: (0, 0)),
            pl.BlockSpec(w1.shape, lambda i: (0, 0)),
            pl.BlockSpec(w2.shape, lambda i: (0, 0)),
        ],
        out_specs=pl.BlockSpec((tm, n), lambda i: (i, 0)),
        out_shape=jax.ShapeDtypeStruct((t, n), BF16),
        compiler_params=_cparams(("parallel",)),
        name="mla_q",
    )(proj, pos, invf, g, w1, w2)


def _mla_kv_kernel(ckv_ref, sm_ref, pos_ref, invf_ref, g_ref, w_ref, k_ref, v_ref):
    h = _rms(ckv_ref[...], g_ref[...]).astype(BF16)
    z = _dot(h, w_ref[...])
    cosm, sinm = _rope_tables(pos_ref, invf_ref)
    kpe = (sm_ref[:, 2 * LANES:3 * LANES] * cosm + sm_ref[:, 3 * LANES:4 * LANES] * sinm).astype(BF16)
    hw = QK_NOPE + LANES
    for hd in range(MLA_HEADS):
        k_ref[:, hw * hd:hw * hd + QK_NOPE] = z[:, QK_NOPE * hd:QK_NOPE * (hd + 1)].astype(BF16)
        k_ref[:, hw * hd + QK_NOPE:hw * (hd + 1)] = kpe
    ones = jnp.ones((V_EXT - V_HEAD, z.shape[0]), BF16)
    for hd in range(MLA_HEADS):
        vcol = MLA_HEADS * QK_NOPE + V_HEAD * hd
        v_ref[V_EXT * hd:V_EXT * hd + V_HEAD, :] = z[:, vcol:vcol + V_HEAD].T.astype(BF16)
        v_ref[V_EXT * hd + V_HEAD:V_EXT * (hd + 1), :] = ones


def _mla_kv(proj, pos, invf, g, w, tm=512):
    t = proj.shape[0]
    nk = MLA_HEADS * (QK_NOPE + LANES)
    nv = MLA_HEADS * V_EXT
    return pl.pallas_call(
        _mla_kv_kernel,
        grid=(t // tm,),
        in_specs=[
            pl.BlockSpec((tm, KV_LORA), lambda i: (i, COL_CKV)),
            pl.BlockSpec((tm, SMALL), lambda i: (i, COL_SMALL)),
            pl.BlockSpec((tm, 1), lambda i: (i, 0)),
            pl.BlockSpec((1, LANES), lambda i: (0, 0)),
            pl.BlockSpec((1, KV_LORA), lambda i: (0, 0)),
            pl.BlockSpec(w.shape, lambda i: (0, 0)),
        ],
        out_specs=[
            pl.BlockSpec((tm, nk), lambda i: (i, 0)),
            pl.BlockSpec((nv, tm), lambda i: (0, i)),
        ],
        out_shape=[jax.ShapeDtypeStruct((t, nk), BF16), jax.ShapeDtypeStruct((nv, t), BF16)],
        compiler_params=_cparams(("parallel",)),
        name="mla_kv",
    )(proj, proj, pos, invf, g, w)


ATTN_STRIP = 64


def _attn_kernel(q_ref, k_ref, vt_ref, o_ref, *, blk, nh):
    qi = pl.program_id(2)
    hw = QK_NOPE + LANES
    hds = range(nh)
    nst = blk // ATTN_STRIP

    def step(j, carry, masked):
        m, acc = carry
        start = pl.multiple_of(j * blk, blk)

        def scores(h):
            return _dot_nt(k_ref[pl.ds(start, blk), hw * h:hw * (h + 1)], q_ref[:, hw * h:hw * (h + 1)])

        def softmax(h, s):
            if masked:
                key = lax.broadcasted_iota(jnp.int32, s.shape, 0)
                qry = lax.broadcasted_iota(jnp.int32, s.shape, 1)
                s = jnp.where(key <= qry, s, NEG_INF)
            m_new = jnp.maximum(m[h], jnp.max(s, axis=0, keepdims=True))
            alpha = jnp.exp2(m[h] - m_new)
            strips = [jnp.exp2(s[ATTN_STRIP * r:ATTN_STRIP * (r + 1)] - m_new).astype(BF16)
                      for r in range(nst)]
            return m_new, alpha, jnp.concatenate(strips, axis=0)

        def values(h, alpha, p):
            return alpha * acc[h] + _dot(vt_ref[V_EXT * h:V_EXT * (h + 1), pl.ds(start, blk)], p)

        m_out, acc_out = [], []
        s_next = scores(0)
        prev = None
        for h in hds:
            s_cur = s_next
            if h + 1 < nh:
                s_next = scores(h + 1)
            m_new, alpha, p = softmax(h, s_cur)
            m_out.append(m_new)
            if prev is not None:
                acc_out.append(values(*prev))
            prev = (h, alpha, p)
        acc_out.append(values(*prev))
        return m_out, acc_out

    init = ([jnp.full((1, blk), NEG_INF, F32) for _ in hds], [jnp.zeros((V_EXT, blk), F32) for _ in hds])
    carry = lax.fori_loop(0, qi, lambda j, cr: step(j, cr, False), init)
    _, acc = step(qi, carry, True)
    for h in hds:
        out = acc[h][:V_HEAD] / acc[h][V_HEAD:V_HEAD + 1]
        o_ref[:, V_HEAD * h:V_HEAD * (h + 1)] = out.T.astype(o_ref.dtype)


def _attn(q, k, v, bsz, seq, blk=512, nh=4):
    t = q.shape[0]
    nq = seq // blk
    hw = QK_NOPE + LANES
    kern = functools.partial(_attn_kernel, blk=blk, nh=nh)
    return pl.pallas_call(
        kern,
        grid=(bsz, MLA_HEADS // nh, nq),
        in_specs=[
            pl.BlockSpec((blk, hw * nh), lambda b, h, i: (b * nq + i, h)),
            pl.BlockSpec((seq, hw * nh), lambda b, h, i: (b, h)),
            pl.BlockSpec((V_EXT * nh, seq), lambda b, h, i: (h, b)),
        ],
        out_specs=pl.BlockSpec((blk, V_HEAD * nh), lambda b, h, i: (b * nq + i, h)),
        out_shape=jax.ShapeDtypeStruct((t, MLA_HEADS * V_HEAD), BF16),
        compiler_params=_cparams(("parallel", "parallel", "arbitrary")),
        name="attn",
    )(q, k, v)


def _outproj_kernel(yr_ref, ym_ref, wt_ref, wb_ref, x_ref, g_ref, x1_ref, h2_ref):
    x1 = x_ref[...] + _dot(yr_ref[...], wt_ref[...]) + _dot(ym_ref[...], wb_ref[...])
    x1_ref[...] = x1
    h2_ref[...] = _rms(x1, g_ref[...]).astype(BF16)


def _outproj(yr, ym, wt, wb, x2, g, tm=512):
    t, d = x2.shape
    return pl.pallas_call(
        _outproj_kernel,
        grid=(t // tm,),
        in_specs=[
            pl.BlockSpec((tm, RWKV_WIDTH), lambda i: (i, 0)),
            pl.BlockSpec((tm, MLA_WIDTH), lambda i: (i, 0)),
            pl.BlockSpec(wt.shape, lambda i: (0, 0), pipeline_mode=pl.Buffered(1)),
            pl.BlockSpec(wb.shape, lambda i: (0, 0), pipeline_mode=pl.Buffered(1)),
            pl.BlockSpec((tm, d), lambda i: (i, 0)),
            pl.BlockSpec((1, d), lambda i: (0, 0)),
        ],
        out_specs=[pl.BlockSpec((tm, d), lambda i: (i, 0)), pl.BlockSpec((tm, d), lambda i: (i, 0))],
        out_shape=[jax.ShapeDtypeStruct((t, d), F32), jax.ShapeDtypeStruct((t, d), BF16)],
        compiler_params=_cparams(("parallel",)),
        name="outproj",
    )(yr, ym, wt, wb, x2, g)


HALO = 16


MXU_N = 256


def _shift_down(x, first):
    r = pltpu.roll(x, 1, 0)
    top = r[0:SUBLANES]
    row = lax.broadcasted_iota(jnp.int32, top.shape, 0)
    return jnp.concatenate([jnp.where(row == 0, first, top), r[SUBLANES:]], axis=0)


def _ffn_act_kernel(h_ref, halo_ref, wg_ref, wu_ref, cw_ref, cb_ref, a_ref, hh_scr, *, tm, seq):
    i = pl.program_id(1)
    keep = jnp.where((i * tm) % seq != 0, 1.0, 0.0)
    hh_scr[0:HALO, :] = (halo_ref[...] * keep.astype(BF16)).astype(BF16)
    hh_scr[HALO:, :] = h_ref[...]
    n_sub = wg_ref.shape[1] // MXU_N

    def dots(s):
        cs = slice(MXU_N * s, MXU_N * (s + 1))
        return _dot(hh_scr[...], wg_ref[:, cs]), _dot(h_ref[...], wu_ref[:, cs])

    nxt = dots(0)
    for s in range(n_sub):
        cs = slice(MXU_N * s, MXU_N * (s + 1))
        ge, u = nxt
        if s + 1 < n_sub:
            nxt = dots(s + 1)
        c0, c1, c2 = cw_ref[0:1, cs], cw_ref[1:2, cs], cw_ref[2:3, cs]
        g = ge[HALO:]
        gm1 = ge[HALO - 1:HALO]
        gm2 = ge[HALO - 2:HALO - 1]
        w = c1 * g + _shift_down(c0 * g, c0 * gm1)
        conv = c2 * g + _shift_down(w, c1 * gm1 + c0 * gm2) + cb_ref[:, cs]
        a_ref[:, cs] = (conv * _sigmoid(conv) * u).astype(BF16)


def _ffn_act(h2, wg, wu, cw, cb, seq, tm=512, n_col=2):
    t, d = h2.shape
    f = wg.shape[1]
    tf = f // n_col
    kern = functools.partial(_ffn_act_kernel, tm=tm, seq=seq)
    hb = tm // HALO
    once = pl.Buffered(1)
    return pl.pallas_call(
        kern,
        grid=(n_col, t // tm),
        in_specs=[
            pl.BlockSpec((tm, d), lambda j, i: (i, 0)),
            pl.BlockSpec((HALO, d), lambda j, i: (jnp.maximum(i * hb - 1, 0), 0)),
            pl.BlockSpec((d, tf), lambda j, i: (0, j), pipeline_mode=once),
            pl.BlockSpec((d, tf), lambda j, i: (0, j), pipeline_mode=once),
            pl.BlockSpec((3, tf), lambda j, i: (0, j)),
            pl.BlockSpec((1, tf), lambda j, i: (0, j)),
        ],
        out_specs=pl.BlockSpec((tm, tf), lambda j, i: (i, j)),
        out_shape=jax.ShapeDtypeStruct((t, f), BF16),
        scratch_shapes=[pltpu.VMEM((tm + HALO, d), BF16)],
        compiler_params=_cparams(("parallel", "parallel")),
        name="ffn_act",
    )(h2, h2, wg, wu, cw, cb)


def _ffn_down_kernel(a_ref, wd_ref, x1_ref, gf_ref, o_ref):
    o_ref[...] = _rms(x1_ref[...] + _dot(a_ref[...], wd_ref[...]), gf_ref[...])


def _ffn_down(act, wd, x1, gf, tm=256):
    t, d = x1.shape
    f = act.shape[1]
    return pl.pallas_call(
        _ffn_down_kernel,
        grid=(t // tm,),
        in_specs=[
            pl.BlockSpec((tm, f), lambda i: (i, 0)),
            pl.BlockSpec((f, d), lambda i: (0, 0), pipeline_mode=pl.Buffered(1)),
            pl.BlockSpec((tm, d), lambda i: (i, 0)),
            pl.BlockSpec((1, d), lambda i: (0, 0)),
        ],
        out_specs=pl.BlockSpec((tm, d), lambda i: (i, 0)),
        out_shape=jax.ShapeDtypeStruct((t, d), F32),
        compiler_params=_cparams(("parallel",)),
        name="ffn_down",
    )(act, wd, x1, gf)


def _pack_in(w_in, mu):
    d = w_in.shape[0]
    o = 3 * RWKV_WIDTH
    og = o + DECAY_LORA + AAA_LORA
    wt = w_in[:, o:].T
    lo = DECAY_LORA + AAA_LORA
    wl_al = wt[:lo]
    gl = wt[lo:lo + GATE_LORA]
    oq = lo + GATE_LORA
    cq = wt[oq:oq + Q_LORA]
    ckv = wt[oq + Q_LORA:oq + Q_LORA + KV_LORA]
    kpe = wt[oq + Q_LORA + KV_LORA:]
    hr = QK_ROPE // 2
    kpe_rot = jnp.concatenate([-kpe[hr:], kpe[:hr]], axis=0)
    z = lambda n: jnp.zeros((n, d), w_in.dtype)
    rest = GATE_LORA - LANES
    w_a = w_in[:, :o].astype(BF16)
    w_bt = jnp.concatenate(
        [cq, ckv, wl_al, gl[:LANES], kpe, gl[LANES:], z(LANES - QK_ROPE - rest), kpe_rot, z(LANES - QK_ROPE)],
        axis=0).astype(BF16)
    zm = lambda n: jnp.zeros((n,), mu.dtype)
    mu_sm = jnp.concatenate(
        [mu[o:og], mu[og:og + LANES], zm(QK_ROPE), mu[og + LANES:og + GATE_LORA],
         zm(LANES - QK_ROPE - rest), zm(LANES)])
    return w_a, w_bt, mu[:RWKV_WIDTH], mu[RWKV_WIDTH:2 * RWKV_WIDTH], mu[2 * RWKV_WIDTH:o], mu_sm


def _pack_lora(w2, a2, g2):
    n = w2.shape[1]
    z = lambda r: jnp.zeros((r, n), w2.dtype)
    rest = GATE_LORA - LANES
    w2p = jnp.concatenate([w2, z(LANES - DECAY_LORA)], axis=0)
    a2p = jnp.concatenate([z(DECAY_LORA), a2], axis=0)
    g2p = jnp.concatenate([g2[:LANES], z(QK_ROPE), g2[LANES:], z(LANES - QK_ROPE - rest)], axis=0)
    return w2p.astype(BF16), a2p.astype(BF16), g2p.astype(BF16)


def _pack_q(w_uq):
    d = w_uq.shape[0]
    hw = QK_NOPE + QK_ROPE
    hr = QK_ROPE // 2
    w = w_uq.reshape(d, MLA_HEADS, hw)
    pe = w[:, :, QK_NOPE:]
    z = jnp.zeros((d, MLA_HEADS, LANES - QK_ROPE), w_uq.dtype)
    w1 = jnp.concatenate([w[:, :, :QK_NOPE], pe, z], axis=2).reshape(d, -1)
    w2 = jnp.concatenate([-pe[:, :, hr:], pe[:, :, :hr], z], axis=2).reshape(d, -1)
    return w1.astype(BF16), w2.astype(BF16)


def _pack_kv(w_ukv):
    d = w_ukv.shape[0]
    w = w_ukv.reshape(d, MLA_HEADS, QK_NOPE + V_HEAD)
    return jnp.concatenate([w[:, :, :QK_NOPE].reshape(d, -1), w[:, :, QK_NOPE:].reshape(d, -1)],
                           axis=1).astype(BF16)


def kernel(x, positions, attn_norm_g, w_in, rwkv_mu, rwkv_w0, rwkv_w2, rwkv_a0, rwkv_a2, rwkv_g2,
           rwkv_k_k, rwkv_k_a, rwkv_r_k, rwkv_gn_w, rwkv_gn_b, mla_q_norm_g, mla_w_uq,
           mla_kv_norm_g, mla_w_ukv, w_out, ffn_norm_g, ffn_w_gate, ffn_w_up, ffn_conv_w,
           ffn_conv_b, ffn_w_down, final_norm_g):
    bsz, seq, d = x.shape
    t = bsz * seq
    depth = w_in.shape[0]
    row = lambda a: a.reshape(1, -1)
    xc = x.reshape(t, d)
    pos = positions.reshape(t, 1)
    hr = QK_ROPE // 2
    inv_freq = ROPE_THETA ** (-jnp.arange(hr, dtype=F32) / hr)
    invf = jnp.concatenate([inv_freq, inv_freq, jnp.zeros((LANES - QK_ROPE,), F32)]).reshape(1, LANES)

    assert depth == 1, "the final RMSNorm is fused into the (single) layer's ffn kernel"
    l = 0
    w_a, w_b, mu_r, mu_k, mu_v, mu_sm = _pack_in(w_in[l], rwkv_mu[l])
    w2p, a2p, g2p = _pack_lora(rwkv_w2[l], rwkv_a2[l], rwkv_g2[l])
    wq1, wq2 = _pack_q(mla_w_uq[l])
    wkv = _pack_kv(mla_w_ukv[l])

    proj = _inproj(xc, row(attn_norm_g[l]), w_a, w_b)
    y_r = _rwkv(proj, bsz, seq, row(mu_r), row(mu_k), row(mu_v), row(mu_sm),
                row(rwkv_w0[l]), row(rwkv_a0[l]), row(rwkv_k_k[l]), row(rwkv_k_a[l]),
                row(rwkv_r_k[l]), row(rwkv_gn_w[l]), row(rwkv_gn_b[l]), w2p, a2p, g2p)
    q = _mla_q(proj, pos, invf, row(mla_q_norm_g[l]), wq1, wq2)
    k, v = _mla_kv(proj, pos, invf, row(mla_kv_norm_g[l]), wkv)
    y_m = _attn(q, k, v, bsz, seq)
    wo = w_out[l].astype(BF16)
    x1, h2 = _outproj(y_r, y_m, wo[:RWKV_WIDTH], wo[RWKV_WIDTH:], xc, row(ffn_norm_g[l]))
    act = _ffn_act(h2, ffn_w_gate[l].astype(BF16), ffn_w_up[l].astype(BF16), ffn_conv_w[l],
                   row(ffn_conv_b[l]), seq)
    out = _ffn_down(act, ffn_w_down[l].astype(BF16), x1, row(final_norm_g))
    return out.reshape(bsz, seq, d)
```

```python
import functools
import math

import jax
import jax.numpy as jnp
from jax import lax
from jax.experimental import pallas as pl
from jax.experimental.pallas import tpu as pltpu

F32 = jnp.float32
BF16 = jnp.bfloat16

D_MODEL = 2048
RWKV_HEAD = 64
RWKV_WIDTH = 1024
RWKV_HEADS = 16
DECAY_LORA = 64
AAA_LORA = 64
GATE_LORA = 160
GN_EPS = 64e-5
QK_NOPE = 128
QK_ROPE = 64
V_HEAD = 128
MLA_WIDTH = 1024
MLA_HEADS = 8
Q_LORA = 512
KV_LORA = 512
ROPE_THETA = 10000.0
D_FF = 5632
NORM_EPS = 1e-6
NEG_INF = -1e30

LANES = 128
SUBLANES = 8
SMALL = 512
D_IN_P = 3 * RWKV_WIDTH + Q_LORA + KV_LORA + SMALL
COL_CQ = 3 * RWKV_WIDTH // Q_LORA
COL_CKV = COL_CQ + 1
COL_SMALL = COL_CKV + 1
CHUNK = 64
PAIRS = RWKV_WIDTH // LANES
VMEM_LIMIT = 56 * 1024 * 1024
V_EXT = V_HEAD + 16
SCORE_SCALE = (QK_NOPE + QK_ROPE) ** -0.5 * math.log2(math.e)


def _cparams(sem):
    return pltpu.CompilerParams(dimension_semantics=sem, vmem_limit_bytes=VMEM_LIMIT)


def _rms(x, g):
    ms = jnp.mean(x * x, axis=-1, keepdims=True)
    return x * lax.rsqrt(ms + NORM_EPS) * g


def _sigmoid(x):
    return 1.0 / (1.0 + jnp.exp(-x))


def _dot(a, b):
    return jnp.dot(a, b, preferred_element_type=F32)


def _dot_nt(a, b):
    return lax.dot_general(a, b, (((1,), (1,)), ((), ())), preferred_element_type=F32)


def _dot_tn(a, b):
    return lax.dot_general(a, b, (((0,), (0,)), ((), ())), preferred_element_type=F32)


def _inproj_kernel(x_ref, g_ref, wa_ref, wbt_ref, o_ref):
    h = _rms(x_ref[...], g_ref[...]).astype(BF16)
    na = wa_ref.shape[1]
    o_ref[:, :na] = _dot(h, wa_ref[...])
    o_ref[:, na:] = _dot_nt(h, wbt_ref[...])


def _inproj(x2, g, w_a, w_b, tm=512):
    t, d = x2.shape
    n = w_a.shape[1] + w_b.shape[0]
    once = pl.Buffered(1)
    return pl.pallas_call(
        _inproj_kernel,
        grid=(t // tm,),
        in_specs=[
            pl.BlockSpec((tm, d), lambda i: (i, 0)),
            pl.BlockSpec((1, d), lambda i: (0, 0)),
            pl.BlockSpec(w_a.shape, lambda i: (0, 0), pipeline_mode=once),
            pl.BlockSpec(w_b.shape, lambda i: (0, 0), pipeline_mode=once),
        ],
        out_specs=pl.BlockSpec((tm, n), lambda i: (i, 0)),
        out_shape=jax.ShapeDtypeStruct((t, n), F32),
        compiler_params=_cparams(("parallel",)),
        name="inproj",
    )(x2, g, w_a, w_b)


def _rwkv_kernel(r_ref, k_ref, v_ref, sm_ref, mur_ref, muk_ref, muv_ref, musm_ref,
                 w0_ref, a0_ref, kk_ref, ka_ref, rk_ref, gnw_ref, gnb_ref,
                 w2_ref, a2_ref, g2_ref, o_ref,
                 st_scr, pr_scr, pk_scr, pv_scr, psm_scr):
    c = CHUNK
    half = RWKV_HEAD
    rows = r_ref.shape[0]
    nch = rows // c

    @pl.when(pl.program_id(1) == 0)
    def _():
        st_scr[...] = jnp.zeros_like(st_scr)
        pr_scr[...] = jnp.zeros_like(pr_scr)
        pk_scr[...] = jnp.zeros_like(pk_scr)
        pv_scr[...] = jnp.zeros_like(pv_scr)
        psm_scr[...] = jnp.zeros_like(psm_scr)

    def shift(x_ref, p_scr, mu_ref, ci):
        x = x_ref[c * ci:c * (ci + 1), :]
        prev = p_scr[...] if ci == 0 else x_ref[c * ci - 1:c * ci, :]
        row = lax.broadcasted_iota(jnp.int32, x.shape, 0)
        xp = jnp.where(row == 0, prev, pltpu.roll(x, 1, 0))
        return x + (xp - x) * mu_ref[...]

    ri = lax.broadcasted_iota(jnp.int32, (c, LANES), 0)
    li = lax.broadcasted_iota(jnp.int32, (c, LANES), 1)
    lj = li & (half - 1)
    lane_a = li < half
    strict = lj < ri
    incl = lj <= ri
    eye2 = (lj == ri).astype(F32)
    r2 = lax.broadcasted_iota(jnp.int32, (LANES, LANES), 0)
    l2 = lax.broadcasted_iota(jnp.int32, (LANES, LANES), 1)
    bd = (r2 < half) == (l2 < half)
    bd_ones = bd.astype(BF16)
    bd_ones2 = jnp.concatenate([bd_ones, bd_ones], axis=0)
    tr = lax.broadcasted_iota(jnp.int32, (c, 3 * c), 0)
    tc = lax.broadcasted_iota(jnp.int32, (c, 3 * c), 1)
    tril3 = ((tc & (c - 1)) <= tr).astype(BF16)

    def bd_stack(x):
        xb = x.astype(BF16)
        z = jnp.zeros_like(xb)
        return jnp.concatenate([jnp.where(lane_a, xb, z), jnp.where(lane_a, z, xb)], axis=0)

    def segsum(x):
        xs = jnp.concatenate([x[:, LANES * g:LANES * (g + 1)] for g in range(PAIRS)], axis=0)
        hi = xs.astype(BF16)
        lo = (xs - hi.astype(F32)).astype(BF16)
        s = _dot(jnp.concatenate([hi, lo], axis=1), bd_ones2)
        return jnp.concatenate([s[c * g:c * (g + 1), :] for g in range(PAIRS)], axis=1)

    def cumsum_rows(x):
        hi = x.astype(BF16)
        r1 = x - hi.astype(F32)
        mid = r1.astype(BF16)
        lo = (r1 - mid.astype(F32)).astype(BF16)
        return _dot(tril3, jnp.concatenate([hi, mid, lo], axis=0))

    prs = range(PAIRS)
    sls = [slice(LANES * p, LANES * (p + 1)) for p in prs]
    inv_n = 1.0 / RWKV_HEAD
    state = {"s": [st_scr[p] for p in prs]}
    pre, ind, dep = {}, {}, {}

    def prologue(ci):
        rs = shift(r_ref, pr_scr, mur_ref, ci)
        ks = shift(k_ref, pk_scr, muk_ref, ci)
        vs = shift(v_ref, pv_scr, muv_ref, ci)
        sm = shift(sm_ref, psm_scr, musm_ref, ci)
        g0 = sm[:, 0:LANES]
        zw = _dot(jnp.tanh(g0).astype(BF16), w2_ref[...])
        za = _dot(g0.astype(BF16), a2_ref[...])
        kkr = ks * kk_ref[...]
        nrm = segsum(kkr * kkr)
        gate = _dot(_sigmoid(sm[:, LANES:3 * LANES]).astype(BF16), g2_ref[...])
        yield
        ld = (-math.exp(-0.5)) * _sigmoid(w0_ref[...] + zw)
        lg = cumsum_rows(ld)
        a = _sigmoid(a0_ref[...] + za)
        kk = kkr * lax.rsqrt(jnp.maximum(nrm, 1e-24))
        kp = ks * (1.0 + (a - 1.0) * ka_ref[...])
        beta = kk * a
        bonus = segsum(rs * kp * rk_ref[...])
        yield
        lgc = lg[c - 1:c, :]
        einv = jnp.exp(-lg)
        ec = jnp.exp(lgc - lg)
        rt = rs * jnp.exp(lg)
        at = -kk * jnp.exp(lg - ld)
        pre[ci] = dict(vs=vs, gate=gate, bonus=bonus, gc=jnp.exp(lgc), rt=rt, at=at,
                       bt=beta * einv, kt=kp * einv, bh=beta * ec, kh=kp * ec)
        yield

    def independent(ci):
        q = pre[ci]
        v_p = [q["vs"][:, sl] for sl in sls]
        lhs1 = [jnp.concatenate([q["at"][:, sl], q["rt"][:, sl]], axis=0).astype(BF16) for sl in sls]
        rhs1 = [jnp.concatenate([bd_stack(q["bt"][:, sl]), bd_stack(q["kt"][:, sl])], axis=0) for sl in sls]
        pm = [_dot_nt(lhs1[p], rhs1[p]) for p in prs]
        bk = [jnp.concatenate([q["bh"][:, sl], q["kh"][:, sl]], axis=0).astype(BF16) for sl in sls]
        yield
        lab = [jnp.where(strict, m[0:c, 0:LANES], 0.0) for m in pm]
        lm = [jnp.concatenate([jnp.where(strict, m[0:c, LANES:2 * LANES], 0.0),
                               jnp.where(incl, m[c:2 * c, LANES:2 * LANES], 0.0)], axis=0).astype(BF16)
              for m in pm]
        mrb = [jnp.where(incl, m[c:2 * c, 0:LANES], 0.0).astype(BF16) for m in pm]
        lv = [_dot(lm[p], bd_stack(v_p[p])) for p in prs]
        pw = [_dot(l.astype(BF16), bd_stack(l)) for l in lab]
        tinv = [eye2 + l for l in lab]
        yield
        for _ in range(4):
            res = [_dot(pw[p].astype(BF16),
                        jnp.concatenate([bd_stack(tinv[p]), bd_stack(pw[p])], axis=1)) for p in prs]
            tinv = [tinv[p] + res[p][:, 0:LANES] for p in prs]
            pw = [res[p][:, LANES:2 * LANES] for p in prs]
            yield
        tinv = [(tinv[p] + _dot(pw[p].astype(BF16), bd_stack(tinv[p]))).astype(BF16) for p in prs]
        ind[ci] = dict(v_p=v_p, lhs1=lhs1, bk=bk, mrb=mrb, lv=lv, tinv=tinv)
        yield

    def dependent(ci):
        q, st = ind[ci], state["s"]
        xh = [_dot_nt(q["lhs1"][p], st[p].astype(BF16)) for p in prs]
        yield
        x = [xh[p][0:c] + q["lv"][p][0:c] for p in prs]
        u = [_dot(q["tinv"][p], bd_stack(x[p])) for p in prs]
        yield
        ys = [xh[p][c:2 * c] + q["lv"][p][c:2 * c] + _dot(q["mrb"][p], bd_stack(u[p])) for p in prs]
        uv = [jnp.concatenate([u[p], q["v_p"][p]], axis=0).astype(BF16) for p in prs]
        ds = [_dot_tn(uv[p], q["bk"][p]) for p in prs]
        yield
        gc = pre[ci]["gc"]
        state["s"] = [st[p] * gc[:, sls[p]] + jnp.where(bd, ds[p], 0.0) for p in prs]
        dep[ci] = jnp.concatenate(ys, axis=1)
        yield

    def epilogue(ci):
        y, q = dep[ci], pre[ci]
        mean = segsum(y) * inv_n
        yield
        d = y - mean
        var = segsum(d * d) * inv_n
        yield
        yn = d * lax.rsqrt(var + GN_EPS) * gnw_ref[...] + gnb_ref[...]
        o_ref[c * ci:c * (ci + 1), :] = ((yn + q["bonus"] * q["vs"]) * q["gate"]).astype(o_ref.dtype)
        yield

    phases = (prologue, independent, dependent, epilogue)
    done = object()
    for tick in range(nch + len(phases) - 1):
        live = [ph(tick - k) for k, ph in enumerate(phases) if 0 <= tick - k < nch]
        while live:
            live = [g for g in live if next(g, done) is not done]

    for p in prs:
        st_scr[p] = state["s"][p]
    for x_ref, p_scr in ((r_ref, pr_scr), (k_ref, pk_scr), (v_ref, pv_scr), (sm_ref, psm_scr)):
        p_scr[...] = x_ref[rows - 1:rows, :]


def _rwkv(proj, bsz, seq, mu_r, mu_k, mu_v, mu_sm, w0, a0, k_k, k_a, r_k, gn_w, gn_b, w2p, a2p, g2p,
          nch=8):
    t = proj.shape[0]
    w = RWKV_WIDTH
    rows = CHUNK * nch
    nc = seq // rows
    row = lambda b, c: b * nc + c
    vec = lambda n: pl.BlockSpec((1, n), lambda b, c: (0, 0))
    full = lambda a: pl.BlockSpec(a.shape, lambda b, c: (0, 0))
    return pl.pallas_call(
        _rwkv_kernel,
        grid=(bsz, nc),
        in_specs=[
            pl.BlockSpec((rows, w), lambda b, c: (row(b, c), 0)),
            pl.BlockSpec((rows, w), lambda b, c: (row(b, c), 1)),
            pl.BlockSpec((rows, w), lambda b, c: (row(b, c), 2)),
            pl.BlockSpec((rows, SMALL), lambda b, c: (row(b, c), COL_SMALL)),
            vec(w), vec(w), vec(w), vec(SMALL),
            vec(w), vec(w), vec(w), vec(w), vec(w), vec(w), vec(w),
            full(w2p), full(a2p), full(g2p),
        ],
        out_specs=pl.BlockSpec((rows, w), lambda b, c: (row(b, c), 0)),
        out_shape=jax.ShapeDtypeStruct((t, w), BF16),
        scratch_shapes=[
            pltpu.VMEM((PAIRS, LANES, LANES), F32),
            pltpu.VMEM((1, w), F32), pltpu.VMEM((1, w), F32), pltpu.VMEM((1, w), F32),
            pltpu.VMEM((1, SMALL), F32),
        ],
        compiler_params=_cparams(("parallel", "arbitrary")),
        name="rwkv",
    )(proj, proj, proj, proj, mu_r, mu_k, mu_v, mu_sm, w0, a0, k_k, k_a, r_k, gn_w, gn_b,
      w2p, a2p, g2p)


def _rope_tables(pos_ref, invf_ref):
    ang = pos_ref[...].astype(F32) * invf_ref[...]
    keep = lax.broadcasted_iota(jnp.int32, ang.shape, 1) < QK_ROPE
    return jnp.where(keep, jnp.cos(ang), 0.0), jnp.where(keep, jnp.sin(ang), 0.0)


def _mla_q_kernel(cq_ref, pos_ref, invf_ref, g_ref, w1_ref, w2_ref, q_ref):
    h = _rms(cq_ref[...], g_ref[...]).astype(BF16)
    z1 = _dot(h, w1_ref[...])
    z2 = _dot(h, w2_ref[...])
    cosm, sinm = _rope_tables(pos_ref, invf_ref)
    cosm, sinm = cosm * SCORE_SCALE, sinm * SCORE_SCALE
    hw = QK_NOPE + LANES
    for hd in range(MLA_HEADS):
        q_ref[:, hw * hd:hw * hd + QK_NOPE] = (z1[:, hw * hd:hw * hd + QK_NOPE] * SCORE_SCALE).astype(BF16)
        pe = z1[:, hw * hd + QK_NOPE:hw * (hd + 1)] * cosm + z2[:, LANES * hd:LANES * (hd + 1)] * sinm
        q_ref[:, hw * hd + QK_NOPE:hw * (hd + 1)] = pe.astype(BF16)


def _mla_q(proj, pos, invf, g, w1, w2, tm=512):
    t = proj.shape[0]
    n = MLA_HEADS * (QK_NOPE + LANES)
    return pl.pallas_call(
        _mla_q_kernel,
        grid=(t // tm,),
        in_specs=[
            pl.BlockSpec((tm, Q_LORA), lambda i: (i, COL_CQ)),
            pl.BlockSpec((tm, 1), lambda i: (i, 0)),
            pl.BlockSpec((1, LANES), lambda i: (0, 0)),
            pl.BlockSpec((1, Q_LORA), lambda i: (0, 0)),
            pl.BlockSpec(w1.shape, lambda i: (0, 0)),
            pl.BlockSpec(w2.shape, lambda i: (0, 0)),
        ],
        out_specs=pl.BlockSpec((tm, n), lambda i: (i, 0)),
        out_shape=jax.ShapeDtypeStruct((t, n), BF16),
        compiler_params=_cparams(("parallel",)),
        name="mla_q",
    )(proj, pos, invf, g, w1, w2)


def _mla_kv_kernel(ckv_ref, sm_ref, pos_ref, invf_ref, g_ref, w_ref, k_ref, v_ref):
    h = _rms(ckv_ref[...], g_ref[...]).astype(BF16)
    z = _dot(h, w_ref[...])
    cosm, sinm = _rope_tables(pos_ref, invf_ref)
    kpe = (sm_ref[:, 2 * LANES:3 * LANES] * cosm + sm_ref[:, 3 * LANES:4 * LANES] * sinm).astype(BF16)
    hw = QK_NOPE + LANES
    for hd in range(MLA_HEADS):
        k_ref[:, hw * hd:hw * hd + QK_NOPE] = z[:, QK_NOPE * hd:QK_NOPE * (hd + 1)].astype(BF16)
        k_ref[:, hw * hd + QK_NOPE:hw * (hd + 1)] = kpe
    ones = jnp.ones((V_EXT - V_HEAD, z.shape[0]), BF16)
    for hd in range(MLA_HEADS):
        vcol = MLA_HEADS * QK_NOPE + V_HEAD * hd
        v_ref[V_EXT * hd:V_EXT * hd + V_HEAD, :] = z[:, vcol:vcol + V_HEAD].T.astype(BF16)
        v_ref[V_EXT * hd + V_HEAD:V_EXT * (hd + 1), :] = ones


def _mla_kv(proj, pos, invf, g, w, tm=512):
    t = proj.shape[0]
    nk = MLA_HEADS * (QK_NOPE + LANES)
    nv = MLA_HEADS * V_EXT
    return pl.pallas_call(
        _mla_kv_kernel,
        grid=(t // tm,),
        in_specs=[
            pl.BlockSpec((tm, KV_LORA), lambda i: (i, COL_CKV)),
            pl.BlockSpec((tm, SMALL), lambda i: (i, COL_SMALL)),
            pl.BlockSpec((tm, 1), lambda i: (i, 0)),
            pl.BlockSpec((1, LANES), lambda i: (0, 0)),
            pl.BlockSpec((1, KV_LORA), lambda i: (0, 0)),
            pl.BlockSpec(w.shape, lambda i: (0, 0)),
        ],
        out_specs=[
            pl.BlockSpec((tm, nk), lambda i: (i, 0)),
            pl.BlockSpec((nv, tm), lambda i: (0, i)),
        ],
        out_shape=[jax.ShapeDtypeStruct((t, nk), BF16), jax.ShapeDtypeStruct((nv, t), BF16)],
        compiler_params=_cparams(("parallel",)),
        name="mla_kv",
    )(proj, proj, pos, invf, g, w)


ATTN_STRIP = 64


def _attn_kernel(q_ref, k_ref, vt_ref, o_ref, *, blk, nh):
    qi = pl.program_id(2)
    hw = QK_NOPE + LANES
    hds = range(nh)

    def weights(s, m_new):
        return jnp.concatenate(
            [jnp.exp2(s[ATTN_STRIP * r:ATTN_STRIP * (r + 1)] - m_new).astype(BF16)
             for r in range(s.shape[0] // ATTN_STRIP)], axis=0)

    def pipeline(scores, softmax, values):
        m_out, acc_out = [], []
        s_next = scores(0)
        prev = None
        for h in hds:
            s_cur = s_next
            if h + 1 < nh:
                s_next = scores(h + 1)
            m_new, rest = softmax(h, s_cur)
            m_out.append(m_new)
            if prev is not None:
                acc_out.append(values(*prev))
            prev = (h, rest)
        acc_out.append(values(*prev))
        return m_out, acc_out

    def full_step(j, carry):
        m, acc = carry
        start = pl.multiple_of(j * blk, blk)

        def scores(h):
            return _dot_nt(k_ref[pl.ds(start, blk), hw * h:hw * (h + 1)], q_ref[:, hw * h:hw * (h + 1)])

        def softmax(h, s):
            m_new = jnp.maximum(m[h], jnp.max(s, axis=0, keepdims=True))
            return m_new, (jnp.exp2(m[h] - m_new), weights(s, m_new))

        def values(h, rest):
            alpha, p = rest
            return alpha * acc[h] + _dot(vt_ref[V_EXT * h:V_EXT * (h + 1), pl.ds(start, blk)], p)

        return pipeline(scores, softmax, values)

    def diag_step(carry):
        m, acc = carry
        hb = blk // 2
        lo = pl.multiple_of(qi * blk, blk)
        hi = pl.multiple_of(qi * blk + hb, hb)
        key = lax.broadcasted_iota(jnp.int32, (hb, blk), 0)
        qry = lax.broadcasted_iota(jnp.int32, (hb, blk), 1)
        causal_a = key <= qry
        causal_b = causal_a[:, :hb]

        def scores(h):
            hc = slice(hw * h, hw * (h + 1))
            return (_dot_nt(k_ref[pl.ds(lo, hb), hc], q_ref[:, hc]),
                    _dot_nt(k_ref[pl.ds(hi, hb), hc], q_ref[hb:, hc]))

        def softmax(h, s):
            sa = jnp.where(causal_a, s[0], NEG_INF)
            sb = jnp.where(causal_b, s[1], NEG_INF)
            mx = jnp.max(sa, axis=0, keepdims=True)
            mx = jnp.concatenate([mx[:, :hb], jnp.maximum(mx[:, hb:], jnp.max(sb, axis=0, keepdims=True))],
                                 axis=1)
            m_new = jnp.maximum(m[h], mx)
            return m_new, (jnp.exp2(m[h] - m_new), weights(sa, m_new), weights(sb, m_new[:, hb:]))

        def values(h, rest):
            alpha, pa, pb = rest
            vh = slice(V_EXT * h, V_EXT * (h + 1))
            a = alpha * acc[h] + _dot(vt_ref[vh, pl.ds(lo, hb)], pa)
            b = _dot(vt_ref[vh, pl.ds(hi, hb)], pb)
            return jnp.concatenate([a[:, :hb], a[:, hb:] + b], axis=1)

        return pipeline(scores, softmax, values)

    init = ([jnp.full((1, blk), NEG_INF, F32) for _ in hds], [jnp.zeros((V_EXT, blk), F32) for _ in hds])
    carry = lax.fori_loop(0, qi, full_step, init)
    _, acc = diag_step(carry)
    for h in hds:
        out = acc[h][:V_HEAD] / acc[h][V_HEAD:V_HEAD + 1]
        o_ref[:, V_HEAD * h:V_HEAD * (h + 1)] = out.T.astype(o_ref.dtype)


def _attn(q, k, v, bsz, seq, blk=512, nh=4):
    t = q.shape[0]
    nq = seq // blk
    hw = QK_NOPE + LANES
    kern = functools.partial(_attn_kernel, blk=blk, nh=nh)
    return pl.pallas_call(
        kern,
        grid=(bsz, MLA_HEADS // nh, nq),
        in_specs=[
            pl.BlockSpec((blk, hw * nh), lambda b, h, i: (b * nq + i, h)),
            pl.BlockSpec((seq, hw * nh), lambda b, h, i: (b, h)),
            pl.BlockSpec((V_EXT * nh, seq), lambda b, h, i: (h, b)),
        ],
        out_specs=pl.BlockSpec((blk, V_HEAD * nh), lambda b, h, i: (b * nq + i, h)),
        out_shape=jax.ShapeDtypeStruct((t, MLA_HEADS * V_HEAD), BF16),
        compiler_params=_cparams(("parallel", "parallel", "arbitrary")),
        name="attn",
    )(q, k, v)


def _outproj_kernel(yr_ref, ym_ref, wt_ref, wb_ref, x_ref, g_ref, x1_ref, h2_ref):
    x1 = x_ref[...] + _dot(yr_ref[...], wt_ref[...]) + _dot(ym_ref[...], wb_ref[...])
    x1_ref[...] = x1
    h2_ref[...] = _rms(x1, g_ref[...]).astype(BF16)


def _outproj(yr, ym, wt, wb, x2, g, tm=512):
    t, d = x2.shape
    return pl.pallas_call(
        _outproj_kernel,
        grid=(t // tm,),
        in_specs=[
            pl.BlockSpec((tm, RWKV_WIDTH), lambda i: (i, 0)),
            pl.BlockSpec((tm, MLA_WIDTH), lambda i: (i, 0)),
            pl.BlockSpec(wt.shape, lambda i: (0, 0), pipeline_mode=pl.Buffered(1)),
            pl.BlockSpec(wb.shape, lambda i: (0, 0), pipeline_mode=pl.Buffered(1)),
            pl.BlockSpec((tm, d), lambda i: (i, 0)),
            pl.BlockSpec((1, d), lambda i: (0, 0)),
        ],
        out_specs=[pl.BlockSpec((tm, d), lambda i: (i, 0)), pl.BlockSpec((tm, d), lambda i: (i, 0))],
        out_shape=[jax.ShapeDtypeStruct((t, d), F32), jax.ShapeDtypeStruct((t, d), BF16)],
        compiler_params=_cparams(("parallel",)),
        name="outproj",
    )(yr, ym, wt, wb, x2, g)


HALO = 16


MXU_N = 256


def _shift_down(x, first):
    r = pltpu.roll(x, 1, 0)
    top = r[0:SUBLANES]
    row = lax.broadcasted_iota(jnp.int32, top.shape, 0)
    return jnp.concatenate([jnp.where(row == 0, first, top), r[SUBLANES:]], axis=0)


def _ffn_act_kernel(h_ref, halo_ref, wg_ref, wu_ref, cw_ref, cb_ref, a_ref, hh_scr, *, tm, seq):
    i = pl.program_id(1)
    keep = jnp.where((i * tm) % seq != 0, 1.0, 0.0)
    hh_scr[0:HALO, :] = (halo_ref[...] * keep.astype(BF16)).astype(BF16)
    hh_scr[HALO:, :] = h_ref[...]
    n_sub = wg_ref.shape[1] // MXU_N

    def gate_dot(s):
        return _dot(hh_scr[...], wg_ref[:, MXU_N * s:MXU_N * (s + 1)])

    def up_dot(s):
        return _dot(h_ref[...], wu_ref[:, MXU_N * s:MXU_N * (s + 1)])

    ge, u = gate_dot(0), up_dot(0)
    for s in range(n_sub):
        cs = slice(MXU_N * s, MXU_N * (s + 1))
        more = s + 1 < n_sub
        ge_next = gate_dot(s + 1) if more else None
        c0, c1, c2 = cw_ref[0:1, cs], cw_ref[1:2, cs], cw_ref[2:3, cs]
        g = ge[HALO:]
        gm1 = ge[HALO - 1:HALO]
        gm2 = ge[HALO - 2:HALO - 1]
        w = c1 * g + _shift_down(c0 * g, c0 * gm1)
        conv = c2 * g + _shift_down(w, c1 * gm1 + c0 * gm2) + cb_ref[:, cs]
        u_next = up_dot(s + 1) if more else None
        a_ref[:, cs] = (conv * _sigmoid(conv) * u).astype(BF16)
        ge, u = ge_next, u_next


def _ffn_act(h2, wg, wu, cw, cb, seq, tm=512, n_col=2):
    t, d = h2.shape
    f = wg.shape[1]
    tf = f // n_col
    kern = functools.partial(_ffn_act_kernel, tm=tm, seq=seq)
    hb = tm // HALO
    once = pl.Buffered(1)
    return pl.pallas_call(
        kern,
        grid=(n_col, t // tm),
        in_specs=[
            pl.BlockSpec((tm, d), lambda j, i: (i, 0)),
            pl.BlockSpec((HALO, d), lambda j, i: (jnp.maximum(i * hb - 1, 0), 0)),
            pl.BlockSpec((d, tf), lambda j, i: (0, j), pipeline_mode=once),
            pl.BlockSpec((d, tf), lambda j, i: (0, j), pipeline_mode=once),
            pl.BlockSpec((3, tf), lambda j, i: (0, j)),
            pl.BlockSpec((1, tf), lambda j, i: (0, j)),
        ],
        out_specs=pl.BlockSpec((tm, tf), lambda j, i: (i, j)),
        out_shape=jax.ShapeDtypeStruct((t, f), BF16),
        scratch_shapes=[pltpu.VMEM((tm + HALO, d), BF16)],
        compiler_params=_cparams(("parallel", "parallel")),
        name="ffn_act",
    )(h2, h2, wg, wu, cw, cb)


def _ffn_down_kernel(a_ref, wd_ref, x1_ref, gf_ref, o_ref):
    o_ref[...] = _rms(x1_ref[...] + _dot(a_ref[...], wd_ref[...]), gf_ref[...])


def _ffn_down(act, wd, x1, gf, tm=256):
    t, d = x1.shape
    f = act.shape[1]
    return pl.pallas_call(
        _ffn_down_kernel,
        grid=(t // tm,),
        in_specs=[
            pl.BlockSpec((tm, f), lambda i: (i, 0)),
            pl.BlockSpec((f, d), lambda i: (0, 0), pipeline_mode=pl.Buffered(1)),
            pl.BlockSpec((tm, d), lambda i: (i, 0)),
            pl.BlockSpec((1, d), lambda i: (0, 0)),
        ],
        out_specs=pl.BlockSpec((tm, d), lambda i: (i, 0)),
        out_shape=jax.ShapeDtypeStruct((t, d), F32),
        compiler_params=_cparams(("parallel",)),
        name="ffn_down",
    )(act, wd, x1, gf)


def _pack_in(w_in, mu):
    d = w_in.shape[0]
    o = 3 * RWKV_WIDTH
    og = o + DECAY_LORA + AAA_LORA
    wt = w_in[:, o:].T
    lo = DECAY_LORA + AAA_LORA
    wl_al = wt[:lo]
    gl = wt[lo:lo + GATE_LORA]
    oq = lo + GATE_LORA
    cq = wt[oq:oq + Q_LORA]
    ckv = wt[oq + Q_LORA:oq + Q_LORA + KV_LORA]
    kpe = wt[oq + Q_LORA + KV_LORA:]
    hr = QK_ROPE // 2
    kpe_rot = jnp.concatenate([-kpe[hr:], kpe[:hr]], axis=0)
    z = lambda n: jnp.zeros((n, d), w_in.dtype)
    rest = GATE_LORA - LANES
    w_a = w_in[:, :o].astype(BF16)
    w_bt = jnp.concatenate(
        [cq, ckv, wl_al, gl[:LANES], kpe, gl[LANES:], z(LANES - QK_ROPE - rest), kpe_rot, z(LANES - QK_ROPE)],
        axis=0).astype(BF16)
    zm = lambda n: jnp.zeros((n,), mu.dtype)
    mu_sm = jnp.concatenate(
        [mu[o:og], mu[og:og + LANES], zm(QK_ROPE), mu[og + LANES:og + GATE_LORA],
         zm(LANES - QK_ROPE - rest), zm(LANES)])
    return w_a, w_bt, mu[:RWKV_WIDTH], mu[RWKV_WIDTH:2 * RWKV_WIDTH], mu[2 * RWKV_WIDTH:o], mu_sm


def _pack_lora(w2, a2, g2):
    n = w2.shape[1]
    z = lambda r: jnp.zeros((r, n), w2.dtype)
    rest = GATE_LORA - LANES
    w2p = jnp.concatenate([w2, z(LANES - DECAY_LORA)], axis=0)
    a2p = jnp.concatenate([z(DECAY_LORA), a2], axis=0)
    g2p = jnp.concatenate([g2[:LANES], z(QK_ROPE), g2[LANES:], z(LANES - QK_ROPE - rest)], axis=0)
    return w2p.astype(BF16), a2p.astype(BF16), g2p.astype(BF16)


def _pack_q(w_uq):
    d = w_uq.shape[0]
    hw = QK_NOPE + QK_ROPE
    hr = QK_ROPE // 2
    w = w_uq.reshape(d, MLA_HEADS, hw)
    pe = w[:, :, QK_NOPE:]
    z = jnp.zeros((d, MLA_HEADS, LANES - QK_ROPE), w_uq.dtype)
    w1 = jnp.concatenate([w[:, :, :QK_NOPE], pe, z], axis=2).reshape(d, -1)
    w2 = jnp.concatenate([-pe[:, :, hr:], pe[:, :, :hr], z], axis=2).reshape(d, -1)
    return w1.astype(BF16), w2.astype(BF16)


def _pack_kv(w_ukv):
    d = w_ukv.shape[0]
    w = w_ukv.reshape(d, MLA_HEADS, QK_NOPE + V_HEAD)
    return jnp.concatenate([w[:, :, :QK_NOPE].reshape(d, -1), w[:, :, QK_NOPE:].reshape(d, -1)],
                           axis=1).astype(BF16)


def kernel(x, positions, attn_norm_g, w_in, rwkv_mu, rwkv_w0, rwkv_w2, rwkv_a0, rwkv_a2, rwkv_g2,
           rwkv_k_k, rwkv_k_a, rwkv_r_k, rwkv_gn_w, rwkv_gn_b, mla_q_norm_g, mla_w_uq,
           mla_kv_norm_g, mla_w_ukv, w_out, ffn_norm_g, ffn_w_gate, ffn_w_up, ffn_conv_w,
           ffn_conv_b, ffn_w_down, final_norm_g):
    bsz, seq, d = x.shape
    t = bsz * seq
    depth = w_in.shape[0]
    row = lambda a: a.reshape(1, -1)
    xc = x.reshape(t, d)
    pos = positions.reshape(t, 1)
    hr = QK_ROPE // 2
    inv_freq = ROPE_THETA ** (-jnp.arange(hr, dtype=F32) / hr)
    invf = jnp.concatenate([inv_freq, inv_freq, jnp.zeros((LANES - QK_ROPE,), F32)]).reshape(1, LANES)

    assert depth == 1, "the final RMSNorm is fused into the (single) layer's ffn kernel"
    l = 0
    w_a, w_b, mu_r, mu_k, mu_v, mu_sm = _pack_in(w_in[l], rwkv_mu[l])
    w2p, a2p, g2p = _pack_lora(rwkv_w2[l], rwkv_a2[l], rwkv_g2[l])
    wq1, wq2 = _pack_q(mla_w_uq[l])
    wkv = _pack_kv(mla_w_ukv[l])

    proj = _inproj(xc, row(attn_norm_g[l]), w_a, w_b)
    y_r = _rwkv(proj, bsz, seq, row(mu_r), row(mu_k), row(mu_v), row(mu_sm),
                row(rwkv_w0[l]), row(rwkv_a0[l]), row(rwkv_k_k[l]), row(rwkv_k_a[l]),
                row(rwkv_r_k[l]), row(rwkv_gn_w[l]), row(rwkv_gn_b[l]), w2p, a2p, g2p)
    q = _mla_q(proj, pos, invf, row(mla_q_norm_g[l]), wq1, wq2)
    k, v = _mla_kv(proj, pos, invf, row(mla_kv_norm_g[l]), wkv)
    y_m = _attn(q, k, v, bsz, seq)
    wo = w_out[l].astype(BF16)
    x1, h2 = _outproj(y_r, y_m, wo[:RWKV_WIDTH], wo[RWKV_WIDTH:], xc, row(ffn_norm_g[l]))
    act = _ffn_act(h2, ffn_w_gate[l].astype(BF16), ffn_w_up[l].astype(BF16), ffn_conv_w[l],
                   row(ffn_conv_b[l]), seq)
    out = _ffn_down(act, ffn_w_down[l].astype(BF16), x1, row(final_norm_g))
    return out.reshape(bsz, seq, d)
```

```python
import functools
import math

import jax
import jax.numpy as jnp
from jax import lax
from jax.experimental import pallas as pl
from jax.experimental.pallas import tpu as pltpu

F32 = jnp.float32
BF16 = jnp.bfloat16

D_MODEL = 2048
RWKV_HEAD = 64
RWKV_WIDTH = 1024
RWKV_HEADS = 16
DECAY_LORA = 64
AAA_LORA = 64
GATE_LORA = 160
GN_EPS = 64e-5
QK_NOPE = 128
QK_ROPE = 64
V_HEAD = 128
MLA_WIDTH = 1024
MLA_HEADS = 8
Q_LORA = 512
KV_LORA = 512
ROPE_THETA = 10000.0
D_FF = 5632
NORM_EPS = 1e-6
NEG_INF = -1e30

LANES = 128
SUBLANES = 8
BF16_ROWS = 16
SMALL = 512
D_IN_P = 3 * RWKV_WIDTH + Q_LORA + KV_LORA + SMALL
COL_CQ = 3 * RWKV_WIDTH // Q_LORA
COL_CKV = COL_CQ + 1
COL_SMALL = COL_CKV + 1
CHUNK = 64
PAIRS = RWKV_WIDTH // LANES
VMEM_LIMIT = 56 * 1024 * 1024
V_EXT = V_HEAD + BF16_ROWS
SCORE_SCALE = (QK_NOPE + QK_ROPE) ** -0.5 * math.log2(math.e)


def _cparams(sem):
    return pltpu.CompilerParams(dimension_semantics=sem, vmem_limit_bytes=VMEM_LIMIT)


def _rms(x, g):
    ms = jnp.mean(x * x, axis=-1, keepdims=True)
    return x * lax.rsqrt(ms + NORM_EPS) * g


def _sigmoid(x):
    return 1.0 / (1.0 + jnp.exp(-x))


def _dot(a, b):
    return jnp.dot(a, b, preferred_element_type=F32)


def _dot_nt(a, b):
    return lax.dot_general(a, b, (((1,), (1,)), ((), ())), preferred_element_type=F32)


def _dot_tn(a, b):
    return lax.dot_general(a, b, (((0,), (0,)), ((), ())), preferred_element_type=F32)


def _inproj_kernel(x_ref, g_ref, wa_ref, wbt_ref, o_ref):
    h = _rms(x_ref[...], g_ref[...]).astype(BF16)
    na = wa_ref.shape[1]
    o_ref[:, :na] = _dot(h, wa_ref[...])
    o_ref[:, na:] = _dot_nt(h, wbt_ref[...])


def _inproj(x2, g, w_a, w_b, tm=512):
    t, d = x2.shape
    n = w_a.shape[1] + w_b.shape[0]
    once = pl.Buffered(1)
    return pl.pallas_call(
        _inproj_kernel,
        grid=(t // tm,),
        in_specs=[
            pl.BlockSpec((tm, d), lambda i: (i, 0)),
            pl.BlockSpec((1, d), lambda i: (0, 0)),
            pl.BlockSpec(w_a.shape, lambda i: (0, 0), pipeline_mode=once),
            pl.BlockSpec(w_b.shape, lambda i: (0, 0), pipeline_mode=once),
        ],
        out_specs=pl.BlockSpec((tm, n), lambda i: (i, 0)),
        out_shape=jax.ShapeDtypeStruct((t, n), F32),
        compiler_params=_cparams(("parallel",)),
        name="inproj",
    )(x2, g, w_a, w_b)


RWKV_INS = 18


def _rwkv_kernel(*refs, n_cast):
    (r_ref, k_ref, v_ref, sm_ref, mur_ref, muk_ref, muv_ref, musm_ref,
     w0_ref, a0_ref, kk_ref, ka_ref, rk_ref, gnw_ref, gnb_ref,
     w2_ref, a2_ref, g2_ref) = refs[:RWKV_INS]
    cast_in = refs[RWKV_INS:RWKV_INS + n_cast]
    o_ref = refs[RWKV_INS + n_cast]
    cast_out = refs[RWKV_INS + n_cast + 1:RWKV_INS + 2 * n_cast + 1]
    st_scr, pr_scr, pk_scr, pv_scr, psm_scr = refs[RWKV_INS + 2 * n_cast + 1:]
    for src, dst in zip(cast_in, cast_out):
        dst[...] = src[...].astype(dst.dtype)
    c = CHUNK
    half = RWKV_HEAD
    rows = r_ref.shape[0]
    nch = rows // c

    @pl.when(pl.program_id(1) == 0)
    def _():
        st_scr[...] = jnp.zeros_like(st_scr)
        pr_scr[...] = jnp.zeros_like(pr_scr)
        pk_scr[...] = jnp.zeros_like(pk_scr)
        pv_scr[...] = jnp.zeros_like(pv_scr)
        psm_scr[...] = jnp.zeros_like(psm_scr)

    def shift(x_ref, p_scr, mu_ref, ci):
        x = x_ref[c * ci:c * (ci + 1), :]
        prev = p_scr[...] if ci == 0 else x_ref[c * ci - 1:c * ci, :]
        row = lax.broadcasted_iota(jnp.int32, x.shape, 0)
        xp = jnp.where(row == 0, prev, pltpu.roll(x, 1, 0))
        return x + (xp - x) * mu_ref[...]

    ri = lax.broadcasted_iota(jnp.int32, (c, LANES), 0)
    li = lax.broadcasted_iota(jnp.int32, (c, LANES), 1)
    lj = li & (half - 1)
    lane_a = li < half
    strict = lj < ri
    incl = lj <= ri
    eye2 = (lj == ri).astype(F32)
    r2 = lax.broadcasted_iota(jnp.int32, (LANES, LANES), 0)
    l2 = lax.broadcasted_iota(jnp.int32, (LANES, LANES), 1)
    bd = (r2 < half) == (l2 < half)
    bd_ones = bd.astype(BF16)
    bd_ones2 = jnp.concatenate([bd_ones, bd_ones], axis=0)
    tr = lax.broadcasted_iota(jnp.int32, (c, 3 * c), 0)
    tc = lax.broadcasted_iota(jnp.int32, (c, 3 * c), 1)
    tril3 = ((tc & (c - 1)) <= tr).astype(BF16)

    def bd_stack(x):
        xb = x.astype(BF16)
        z = jnp.zeros_like(xb)
        return jnp.concatenate([jnp.where(lane_a, xb, z), jnp.where(lane_a, z, xb)], axis=0)

    def segsum(x):
        xs = jnp.concatenate([x[:, LANES * g:LANES * (g + 1)] for g in range(PAIRS)], axis=0)
        hi = xs.astype(BF16)
        lo = (xs - hi.astype(F32)).astype(BF16)
        s = _dot(jnp.concatenate([hi, lo], axis=1), bd_ones2)
        return jnp.concatenate([s[c * g:c * (g + 1), :] for g in range(PAIRS)], axis=1)

    def cumsum_rows(x):
        hi = x.astype(BF16)
        r1 = x - hi.astype(F32)
        mid = r1.astype(BF16)
        lo = (r1 - mid.astype(F32)).astype(BF16)
        return _dot(tril3, jnp.concatenate([hi, mid, lo], axis=0))

    prs = range(PAIRS)
    sls = [slice(LANES * p, LANES * (p + 1)) for p in prs]
    inv_n = 1.0 / RWKV_HEAD
    state = {"s": [st_scr[p] for p in prs]}
    pre, ind, dep = {}, {}, {}

    def prologue(ci):
        rs = shift(r_ref, pr_scr, mur_ref, ci)
        ks = shift(k_ref, pk_scr, muk_ref, ci)
        vs = shift(v_ref, pv_scr, muv_ref, ci)
        sm = shift(sm_ref, psm_scr, musm_ref, ci)
        g0 = sm[:, 0:LANES]
        zw = _dot(jnp.tanh(g0).astype(BF16), w2_ref[...])
        za = _dot(g0.astype(BF16), a2_ref[...])
        kkr = ks * kk_ref[...]
        nrm = segsum(kkr * kkr)
        gate = _dot(_sigmoid(sm[:, LANES:3 * LANES]).astype(BF16), g2_ref[...])
        yield
        ld = (-math.exp(-0.5)) * _sigmoid(w0_ref[...] + zw)
        lg = cumsum_rows(ld)
        a = _sigmoid(a0_ref[...] + za)
        kk = kkr * lax.rsqrt(jnp.maximum(nrm, 1e-24))
        kp = ks * (1.0 + (a - 1.0) * ka_ref[...])
        beta = kk * a
        bonus = segsum(rs * kp * rk_ref[...])
        yield
        lgc = lg[c - 1:c, :]
        einv = jnp.exp(-lg)
        ec = jnp.exp(lgc - lg)
        rt = rs * jnp.exp(lg)
        at = -kk * jnp.exp(lg - ld)
        pre[ci] = dict(vs=vs, gate=gate, bonus=bonus, gc=jnp.exp(lgc), rt=rt, at=at,
                       bt=beta * einv, kt=kp * einv, bh=beta * ec, kh=kp * ec)
        yield

    def independent(ci):
        q = pre[ci]
        v_p = [q["vs"][:, sl] for sl in sls]
        lhs1 = [jnp.concatenate([q["at"][:, sl], q["rt"][:, sl]], axis=0).astype(BF16) for sl in sls]
        rhs1 = [jnp.concatenate([bd_stack(q["bt"][:, sl]), bd_stack(q["kt"][:, sl])], axis=0) for sl in sls]
        pm = [_dot_nt(lhs1[p], rhs1[p]) for p in prs]
        bk = [jnp.concatenate([q["bh"][:, sl], q["kh"][:, sl]], axis=0).astype(BF16) for sl in sls]
        yield
        lab = [jnp.where(strict, m[0:c, 0:LANES], 0.0) for m in pm]
        lm = [jnp.concatenate([jnp.where(strict, m[0:c, LANES:2 * LANES], 0.0),
                               jnp.where(incl, m[c:2 * c, LANES:2 * LANES], 0.0)], axis=0).astype(BF16)
              for m in pm]
        mrb = [jnp.where(incl, m[c:2 * c, 0:LANES], 0.0).astype(BF16) for m in pm]
        lv = [_dot(lm[p], bd_stack(v_p[p])) for p in prs]
        pw = [_dot(l.astype(BF16), bd_stack(l)) for l in lab]
        tinv = [eye2 + l for l in lab]
        yield
        for _ in range(4):
            res = [_dot(pw[p].astype(BF16),
                        jnp.concatenate([bd_stack(tinv[p]), bd_stack(pw[p])], axis=1)) for p in prs]
            tinv = [tinv[p] + res[p][:, 0:LANES] for p in prs]
            pw = [res[p][:, LANES:2 * LANES] for p in prs]
            yield
        tinv = [(tinv[p] + _dot(pw[p].astype(BF16), bd_stack(tinv[p]))).astype(BF16) for p in prs]
        ind[ci] = dict(v_p=v_p, lhs1=lhs1, bk=bk, mrb=mrb, lv=lv, tinv=tinv)
        yield

    def dependent(ci):
        q, st = ind[ci], state["s"]
        xh = [_dot_nt(q["lhs1"][p], st[p].astype(BF16)) for p in prs]
        yield
        x = [xh[p][0:c] + q["lv"][p][0:c] for p in prs]
        u = [_dot(q["tinv"][p], bd_stack(x[p])) for p in prs]
        yield
        ys = [xh[p][c:2 * c] + q["lv"][p][c:2 * c] + _dot(q["mrb"][p], bd_stack(u[p])) for p in prs]
        uv = [jnp.concatenate([u[p], q["v_p"][p]], axis=0).astype(BF16) for p in prs]
        ds = [_dot_tn(uv[p], q["bk"][p]) for p in prs]
        yield
        gc = pre[ci]["gc"]
        state["s"] = [st[p] * gc[:, sls[p]] + jnp.where(bd, ds[p], 0.0) for p in prs]
        dep[ci] = jnp.concatenate(ys, axis=1)
        yield

    def epilogue(ci):
        y, q = dep[ci], pre[ci]
        mean = segsum(y) * inv_n
        yield
        d = y - mean
        var = segsum(d * d) * inv_n
        yield
        yn = d * lax.rsqrt(var + GN_EPS) * gnw_ref[...] + gnb_ref[...]
        o_ref[c * ci:c * (ci + 1), :] = ((yn + q["bonus"] * q["vs"]) * q["gate"]).astype(o_ref.dtype)
        yield

    phases = (prologue, independent, dependent, epilogue)
    done = object()
    for tick in range(nch + len(phases) - 1):
        live = [ph(tick - k) for k, ph in enumerate(phases) if 0 <= tick - k < nch]
        while live:
            live = [g for g in live if next(g, done) is not done]

    for p in prs:
        st_scr[p] = state["s"][p]
    for x_ref, p_scr in ((r_ref, pr_scr), (k_ref, pk_scr), (v_ref, pv_scr), (sm_ref, psm_scr)):
        p_scr[...] = x_ref[rows - 1:rows, :]


def _rwkv(proj, bsz, seq, mu_r, mu_k, mu_v, mu_sm, w0, a0, k_k, k_a, r_k, gn_w, gn_b, w2p, a2p, g2p,
          to_bf16=(), nch=8):
    t = proj.shape[0]
    w = RWKV_WIDTH
    rows = CHUNK * nch
    nc = seq // rows
    steps = bsz * nc
    row = lambda b, c: b * nc + c
    vec = lambda n: pl.BlockSpec((1, n), lambda b, c: (0, 0))
    full = lambda a: pl.BlockSpec(a.shape, lambda b, c: (0, 0))
    assert all(a.shape[0] % (steps * BF16_ROWS) == 0 for a in to_bf16), "row slabs must be bf16-tile aligned"
    slab = lambda a: pl.BlockSpec((a.shape[0] // steps, a.shape[1]), lambda b, c: (row(b, c), 0))
    return pl.pallas_call(
        functools.partial(_rwkv_kernel, n_cast=len(to_bf16)),
        grid=(bsz, nc),
        in_specs=[
            pl.BlockSpec((rows, w), lambda b, c: (row(b, c), 0)),
            pl.BlockSpec((rows, w), lambda b, c: (row(b, c), 1)),
            pl.BlockSpec((rows, w), lambda b, c: (row(b, c), 2)),
            pl.BlockSpec((rows, SMALL), lambda b, c: (row(b, c), COL_SMALL)),
            vec(w), vec(w), vec(w), vec(SMALL),
            vec(w), vec(w), vec(w), vec(w), vec(w), vec(w), vec(w),
            full(w2p), full(a2p), full(g2p),
        ] + [slab(a) for a in to_bf16],
        out_specs=[pl.BlockSpec((rows, w), lambda b, c: (row(b, c), 0))] + [slab(a) for a in to_bf16],
        out_shape=[jax.ShapeDtypeStruct((t, w), BF16)]
        + [jax.ShapeDtypeStruct(a.shape, BF16) for a in to_bf16],
        scratch_shapes=[
            pltpu.VMEM((PAIRS, LANES, LANES), F32),
            pltpu.VMEM((1, w), F32), pltpu.VMEM((1, w), F32), pltpu.VMEM((1, w), F32),
            pltpu.VMEM((1, SMALL), F32),
        ],
        compiler_params=_cparams(("parallel", "arbitrary")),
        name="rwkv",
    )(proj, proj, proj, proj, mu_r, mu_k, mu_v, mu_sm, w0, a0, k_k, k_a, r_k, gn_w, gn_b,
      w2p, a2p, g2p, *to_bf16)


def _rope_tables(pos_ref, invf_ref):
    ang = pos_ref[...].astype(F32) * invf_ref[...]
    keep = lax.broadcasted_iota(jnp.int32, ang.shape, 1) < QK_ROPE
    return jnp.where(keep, jnp.cos(ang), 0.0), jnp.where(keep, jnp.sin(ang), 0.0)


def _mla_q_kernel(cq_ref, pos_ref, invf_ref, g_ref, w1_ref, w2_ref, q_ref):
    h = _rms(cq_ref[...], g_ref[...]).astype(BF16)
    z1 = _dot(h, w1_ref[...])
    z2 = _dot(h, w2_ref[...])
    cosm, sinm = _rope_tables(pos_ref, invf_ref)
    cosm, sinm = cosm * SCORE_SCALE, sinm * SCORE_SCALE
    hw = QK_NOPE + LANES
    for hd in range(MLA_HEADS):
        q_ref[:, hw * hd:hw * hd + QK_NOPE] = (z1[:, hw * hd:hw * hd + QK_NOPE] * SCORE_SCALE).astype(BF16)
        pe = z1[:, hw * hd + QK_NOPE:hw * (hd + 1)] * cosm + z2[:, LANES * hd:LANES * (hd + 1)] * sinm
        q_ref[:, hw * hd + QK_NOPE:hw * (hd + 1)] = pe.astype(BF16)


def _mla_q(proj, pos, invf, g, w1, w2, tm=512):
    t = proj.shape[0]
    n = MLA_HEADS * (QK_NOPE + LANES)
    return pl.pallas_call(
        _mla_q_kernel,
        grid=(t // tm,),
        in_specs=[
            pl.BlockSpec((tm, Q_LORA), lambda i: (i, COL_CQ)),
            pl.BlockSpec((tm, 1), lambda i: (i, 0)),
            pl.BlockSpec((1, LANES), lambda i: (0, 0)),
            pl.BlockSpec((1, Q_LORA), lambda i: (0, 0)),
            pl.BlockSpec(w1.shape, lambda i: (0, 0)),
            pl.BlockSpec(w2.shape, lambda i: (0, 0)),
        ],
        out_specs=pl.BlockSpec((tm, n), lambda i: (i, 0)),
        out_shape=jax.ShapeDtypeStruct((t, n), BF16),
        compiler_params=_cparams(("parallel",)),
        name="mla_q",
    )(proj, pos, invf, g, w1, w2)


def _mla_kv_kernel(ckv_ref, sm_ref, pos_ref, invf_ref, g_ref, w_ref, k_ref, v_ref):
    h = _rms(ckv_ref[...], g_ref[...]).astype(BF16)
    z = _dot(h, w_ref[...])
    cosm, sinm = _rope_tables(pos_ref, invf_ref)
    kpe = (sm_ref[:, 2 * LANES:3 * LANES] * cosm + sm_ref[:, 3 * LANES:4 * LANES] * sinm).astype(BF16)
    hw = QK_NOPE + LANES
    for hd in range(MLA_HEADS):
        k_ref[:, hw * hd:hw * hd + QK_NOPE] = z[:, QK_NOPE * hd:QK_NOPE * (hd + 1)].astype(BF16)
        k_ref[:, hw * hd + QK_NOPE:hw * (hd + 1)] = kpe
    ones = jnp.ones((V_EXT - V_HEAD, z.shape[0]), BF16)
    for hd in range(MLA_HEADS):
        vcol = MLA_HEADS * QK_NOPE + V_HEAD * hd
        v_ref[V_EXT * hd:V_EXT * hd + V_HEAD, :] = z[:, vcol:vcol + V_HEAD].T.astype(BF16)
        v_ref[V_EXT * hd + V_HEAD:V_EXT * (hd + 1), :] = ones


def _mla_kv(proj, pos, invf, g, w, tm=512):
    t = proj.shape[0]
    nk = MLA_HEADS * (QK_NOPE + LANES)
    nv = MLA_HEADS * V_EXT
    return pl.pallas_call(
        _mla_kv_kernel,
        grid=(t // tm,),
        in_specs=[
            pl.BlockSpec((tm, KV_LORA), lambda i: (i, COL_CKV)),
            pl.BlockSpec((tm, SMALL), lambda i: (i, COL_SMALL)),
            pl.BlockSpec((tm, 1), lambda i: (i, 0)),
            pl.BlockSpec((1, LANES), lambda i: (0, 0)),
            pl.BlockSpec((1, KV_LORA), lambda i: (0, 0)),
            pl.BlockSpec(w.shape, lambda i: (0, 0)),
        ],
        out_specs=[
            pl.BlockSpec((tm, nk), lambda i: (i, 0)),
            pl.BlockSpec((nv, tm), lambda i: (0, i)),
        ],
        out_shape=[jax.ShapeDtypeStruct((t, nk), BF16), jax.ShapeDtypeStruct((nv, t), BF16)],
        compiler_params=_cparams(("parallel",)),
        name="mla_kv",
    )(proj, proj, pos, invf, g, w)


ATTN_STRIP = 64


def _attn_kernel(q_ref, k_ref, vt_ref, o_ref, *, blk, nh):
    qi = pl.program_id(2)
    hw = QK_NOPE + LANES
    hds = range(nh)

    def weights(s, m_new):
        return jnp.concatenate(
            [jnp.exp2(s[ATTN_STRIP * r:ATTN_STRIP * (r + 1)] - m_new).astype(BF16)
             for r in range(s.shape[0] // ATTN_STRIP)], axis=0)

    def pipeline(scores, softmax, values):
        m_out, acc_out = [], []
        s_next = scores(0)
        prev = None
        for h in hds:
            s_cur = s_next
            if h + 1 < nh:
                s_next = scores(h + 1)
            m_new, rest = softmax(h, s_cur)
            m_out.append(m_new)
            if prev is not None:
                acc_out.append(values(*prev))
            prev = (h, rest)
        acc_out.append(values(*prev))
        return m_out, acc_out

    def full_step(j, carry):
        m, acc = carry
        start = pl.multiple_of(j * blk, blk)

        def scores(h):
            return _dot_nt(k_ref[pl.ds(start, blk), hw * h:hw * (h + 1)], q_ref[:, hw * h:hw * (h + 1)])

        def softmax(h, s):
            m_new = jnp.maximum(m[h], jnp.max(s, axis=0, keepdims=True))
            return m_new, (jnp.exp2(m[h] - m_new), weights(s, m_new))

        def values(h, rest):
            alpha, p = rest
            return alpha * acc[h] + _dot(vt_ref[V_EXT * h:V_EXT * (h + 1), pl.ds(start, blk)], p)

        return pipeline(scores, softmax, values)

    def diag_step(carry):
        m, acc = carry
        hb = blk // 2
        lo = pl.multiple_of(qi * blk, blk)
        hi = pl.multiple_of(qi * blk + hb, hb)
        key = lax.broadcasted_iota(jnp.int32, (hb, blk), 0)
        qry = lax.broadcasted_iota(jnp.int32, (hb, blk), 1)
        causal_a = key <= qry
        causal_b = causal_a[:, :hb]

        def scores(h):
            hc = slice(hw * h, hw * (h + 1))
            return (_dot_nt(k_ref[pl.ds(lo, hb), hc], q_ref[:, hc]),
                    _dot_nt(k_ref[pl.ds(hi, hb), hc], q_ref[hb:, hc]))

        def softmax(h, s):
            sa = jnp.where(causal_a, s[0], NEG_INF)
            sb = jnp.where(causal_b, s[1], NEG_INF)
            mx = jnp.max(sa, axis=0, keepdims=True)
            mx = jnp.concatenate([mx[:, :hb], jnp.maximum(mx[:, hb:], jnp.max(sb, axis=0, keepdims=True))],
                                 axis=1)
            m_new = jnp.maximum(m[h], mx)
            return m_new, (jnp.exp2(m[h] - m_new), weights(sa, m_new), weights(sb, m_new[:, hb:]))

        def values(h, rest):
            alpha, pa, pb = rest
            vh = slice(V_EXT * h, V_EXT * (h + 1))
            a = alpha * acc[h] + _dot(vt_ref[vh, pl.ds(lo, hb)], pa)
            b = _dot(vt_ref[vh, pl.ds(hi, hb)], pb)
            return jnp.concatenate([a[:, :hb], a[:, hb:] + b], axis=1)

        return pipeline(scores, softmax, values)

    init = ([jnp.full((1, blk), NEG_INF, F32) for _ in hds], [jnp.zeros((V_EXT, blk), F32) for _ in hds])
    carry = lax.fori_loop(0, qi, full_step, init)
    _, acc = diag_step(carry)
    for h in hds:
        out = acc[h][:V_HEAD] / acc[h][V_HEAD:V_HEAD + 1]
        o_ref[:, V_HEAD * h:V_HEAD * (h + 1)] = out.T.astype(o_ref.dtype)


def _attn(q, k, v, bsz, seq, blk=512, nh=4):
    t = q.shape[0]
    nq = seq // blk
    hw = QK_NOPE + LANES
    kern = functools.partial(_attn_kernel, blk=blk, nh=nh)
    return pl.pallas_call(
        kern,
        grid=(bsz, MLA_HEADS // nh, nq),
        in_specs=[
            pl.BlockSpec((blk, hw * nh), lambda b, h, i: (b * nq + i, h)),
            pl.BlockSpec((seq, hw * nh), lambda b, h, i: (b, h)),
            pl.BlockSpec((V_EXT * nh, seq), lambda b, h, i: (h, b)),
        ],
        out_specs=pl.BlockSpec((blk, V_HEAD * nh), lambda b, h, i: (b * nq + i, h)),
        out_shape=jax.ShapeDtypeStruct((t, MLA_HEADS * V_HEAD), BF16),
        compiler_params=_cparams(("parallel", "parallel", "arbitrary")),
        name="attn",
    )(q, k, v)


def _outproj_kernel(yr_ref, ym_ref, w_ref, x_ref, g_ref, x1_ref, h2_ref):
    nr = yr_ref.shape[1]
    x1 = x_ref[...] + _dot(yr_ref[...], w_ref[:nr, :]) + _dot(ym_ref[...], w_ref[nr:, :])
    x1_ref[...] = x1
    h2_ref[...] = _rms(x1, g_ref[...]).astype(BF16)


def _outproj(yr, ym, w, x2, g, tm=512):
    t, d = x2.shape
    return pl.pallas_call(
        _outproj_kernel,
        grid=(t // tm,),
        in_specs=[
            pl.BlockSpec((tm, RWKV_WIDTH), lambda i: (i, 0)),
            pl.BlockSpec((tm, MLA_WIDTH), lambda i: (i, 0)),
            pl.BlockSpec(w.shape, lambda i: (0, 0), pipeline_mode=pl.Buffered(1)),
            pl.BlockSpec((tm, d), lambda i: (i, 0)),
            pl.BlockSpec((1, d), lambda i: (0, 0)),
        ],
        out_specs=[pl.BlockSpec((tm, d), lambda i: (i, 0)), pl.BlockSpec((tm, d), lambda i: (i, 0))],
        out_shape=[jax.ShapeDtypeStruct((t, d), F32), jax.ShapeDtypeStruct((t, d), BF16)],
        compiler_params=_cparams(("parallel",)),
        name="outproj",
    )(yr, ym, w, x2, g)


HALO = BF16_ROWS


MXU_N = 256


def _shift_down(x, first):
    r = pltpu.roll(x, 1, 0)
    top = r[0:SUBLANES]
    row = lax.broadcasted_iota(jnp.int32, top.shape, 0)
    return jnp.concatenate([jnp.where(row == 0, first, top), r[SUBLANES:]], axis=0)


def _ffn_act_kernel(h_ref, halo_ref, wg_ref, wu_ref, cw_ref, cb_ref, a_ref, hh_scr, *, tm, seq):
    i = pl.program_id(1)
    keep = jnp.where((i * tm) % seq != 0, 1.0, 0.0)
    hh_scr[0:HALO, :] = (halo_ref[...] * keep.astype(BF16)).astype(BF16)
    hh_scr[HALO:, :] = h_ref[...]
    n_sub = wg_ref.shape[1] // MXU_N

    def gate_dot(s):
        return _dot(hh_scr[...], wg_ref[:, MXU_N * s:MXU_N * (s + 1)])

    def up_dot(s):
        return _dot(h_ref[...], wu_ref[:, MXU_N * s:MXU_N * (s + 1)])

    ge, u = gate_dot(0), up_dot(0)
    for s in range(n_sub):
        cs = slice(MXU_N * s, MXU_N * (s + 1))
        more = s + 1 < n_sub
        ge_next = gate_dot(s + 1) if more else None
        c0, c1, c2 = cw_ref[0:1, cs], cw_ref[1:2, cs], cw_ref[2:3, cs]
        g = ge[HALO:]
        gm1 = ge[HALO - 1:HALO]
        gm2 = ge[HALO - 2:HALO - 1]
        w = c1 * g + _shift_down(c0 * g, c0 * gm1)
        conv = c2 * g + _shift_down(w, c1 * gm1 + c0 * gm2) + cb_ref[:, cs]
        u_next = up_dot(s + 1) if more else None
        a_ref[:, cs] = (conv * _sigmoid(conv) * u).astype(BF16)
        ge, u = ge_next, u_next


def _ffn_act(h2, wg, wu, cw, cb, seq, tm=512, n_col=2):
    t, d = h2.shape
    f = wg.shape[1]
    tf = f // n_col
    kern = functools.partial(_ffn_act_kernel, tm=tm, seq=seq)
    hb = tm // HALO
    once = pl.Buffered(1)
    return pl.pallas_call(
        kern,
        grid=(n_col, t // tm),
        in_specs=[
            pl.BlockSpec((tm, d), lambda j, i: (i, 0)),
            pl.BlockSpec((HALO, d), lambda j, i: (jnp.maximum(i * hb - 1, 0), 0)),
            pl.BlockSpec((d, tf), lambda j, i: (0, j), pipeline_mode=once),
            pl.BlockSpec((d, tf), lambda j, i: (0, j), pipeline_mode=once),
            pl.BlockSpec((3, tf), lambda j, i: (0, j)),
            pl.BlockSpec((1, tf), lambda j, i: (0, j)),
        ],
        out_specs=pl.BlockSpec((tm, tf), lambda j, i: (i, j)),
        out_shape=jax.ShapeDtypeStruct((t, f), BF16),
        scratch_shapes=[pltpu.VMEM((tm + HALO, d), BF16)],
        compiler_params=_cparams(("parallel", "parallel")),
        name="ffn_act",
    )(h2, h2, wg, wu, cw, cb)


def _ffn_down_kernel(a_ref, wd_ref, x1_ref, gf_ref, o_ref):
    o_ref[...] = _rms(x1_ref[...] + _dot(a_ref[...], wd_ref[...]), gf_ref[...])


def _ffn_down(act, wd, x1, gf, tm=256):
    t, d = x1.shape
    f = act.shape[1]
    return pl.pallas_call(
        _ffn_down_kernel,
        grid=(t // tm,),
        in_specs=[
            pl.BlockSpec((tm, f), lambda i: (i, 0)),
            pl.BlockSpec((f, d), lambda i: (0, 0), pipeline_mode=pl.Buffered(1)),
            pl.BlockSpec((tm, d), lambda i: (i, 0)),
            pl.BlockSpec((1, d), lambda i: (0, 0)),
        ],
        out_specs=pl.BlockSpec((tm, d), lambda i: (i, 0)),
        out_shape=jax.ShapeDtypeStruct((t, d), F32),
        compiler_params=_cparams(("parallel",)),
        name="ffn_down",
    )(act, wd, x1, gf)


def _pack_in(w_in, mu):
    d = w_in.shape[0]
    o = 3 * RWKV_WIDTH
    og = o + DECAY_LORA + AAA_LORA
    wt = w_in[:, o:].T
    lo = DECAY_LORA + AAA_LORA
    wl_al = wt[:lo]
    gl = wt[lo:lo + GATE_LORA]
    oq = lo + GATE_LORA
    cq = wt[oq:oq + Q_LORA]
    ckv = wt[oq + Q_LORA:oq + Q_LORA + KV_LORA]
    kpe = wt[oq + Q_LORA + KV_LORA:]
    hr = QK_ROPE // 2
    kpe_rot = jnp.concatenate([-kpe[hr:], kpe[:hr]], axis=0)
    z = lambda n: jnp.zeros((n, d), w_in.dtype)
    rest = GATE_LORA - LANES
    w_a = w_in[:, :o].astype(BF16)
    w_bt = jnp.concatenate(
        [cq, ckv, wl_al, gl[:LANES], kpe, gl[LANES:], z(LANES - QK_ROPE - rest), kpe_rot, z(LANES - QK_ROPE)],
        axis=0).astype(BF16)
    zm = lambda n: jnp.zeros((n,), mu.dtype)
    mu_sm = jnp.concatenate(
        [mu[o:og], mu[og:og + LANES], zm(QK_ROPE), mu[og + LANES:og + GATE_LORA],
         zm(LANES - QK_ROPE - rest), zm(LANES)])
    return w_a, w_bt, mu[:RWKV_WIDTH], mu[RWKV_WIDTH:2 * RWKV_WIDTH], mu[2 * RWKV_WIDTH:o], mu_sm


def _pack_lora(w2, a2, g2):
    n = w2.shape[1]
    z = lambda r: jnp.zeros((r, n), w2.dtype)
    rest = GATE_LORA - LANES
    w2p = jnp.concatenate([w2, z(LANES - DECAY_LORA)], axis=0)
    a2p = jnp.concatenate([z(DECAY_LORA), a2], axis=0)
    g2p = jnp.concatenate([g2[:LANES], z(QK_ROPE), g2[LANES:], z(LANES - QK_ROPE - rest)], axis=0)
    return w2p.astype(BF16), a2p.astype(BF16), g2p.astype(BF16)


def _pack_q(w_uq):
    d = w_uq.shape[0]
    hw = QK_NOPE + QK_ROPE
    hr = QK_ROPE // 2
    w = w_uq.reshape(d, MLA_HEADS, hw)
    pe = w[:, :, QK_NOPE:]
    z = jnp.zeros((d, MLA_HEADS, LANES - QK_ROPE), w_uq.dtype)
    w1 = jnp.concatenate([w[:, :, :QK_NOPE], pe, z], axis=2).reshape(d, -1)
    w2 = jnp.concatenate([-pe[:, :, hr:], pe[:, :, :hr], z], axis=2).reshape(d, -1)
    return w1.astype(BF16), w2.astype(BF16)


def _pack_kv(w_ukv):
    d = w_ukv.shape[0]
    w = w_ukv.reshape(d, MLA_HEADS, QK_NOPE + V_HEAD)
    return jnp.concatenate([w[:, :, :QK_NOPE].reshape(d, -1), w[:, :, QK_NOPE:].reshape(d, -1)],
                           axis=1).astype(BF16)


def kernel(x, positions, attn_norm_g, w_in, rwkv_mu, rwkv_w0, rwkv_w2, rwkv_a0, rwkv_a2, rwkv_g2,
           rwkv_k_k, rwkv_k_a, rwkv_r_k, rwkv_gn_w, rwkv_gn_b, mla_q_norm_g, mla_w_uq,
           mla_kv_norm_g, mla_w_ukv, w_out, ffn_norm_g, ffn_w_gate, ffn_w_up, ffn_conv_w,
           ffn_conv_b, ffn_w_down, final_norm_g):
    bsz, seq, d = x.shape
    t = bsz * seq
    depth = w_in.shape[0]
    row = lambda a: a.reshape(1, -1)
    xc = x.reshape(t, d)
    pos = positions.reshape(t, 1)
    hr = QK_ROPE // 2
    inv_freq = ROPE_THETA ** (-jnp.arange(hr, dtype=F32) / hr)
    invf = jnp.concatenate([inv_freq, inv_freq, jnp.zeros((LANES - QK_ROPE,), F32)]).reshape(1, LANES)

    assert depth == 1, "the final RMSNorm is fused into the (single) layer's ffn kernel"
    l = 0
    w_a, w_b, mu_r, mu_k, mu_v, mu_sm = _pack_in(w_in[l], rwkv_mu[l])
    w2p, a2p, g2p = _pack_lora(rwkv_w2[l], rwkv_a2[l], rwkv_g2[l])
    wq1, wq2 = _pack_q(mla_w_uq[l])
    wkv = _pack_kv(mla_w_ukv[l])

    proj = _inproj(xc, row(attn_norm_g[l]), w_a, w_b)
    y_r, wo, wg, wu, wd = _rwkv(
        proj, bsz, seq, row(mu_r), row(mu_k), row(mu_v), row(mu_sm),
        row(rwkv_w0[l]), row(rwkv_a0[l]), row(rwkv_k_k[l]), row(rwkv_k_a[l]),
        row(rwkv_r_k[l]), row(rwkv_gn_w[l]), row(rwkv_gn_b[l]), w2p, a2p, g2p,
        to_bf16=(w_out[l], ffn_w_gate[l], ffn_w_up[l], ffn_w_down[l]))
    q = _mla_q(proj, pos, invf, row(mla_q_norm_g[l]), wq1, wq2)
    k, v = _mla_kv(proj, pos, invf, row(mla_kv_norm_g[l]), wkv)
    y_m = _attn(q, k, v, bsz, seq)
    x1, h2 = _outproj(y_r, y_m, wo, xc, row(ffn_norm_g[l]))
    act = _ffn_act(h2, wg, wu, ffn_conv_w[l], row(ffn_conv_b[l]), seq)
    out = _ffn_down(act, wd, x1, row(final_norm_g))
    return out.reshape(bsz, seq, d)
```

```python
import functools
import math

import jax
import jax.numpy as jnp
from jax import lax
from jax.experimental import pallas as pl
from jax.experimental.pallas import tpu as pltpu

F32 = jnp.float32
BF16 = jnp.bfloat16

D_MODEL = 2048
RWKV_HEAD = 64
RWKV_WIDTH = 1024
RWKV_HEADS = 16
DECAY_LORA = 64
AAA_LORA = 64
GATE_LORA = 160
GN_EPS = 64e-5
QK_NOPE = 128
QK_ROPE = 64
V_HEAD = 128
MLA_WIDTH = 1024
MLA_HEADS = 8
Q_LORA = 512
KV_LORA = 512
ROPE_THETA = 10000.0
D_FF = 5632
NORM_EPS = 1e-6
NEG_INF = -1e30

LANES = 128
SUBLANES = 8
BF16_ROWS = 16
SMALL = 512
D_IN_P = 3 * RWKV_WIDTH + Q_LORA + KV_LORA + SMALL
COL_CQ = 3 * RWKV_WIDTH // Q_LORA
COL_CKV = COL_CQ + 1
COL_SMALL = COL_CKV + 1
CHUNK = 64
PAIRS = RWKV_WIDTH // LANES
VMEM_LIMIT = 56 * 1024 * 1024
V_EXT = V_HEAD + BF16_ROWS
SCORE_SCALE = (QK_NOPE + QK_ROPE) ** -0.5 * math.log2(math.e)


def _cparams(sem):
    return pltpu.CompilerParams(dimension_semantics=sem, vmem_limit_bytes=VMEM_LIMIT)


def _rms(x, g):
    ms = jnp.mean(x * x, axis=-1, keepdims=True)
    return x * lax.rsqrt(ms + NORM_EPS) * g


def _sigmoid(x):
    return 1.0 / (1.0 + jnp.exp(-x))


def _dot(a, b):
    return jnp.dot(a, b, preferred_element_type=F32)


def _dot_nt(a, b):
    return lax.dot_general(a, b, (((1,), (1,)), ((), ())), preferred_element_type=F32)


def _dot_tn(a, b):
    return lax.dot_general(a, b, (((0,), (0,)), ((), ())), preferred_element_type=F32)


def _cast_kernel(x_ref, o_ref):
    o_ref[...] = x_ref[...].astype(o_ref.dtype)


def _cast_cols(w, ncols, tn=512):
    rows = w.shape[0]
    return pl.pallas_call(
        _cast_kernel,
        grid=(ncols // tn,),
        in_specs=[pl.BlockSpec((rows, tn), lambda j: (0, j))],
        out_specs=pl.BlockSpec((rows, tn), lambda j: (0, j)),
        out_shape=jax.ShapeDtypeStruct((rows, ncols), BF16),
        compiler_params=_cparams(("parallel",)),
        name="cast_cols",
    )(w)


def _inproj_kernel(x_ref, g_ref, wa_ref, wbt_ref, o_ref):
    h = _rms(x_ref[...], g_ref[...]).astype(BF16)
    na = wa_ref.shape[1]
    o_ref[:, :na] = _dot(h, wa_ref[...])
    o_ref[:, na:] = _dot_nt(h, wbt_ref[...])


def _inproj(x2, g, w_a, w_b, tm=512):
    t, d = x2.shape
    n = w_a.shape[1] + w_b.shape[0]
    once = pl.Buffered(1)
    return pl.pallas_call(
        _inproj_kernel,
        grid=(t // tm,),
        in_specs=[
            pl.BlockSpec((tm, d), lambda i: (i, 0)),
            pl.BlockSpec((1, d), lambda i: (0, 0)),
            pl.BlockSpec(w_a.shape, lambda i: (0, 0), pipeline_mode=once),
            pl.BlockSpec(w_b.shape, lambda i: (0, 0), pipeline_mode=once),
        ],
        out_specs=pl.BlockSpec((tm, n), lambda i: (i, 0)),
        out_shape=jax.ShapeDtypeStruct((t, n), F32),
        compiler_params=_cparams(("parallel",)),
        name="inproj",
    )(x2, g, w_a, w_b)


RWKV_INS = 18


def _rwkv_kernel(*refs, n_cast):
    (r_ref, k_ref, v_ref, sm_ref, mur_ref, muk_ref, muv_ref, musm_ref,
     w0_ref, a0_ref, kk_ref, ka_ref, rk_ref, gnw_ref, gnb_ref,
     w2_ref, a2_ref, g2_ref) = refs[:RWKV_INS]
    cast_in = refs[RWKV_INS:RWKV_INS + n_cast]
    o_ref = refs[RWKV_INS + n_cast]
    cast_out = refs[RWKV_INS + n_cast + 1:RWKV_INS + 2 * n_cast + 1]
    st_scr, pr_scr, pk_scr, pv_scr, psm_scr = refs[RWKV_INS + 2 * n_cast + 1:]
    for src, dst in zip(cast_in, cast_out):
        dst[...] = src[...].astype(dst.dtype)
    c = CHUNK
    half = RWKV_HEAD
    rows = r_ref.shape[0]
    nch = rows // c

    @pl.when(pl.program_id(1) == 0)
    def _():
        st_scr[...] = jnp.zeros_like(st_scr)
        pr_scr[...] = jnp.zeros_like(pr_scr)
        pk_scr[...] = jnp.zeros_like(pk_scr)
        pv_scr[...] = jnp.zeros_like(pv_scr)
        psm_scr[...] = jnp.zeros_like(psm_scr)

    def shift(x_ref, p_scr, mu_ref, ci):
        x = x_ref[c * ci:c * (ci + 1), :]
        prev = p_scr[...] if ci == 0 else x_ref[c * ci - 1:c * ci, :]
        row = lax.broadcasted_iota(jnp.int32, x.shape, 0)
        xp = jnp.where(row == 0, prev, pltpu.roll(x, 1, 0))
        return x + (xp - x) * mu_ref[...]

    ri = lax.broadcasted_iota(jnp.int32, (c, LANES), 0)
    li = lax.broadcasted_iota(jnp.int32, (c, LANES), 1)
    lj = li & (half - 1)
    lane_a = li < half
    strict = lj < ri
    incl = lj <= ri
    eye2 = (lj == ri).astype(F32)
    r2 = lax.broadcasted_iota(jnp.int32, (LANES, LANES), 0)
    l2 = lax.broadcasted_iota(jnp.int32, (LANES, LANES), 1)
    bd = (r2 < half) == (l2 < half)
    bd_ones = bd.astype(BF16)
    bd_ones2 = jnp.concatenate([bd_ones, bd_ones], axis=0)
    tr = lax.broadcasted_iota(jnp.int32, (c, 3 * c), 0)
    tc = lax.broadcasted_iota(jnp.int32, (c, 3 * c), 1)
    tril3 = ((tc & (c - 1)) <= tr).astype(BF16)

    def bd_stack(x):
        xb = x.astype(BF16)
        z = jnp.zeros_like(xb)
        return jnp.concatenate([jnp.where(lane_a, xb, z), jnp.where(lane_a, z, xb)], axis=0)

    def segsum(x):
        xs = jnp.concatenate([x[:, LANES * g:LANES * (g + 1)] for g in range(PAIRS)], axis=0)
        hi = xs.astype(BF16)
        lo = (xs - hi.astype(F32)).astype(BF16)
        s = _dot(jnp.concatenate([hi, lo], axis=1), bd_ones2)
        return jnp.concatenate([s[c * g:c * (g + 1), :] for g in range(PAIRS)], axis=1)

    def cumsum_rows(x):
        hi = x.astype(BF16)
        r1 = x - hi.astype(F32)
        mid = r1.astype(BF16)
        lo = (r1 - mid.astype(F32)).astype(BF16)
        return _dot(tril3, jnp.concatenate([hi, mid, lo], axis=0))

    prs = range(PAIRS)
    sls = [slice(LANES * p, LANES * (p + 1)) for p in prs]
    inv_n = 1.0 / RWKV_HEAD
    state = {"s": [st_scr[p] for p in prs]}
    pre, ind, dep = {}, {}, {}

    def prologue(ci):
        rs = shift(r_ref, pr_scr, mur_ref, ci)
        ks = shift(k_ref, pk_scr, muk_ref, ci)
        vs = shift(v_ref, pv_scr, muv_ref, ci)
        sm = shift(sm_ref, psm_scr, musm_ref, ci)
        g0 = sm[:, 0:LANES]
        zw = _dot(jnp.tanh(g0).astype(BF16), w2_ref[...])
        za = _dot(g0.astype(BF16), a2_ref[...])
        kkr = ks * kk_ref[...]
        nrm = segsum(kkr * kkr)
        gate = _dot(_sigmoid(sm[:, LANES:3 * LANES]).astype(BF16), g2_ref[...])
        yield
        ld = (-math.exp(-0.5)) * _sigmoid(w0_ref[...] + zw)
        lg = cumsum_rows(ld)
        a = _sigmoid(a0_ref[...] + za)
        kk = kkr * lax.rsqrt(jnp.maximum(nrm, 1e-24))
        kp = ks * (1.0 + (a - 1.0) * ka_ref[...])
        beta = kk * a
        bonus = segsum(rs * kp * rk_ref[...])
        yield
        lgc = lg[c - 1:c, :]
        einv = jnp.exp(-lg)
        ec = jnp.exp(lgc - lg)
        rt = rs * jnp.exp(lg)
        at = -kk * jnp.exp(lg - ld)
        pre[ci] = dict(vs=vs, gate=gate, bonus=bonus, gc=jnp.exp(lgc), rt=rt, at=at,
                       bt=beta * einv, kt=kp * einv, bh=beta * ec, kh=kp * ec)
        yield

    def independent(ci):
        q = pre[ci]
        v_p = [q["vs"][:, sl] for sl in sls]
        lhs1 = [jnp.concatenate([q["at"][:, sl], q["rt"][:, sl]], axis=0).astype(BF16) for sl in sls]
        rhs1 = [jnp.concatenate([bd_stack(q["bt"][:, sl]), bd_stack(q["kt"][:, sl])], axis=0) for sl in sls]
        pm = [_dot_nt(lhs1[p], rhs1[p]) for p in prs]
        bk = [jnp.concatenate([q["bh"][:, sl], q["kh"][:, sl]], axis=0).astype(BF16) for sl in sls]
        yield
        lab = [jnp.where(strict, m[0:c, 0:LANES], 0.0) for m in pm]
        lm = [jnp.concatenate([jnp.where(strict, m[0:c, LANES:2 * LANES], 0.0),
                               jnp.where(incl, m[c:2 * c, LANES:2 * LANES], 0.0)], axis=0).astype(BF16)
              for m in pm]
        mrb = [jnp.where(incl, m[c:2 * c, 0:LANES], 0.0).astype(BF16) for m in pm]
        lv = [_dot(lm[p], bd_stack(v_p[p])) for p in prs]
        pw = [_dot(l.astype(BF16), bd_stack(l)) for l in lab]
        tinv = [eye2 + l for l in lab]
        yield
        for _ in range(4):
            res = [_dot(pw[p].astype(BF16),
                        jnp.concatenate([bd_stack(tinv[p]), bd_stack(pw[p])], axis=1)) for p in prs]
            tinv = [tinv[p] + res[p][:, 0:LANES] for p in prs]
            pw = [res[p][:, LANES:2 * LANES] for p in prs]
            yield
        tinv = [(tinv[p] + _dot(pw[p].astype(BF16), bd_stack(tinv[p]))).astype(BF16) for p in prs]
        ind[ci] = dict(v_p=v_p, lhs1=lhs1, bk=bk, mrb=mrb, lv=lv, tinv=tinv)
        yield

    def dependent(ci):
        q, st = ind[ci], state["s"]
        xh = [_dot_nt(q["lhs1"][p], st[p].astype(BF16)) for p in prs]
        yield
        x = [xh[p][0:c] + q["lv"][p][0:c] for p in prs]
        u = [_dot(q["tinv"][p], bd_stack(x[p])) for p in prs]
        yield
        ys = [xh[p][c:2 * c] + q["lv"][p][c:2 * c] + _dot(q["mrb"][p], bd_stack(u[p])) for p in prs]
        uv = [jnp.concatenate([u[p], q["v_p"][p]], axis=0).astype(BF16) for p in prs]
        ds = [_dot_tn(uv[p], q["bk"][p]) for p in prs]
        yield
        gc = pre[ci]["gc"]
        state["s"] = [st[p] * gc[:, sls[p]] + jnp.where(bd, ds[p], 0.0) for p in prs]
        dep[ci] = jnp.concatenate(ys, axis=1)
        yield

    def epilogue(ci):
        y, q = dep[ci], pre[ci]
        mean = segsum(y) * inv_n
        yield
        d = y - mean
        var = segsum(d * d) * inv_n
        yield
        yn = d * lax.rsqrt(var + GN_EPS) * gnw_ref[...] + gnb_ref[...]
        o_ref[c * ci:c * (ci + 1), :] = ((yn + q["bonus"] * q["vs"]) * q["gate"]).astype(o_ref.dtype)
        yield

    phases = (prologue, independent, dependent, epilogue)
    done = object()
    for tick in range(nch + len(phases) - 1):
        live = [ph(tick - k) for k, ph in enumerate(phases) if 0 <= tick - k < nch]
        while live:
            live = [g for g in live if next(g, done) is not done]

    for p in prs:
        st_scr[p] = state["s"][p]
    for x_ref, p_scr in ((r_ref, pr_scr), (k_ref, pk_scr), (v_ref, pv_scr), (sm_ref, psm_scr)):
        p_scr[...] = x_ref[rows - 1:rows, :]


def _rwkv(proj, bsz, seq, mu_r, mu_k, mu_v, mu_sm, w0, a0, k_k, k_a, r_k, gn_w, gn_b, w2p, a2p, g2p,
          to_bf16=(), nch=8):
    t = proj.shape[0]
    w = RWKV_WIDTH
    rows = CHUNK * nch
    nc = seq // rows
    steps = bsz * nc
    row = lambda b, c: b * nc + c
    vec = lambda n: pl.BlockSpec((1, n), lambda b, c: (0, 0))
    full = lambda a: pl.BlockSpec(a.shape, lambda b, c: (0, 0))
    assert all(a.shape[0] % (steps * BF16_ROWS) == 0 for a in to_bf16), "row slabs must be bf16-tile aligned"
    slab = lambda a: pl.BlockSpec((a.shape[0] // steps, a.shape[1]), lambda b, c: (row(b, c), 0))
    return pl.pallas_call(
        functools.partial(_rwkv_kernel, n_cast=len(to_bf16)),
        grid=(bsz, nc),
        in_specs=[
            pl.BlockSpec((rows, w), lambda b, c: (row(b, c), 0)),
            pl.BlockSpec((rows, w), lambda b, c: (row(b, c), 1)),
            pl.BlockSpec((rows, w), lambda b, c: (row(b, c), 2)),
            pl.BlockSpec((rows, SMALL), lambda b, c: (row(b, c), COL_SMALL)),
            vec(w), vec(w), vec(w), vec(SMALL),
            vec(w), vec(w), vec(w), vec(w), vec(w), vec(w), vec(w),
            full(w2p), full(a2p), full(g2p),
        ] + [slab(a) for a in to_bf16],
        out_specs=[pl.BlockSpec((rows, w), lambda b, c: (row(b, c), 0))] + [slab(a) for a in to_bf16],
        out_shape=[jax.ShapeDtypeStruct((t, w), BF16)]
        + [jax.ShapeDtypeStruct(a.shape, BF16) for a in to_bf16],
        scratch_shapes=[
            pltpu.VMEM((PAIRS, LANES, LANES), F32),
            pltpu.VMEM((1, w), F32), pltpu.VMEM((1, w), F32), pltpu.VMEM((1, w), F32),
            pltpu.VMEM((1, SMALL), F32),
        ],
        compiler_params=_cparams(("parallel", "arbitrary")),
        name="rwkv",
    )(proj, proj, proj, proj, mu_r, mu_k, mu_v, mu_sm, w0, a0, k_k, k_a, r_k, gn_w, gn_b,
      w2p, a2p, g2p, *to_bf16)


def _rope_tables(pos_ref, invf_ref):
    ang = pos_ref[...].astype(F32) * invf_ref[...]
    keep = lax.broadcasted_iota(jnp.int32, ang.shape, 1) < QK_ROPE
    return jnp.where(keep, jnp.cos(ang), 0.0), jnp.where(keep, jnp.sin(ang), 0.0)


def _mla_proj_kernel(cq_ref, ckv_ref, sm_ref, pos_ref, invf_ref, gq_ref, gkv_ref,
                     w1_ref, w2_ref, wkv_ref, q_ref, k_ref, v_ref):
    cosm, sinm = _rope_tables(pos_ref, invf_ref)
    hw = QK_NOPE + LANES

    h = _rms(cq_ref[...], gq_ref[...]).astype(BF16)
    z1 = _dot(h, w1_ref[...])
    z2 = _dot(h, w2_ref[...])
    cosq, sinq = cosm * SCORE_SCALE, sinm * SCORE_SCALE
    for hd in range(MLA_HEADS):
        q_ref[:, hw * hd:hw * hd + QK_NOPE] = (z1[:, hw * hd:hw * hd + QK_NOPE] * SCORE_SCALE).astype(BF16)
        pe = z1[:, hw * hd + QK_NOPE:hw * (hd + 1)] * cosq + z2[:, LANES * hd:LANES * (hd + 1)] * sinq
        q_ref[:, hw * hd + QK_NOPE:hw * (hd + 1)] = pe.astype(BF16)

    h = _rms(ckv_ref[...], gkv_ref[...]).astype(BF16)
    z = _dot(h, wkv_ref[...])
    kpe = (sm_ref[:, 2 * LANES:3 * LANES] * cosm + sm_ref[:, 3 * LANES:4 * LANES] * sinm).astype(BF16)
    for hd in range(MLA_HEADS):
        k_ref[:, hw * hd:hw * hd + QK_NOPE] = z[:, QK_NOPE * hd:QK_NOPE * (hd + 1)].astype(BF16)
        k_ref[:, hw * hd + QK_NOPE:hw * (hd + 1)] = kpe
    ones = jnp.ones((V_EXT - V_HEAD, z.shape[0]), BF16)
    for hd in range(MLA_HEADS):
        vcol = MLA_HEADS * QK_NOPE + V_HEAD * hd
        v_ref[V_EXT * hd:V_EXT * hd + V_HEAD, :] = z[:, vcol:vcol + V_HEAD].T.astype(BF16)
        v_ref[V_EXT * hd + V_HEAD:V_EXT * (hd + 1), :] = ones


def _mla_proj(proj, pos, invf, gq, gkv, w1, w2, wkv, tm=512):
    t = proj.shape[0]
    nk = MLA_HEADS * (QK_NOPE + LANES)
    nv = MLA_HEADS * V_EXT
    const = lambda a: pl.BlockSpec(a.shape, lambda i: (0, 0))
    return pl.pallas_call(
        _mla_proj_kernel,
        grid=(t // tm,),
        in_specs=[
            pl.BlockSpec((tm, Q_LORA), lambda i: (i, COL_CQ)),
            pl.BlockSpec((tm, KV_LORA), lambda i: (i, COL_CKV)),
            pl.BlockSpec((tm, SMALL), lambda i: (i, COL_SMALL)),
            pl.BlockSpec((tm, 1), lambda i: (i, 0)),
            const(invf), const(gq), const(gkv), const(w1), const(w2), const(wkv),
        ],
        out_specs=[
            pl.BlockSpec((tm, nk), lambda i: (i, 0)),
            pl.BlockSpec((tm, nk), lambda i: (i, 0)),
            pl.BlockSpec((nv, tm), lambda i: (0, i)),
        ],
        out_shape=[jax.ShapeDtypeStruct((t, nk), BF16), jax.ShapeDtypeStruct((t, nk), BF16),
                   jax.ShapeDtypeStruct((nv, t), BF16)],
        compiler_params=_cparams(("parallel",)),
        name="mla_proj",
    )(proj, proj, proj, pos, invf, gq, gkv, w1, w2, wkv)


ATTN_STRIP = 64


def _attn_kernel(q_ref, k_ref, vt_ref, o_ref, *, blk, nh):
    qi = pl.program_id(2)
    hw = QK_NOPE + LANES
    hds = range(nh)

    def weights(s, m_new):
        return jnp.concatenate(
            [jnp.exp2(s[ATTN_STRIP * r:ATTN_STRIP * (r + 1)] - m_new).astype(BF16)
             for r in range(s.shape[0] // ATTN_STRIP)], axis=0)

    def pipeline(scores, softmax, values):
        m_out, acc_out = [], []
        s_next = scores(0)
        prev = None
        for h in hds:
            s_cur = s_next
            if h + 1 < nh:
                s_next = scores(h + 1)
            m_new, rest = softmax(h, s_cur)
            m_out.append(m_new)
            if prev is not None:
                acc_out.append(values(*prev))
            prev = (h, rest)
        acc_out.append(values(*prev))
        return m_out, acc_out

    def full_step(j, carry):
        m, acc = carry
        start = pl.multiple_of(j * blk, blk)

        def scores(h):
            return _dot_nt(k_ref[pl.ds(start, blk), hw * h:hw * (h + 1)], q_ref[:, hw * h:hw * (h + 1)])

        def softmax(h, s):
            m_new = jnp.maximum(m[h], jnp.max(s, axis=0, keepdims=True))
            return m_new, (jnp.exp2(m[h] - m_new), weights(s, m_new))

        def values(h, rest):
            alpha, p = rest
            return alpha * acc[h] + _dot(vt_ref[V_EXT * h:V_EXT * (h + 1), pl.ds(start, blk)], p)

        return pipeline(scores, softmax, values)

    def diag_step(carry):
        m, acc = carry
        hb = blk // 2
        lo = pl.multiple_of(qi * blk, blk)
        hi = pl.multiple_of(qi * blk + hb, hb)
        key = lax.broadcasted_iota(jnp.int32, (hb, blk), 0)
        qry = lax.broadcasted_iota(jnp.int32, (hb, blk), 1)
        causal_a = key <= qry
        causal_b = causal_a[:, :hb]

        def scores(h):
            hc = slice(hw * h, hw * (h + 1))
            return (_dot_nt(k_ref[pl.ds(lo, hb), hc], q_ref[:, hc]),
                    _dot_nt(k_ref[pl.ds(hi, hb), hc], q_ref[hb:, hc]))

        def softmax(h, s):
            sa = jnp.where(causal_a, s[0], NEG_INF)
            sb = jnp.where(causal_b, s[1], NEG_INF)
            mx = jnp.max(sa, axis=0, keepdims=True)
            mx = jnp.concatenate([mx[:, :hb], jnp.maximum(mx[:, hb:], jnp.max(sb, axis=0, keepdims=True))],
                                 axis=1)
            m_new = jnp.maximum(m[h], mx)
            return m_new, (jnp.exp2(m[h] - m_new), weights(sa, m_new), weights(sb, m_new[:, hb:]))

        def values(h, rest):
            alpha, pa, pb = rest
            vh = slice(V_EXT * h, V_EXT * (h + 1))
            a = alpha * acc[h] + _dot(vt_ref[vh, pl.ds(lo, hb)], pa)
            b = _dot(vt_ref[vh, pl.ds(hi, hb)], pb)
            return jnp.concatenate([a[:, :hb], a[:, hb:] + b], axis=1)

        return pipeline(scores, softmax, values)

    init = ([jnp.full((1, blk), NEG_INF, F32) for _ in hds], [jnp.zeros((V_EXT, blk), F32) for _ in hds])
    carry = lax.fori_loop(0, qi, full_step, init)
    _, acc = diag_step(carry)
    for h in hds:
        out = acc[h][:V_HEAD] / acc[h][V_HEAD:V_HEAD + 1]
        o_ref[:, V_HEAD * h:V_HEAD * (h + 1)] = out.T.astype(o_ref.dtype)


def _attn(q, k, v, bsz, seq, blk=512, nh=4):
    t = q.shape[0]
    nq = seq // blk
    hw = QK_NOPE + LANES
    kern = functools.partial(_attn_kernel, blk=blk, nh=nh)
    return pl.pallas_call(
        kern,
        grid=(bsz, MLA_HEADS // nh, nq),
        in_specs=[
            pl.BlockSpec((blk, hw * nh), lambda b, h, i: (b * nq + i, h)),
            pl.BlockSpec((seq, hw * nh), lambda b, h, i: (b, h)),
            pl.BlockSpec((V_EXT * nh, seq), lambda b, h, i: (h, b)),
        ],
        out_specs=pl.BlockSpec((blk, V_HEAD * nh), lambda b, h, i: (b * nq + i, h)),
        out_shape=jax.ShapeDtypeStruct((t, MLA_HEADS * V_HEAD), BF16),
        compiler_params=_cparams(("parallel", "parallel", "arbitrary")),
        name="attn",
    )(q, k, v)


def _outproj_kernel(yr_ref, ym_ref, w_ref, x_ref, g_ref, x1_ref, h2_ref):
    nr = yr_ref.shape[1]
    x1 = x_ref[...] + _dot(yr_ref[...], w_ref[:nr, :]) + _dot(ym_ref[...], w_ref[nr:, :])
    x1_ref[...] = x1
    h2_ref[...] = _rms(x1, g_ref[...]).astype(BF16)


def _outproj(yr, ym, w, x2, g, tm=512):
    t, d = x2.shape
    return pl.pallas_call(
        _outproj_kernel,
        grid=(t // tm,),
        in_specs=[
            pl.BlockSpec((tm, RWKV_WIDTH), lambda i: (i, 0)),
            pl.BlockSpec((tm, MLA_WIDTH), lambda i: (i, 0)),
            pl.BlockSpec(w.shape, lambda i: (0, 0), pipeline_mode=pl.Buffered(1)),
            pl.BlockSpec((tm, d), lambda i: (i, 0)),
            pl.BlockSpec((1, d), lambda i: (0, 0)),
        ],
        out_specs=[pl.BlockSpec((tm, d), lambda i: (i, 0)), pl.BlockSpec((tm, d), lambda i: (i, 0))],
        out_shape=[jax.ShapeDtypeStruct((t, d), F32), jax.ShapeDtypeStruct((t, d), BF16)],
        compiler_params=_cparams(("parallel",)),
        name="outproj",
    )(yr, ym, w, x2, g)


HALO = BF16_ROWS


MXU_N = 256


def _shift_down(x, first):
    r = pltpu.roll(x, 1, 0)
    top = r[0:SUBLANES]
    row = lax.broadcasted_iota(jnp.int32, top.shape, 0)
    return jnp.concatenate([jnp.where(row == 0, first, top), r[SUBLANES:]], axis=0)


def _ffn_act_kernel(h_ref, halo_ref, wg_ref, wu_ref, cw_ref, cb_ref, a_ref, hh_scr, *, tm, seq):
    i = pl.program_id(1)
    keep = jnp.where((i * tm) % seq != 0, 1.0, 0.0)
    hh_scr[0:HALO, :] = (halo_ref[...] * keep.astype(BF16)).astype(BF16)
    hh_scr[HALO:, :] = h_ref[...]
    n_sub = wg_ref.shape[1] // MXU_N

    def gate_dot(s):
        return _dot(hh_scr[...], wg_ref[:, MXU_N * s:MXU_N * (s + 1)])

    def up_dot(s):
        return _dot(h_ref[...], wu_ref[:, MXU_N * s:MXU_N * (s + 1)])

    ge, u = gate_dot(0), up_dot(0)
    for s in range(n_sub):
        cs = slice(MXU_N * s, MXU_N * (s + 1))
        more = s + 1 < n_sub
        ge_next = gate_dot(s + 1) if more else None
        c0, c1, c2 = cw_ref[0:1, cs], cw_ref[1:2, cs], cw_ref[2:3, cs]
        g = ge[HALO:]
        gm1 = ge[HALO - 1:HALO]
        gm2 = ge[HALO - 2:HALO - 1]
        w = c1 * g + _shift_down(c0 * g, c0 * gm1)
        conv = c2 * g + _shift_down(w, c1 * gm1 + c0 * gm2) + cb_ref[:, cs]
        u_next = up_dot(s + 1) if more else None
        a_ref[:, cs] = (conv * _sigmoid(conv) * u).astype(BF16)
        ge, u = ge_next, u_next


def _ffn_act(h2, wg, wu, cw, cb, seq, tm=1024, n_col=2):
    t, d = h2.shape
    f = wg.shape[1]
    tf = f // n_col
    kern = functools.partial(_ffn_act_kernel, tm=tm, seq=seq)
    hb = tm // HALO
    once = pl.Buffered(1)
    return pl.pallas_call(
        kern,
        grid=(n_col, t // tm),
        in_specs=[
            pl.BlockSpec((tm, d), lambda j, i: (i, 0)),
            pl.BlockSpec((HALO, d), lambda j, i: (jnp.maximum(i * hb - 1, 0), 0)),
            pl.BlockSpec((d, tf), lambda j, i: (0, j), pipeline_mode=once),
            pl.BlockSpec((d, tf), lambda j, i: (0, j), pipeline_mode=once),
            pl.BlockSpec((3, tf), lambda j, i: (0, j)),
            pl.BlockSpec((1, tf), lambda j, i: (0, j)),
        ],
        out_specs=pl.BlockSpec((tm, tf), lambda j, i: (i, j)),
        out_shape=jax.ShapeDtypeStruct((t, f), BF16),
        scratch_shapes=[pltpu.VMEM((tm + HALO, d), BF16)],
        compiler_params=_cparams(("parallel", "parallel")),
        name="ffn_act",
    )(h2, h2, wg, wu, cw, cb)


def _ffn_down_kernel(a_ref, wd_ref, x1_ref, gf_ref, o_ref):
    o_ref[...] = _rms(x1_ref[...] + _dot(a_ref[...], wd_ref[...]), gf_ref[...])


def _ffn_down(act, wd, x1, gf, tm=256):
    t, d = x1.shape
    f = act.shape[1]
    return pl.pallas_call(
        _ffn_down_kernel,
        grid=(t // tm,),
        in_specs=[
            pl.BlockSpec((tm, f), lambda i: (i, 0)),
            pl.BlockSpec((f, d), lambda i: (0, 0), pipeline_mode=pl.Buffered(1)),
            pl.BlockSpec((tm, d), lambda i: (i, 0)),
            pl.BlockSpec((1, d), lambda i: (0, 0)),
        ],
        out_specs=pl.BlockSpec((tm, d), lambda i: (i, 0)),
        out_shape=jax.ShapeDtypeStruct((t, d), F32),
        compiler_params=_cparams(("parallel",)),
        name="ffn_down",
    )(act, wd, x1, gf)


def _pack_in(w_in, mu):
    d = w_in.shape[0]
    o = 3 * RWKV_WIDTH
    og = o + DECAY_LORA + AAA_LORA
    wt = w_in[:, o:].T
    lo = DECAY_LORA + AAA_LORA
    wl_al = wt[:lo]
    gl = wt[lo:lo + GATE_LORA]
    oq = lo + GATE_LORA
    cq = wt[oq:oq + Q_LORA]
    ckv = wt[oq + Q_LORA:oq + Q_LORA + KV_LORA]
    kpe = wt[oq + Q_LORA + KV_LORA:]
    hr = QK_ROPE // 2
    kpe_rot = jnp.concatenate([-kpe[hr:], kpe[:hr]], axis=0)
    z = lambda n: jnp.zeros((n, d), w_in.dtype)
    rest = GATE_LORA - LANES
    w_a = _cast_cols(w_in, o)
    w_bt = jnp.concatenate(
        [cq, ckv, wl_al, gl[:LANES], kpe, gl[LANES:], z(LANES - QK_ROPE - rest), kpe_rot, z(LANES - QK_ROPE)],
        axis=0).astype(BF16)
    zm = lambda n: jnp.zeros((n,), mu.dtype)
    mu_sm = jnp.concatenate(
        [mu[o:og], mu[og:og + LANES], zm(QK_ROPE), mu[og + LANES:og + GATE_LORA],
         zm(LANES - QK_ROPE - rest), zm(LANES)])
    return w_a, w_bt, mu[:RWKV_WIDTH], mu[RWKV_WIDTH:2 * RWKV_WIDTH], mu[2 * RWKV_WIDTH:o], mu_sm


def _pack_lora(w2, a2, g2):
    n = w2.shape[1]
    z = lambda r: jnp.zeros((r, n), w2.dtype)
    rest = GATE_LORA - LANES
    w2p = jnp.concatenate([w2, z(LANES - DECAY_LORA)], axis=0)
    a2p = jnp.concatenate([z(DECAY_LORA), a2], axis=0)
    g2p = jnp.concatenate([g2[:LANES], z(QK_ROPE), g2[LANES:], z(LANES - QK_ROPE - rest)], axis=0)
    return w2p.astype(BF16), a2p.astype(BF16), g2p.astype(BF16)


def _pack_q(w_uq):
    d = w_uq.shape[0]
    hw = QK_NOPE + QK_ROPE
    hr = QK_ROPE // 2
    w = w_uq.reshape(d, MLA_HEADS, hw)
    pe = w[:, :, QK_NOPE:]
    z = jnp.zeros((d, MLA_HEADS, LANES - QK_ROPE), w_uq.dtype)
    w1 = jnp.concatenate([w[:, :, :QK_NOPE], pe, z], axis=2).reshape(d, -1)
    w2 = jnp.concatenate([-pe[:, :, hr:], pe[:, :, :hr], z], axis=2).reshape(d, -1)
    return w1.astype(BF16), w2.astype(BF16)


def _pack_kv(w_ukv):
    d = w_ukv.shape[0]
    w = w_ukv.reshape(d, MLA_HEADS, QK_NOPE + V_HEAD)
    return jnp.concatenate([w[:, :, :QK_NOPE].reshape(d, -1), w[:, :, QK_NOPE:].reshape(d, -1)],
                           axis=1).astype(BF16)


def kernel(x, positions, attn_norm_g, w_in, rwkv_mu, rwkv_w0, rwkv_w2, rwkv_a0, rwkv_a2, rwkv_g2,
           rwkv_k_k, rwkv_k_a, rwkv_r_k, rwkv_gn_w, rwkv_gn_b, mla_q_norm_g, mla_w_uq,
           mla_kv_norm_g, mla_w_ukv, w_out, ffn_norm_g, ffn_w_gate, ffn_w_up, ffn_conv_w,
           ffn_conv_b, ffn_w_down, final_norm_g):
    bsz, seq, d = x.shape
    t = bsz * seq
    depth = w_in.shape[0]
    row = lambda a: a.reshape(1, -1)
    xc = x.reshape(t, d)
    pos = positions.reshape(t, 1)
    hr = QK_ROPE // 2
    inv_freq = ROPE_THETA ** (-jnp.arange(hr, dtype=F32) / hr)
    invf = jnp.concatenate([inv_freq, inv_freq, jnp.zeros((LANES - QK_ROPE,), F32)]).reshape(1, LANES)

    assert depth == 1, "the final RMSNorm is fused into the (single) layer's ffn kernel"
    l = 0
    w_a, w_b, mu_r, mu_k, mu_v, mu_sm = _pack_in(w_in[l], rwkv_mu[l])
    w2p, a2p, g2p = _pack_lora(rwkv_w2[l], rwkv_a2[l], rwkv_g2[l])
    wq1, wq2 = _pack_q(mla_w_uq[l])
    wkv = _pack_kv(mla_w_ukv[l])

    proj = _inproj(xc, row(attn_norm_g[l]), w_a, w_b)
    y_r, wo, wg, wu, wd = _rwkv(
        proj, bsz, seq, row(mu_r), row(mu_k), row(mu_v), row(mu_sm),
        row(rwkv_w0[l]), row(rwkv_a0[l]), row(rwkv_k_k[l]), row(rwkv_k_a[l]),
        row(rwkv_r_k[l]), row(rwkv_gn_w[l]), row(rwkv_gn_b[l]), w2p, a2p, g2p,
        to_bf16=(w_out[l], ffn_w_gate[l], ffn_w_up[l], ffn_w_down[l]))
    q, k, v = _mla_proj(proj, pos, invf, row(mla_q_norm_g[l]), row(mla_kv_norm_g[l]), wq1, wq2, wkv)
    y_m = _attn(q, k, v, bsz, seq)
    x1, h2 = _outproj(y_r, y_m, wo, xc, row(ffn_norm_g[l]))
    act = _ffn_act(h2, wg, wu, ffn_conv_w[l], row(ffn_conv_b[l]), seq)
    out = _ffn_down(act, wd, x1, row(final_norm_g))
    return out.reshape(bsz, seq, d)
```

```python
import functools
import math

import jax
import jax.numpy as jnp
from jax import lax
from jax.experimental import pallas as pl
from jax.experimental.pallas import tpu as pltpu

F32 = jnp.float32
BF16 = jnp.bfloat16

D_MODEL = 2048
RWKV_HEAD = 64
RWKV_WIDTH = 1024
RWKV_HEADS = 16
DECAY_LORA = 64
AAA_LORA = 64
GATE_LORA = 160
GN_EPS = 64e-5
QK_NOPE = 128
QK_ROPE = 64
V_HEAD = 128
MLA_WIDTH = 1024
MLA_HEADS = 8
Q_LORA = 512
KV_LORA = 512
ROPE_THETA = 10000.0
D_FF = 5632
NORM_EPS = 1e-6
NEG_INF = -1e30

LANES = 128
SUBLANES = 8
BF16_ROWS = 16
SMALL = 512
D_IN_P = 3 * RWKV_WIDTH + Q_LORA + KV_LORA + SMALL
COL_CQ = 3 * RWKV_WIDTH // Q_LORA
COL_CKV = COL_CQ + 1
COL_SMALL = COL_CKV + 1
CHUNK = 64
PAIRS = RWKV_WIDTH // LANES
VMEM_LIMIT = 56 * 1024 * 1024
V_EXT = V_HEAD + BF16_ROWS
SCORE_SCALE = (QK_NOPE + QK_ROPE) ** -0.5 * math.log2(math.e)


def _cparams(sem):
    return pltpu.CompilerParams(dimension_semantics=sem, vmem_limit_bytes=VMEM_LIMIT)


def _rms(x, g):
    ms = jnp.mean(x * x, axis=-1, keepdims=True)
    return x * lax.rsqrt(ms + NORM_EPS) * g


def _sigmoid(x):
    return 1.0 / (1.0 + jnp.exp(-x))


def _dot(a, b):
    return jnp.dot(a, b, preferred_element_type=F32)


def _dot_nt(a, b):
    return lax.dot_general(a, b, (((1,), (1,)), ((), ())), preferred_element_type=F32)


def _dot_tn(a, b):
    return lax.dot_general(a, b, (((0,), (0,)), ((), ())), preferred_element_type=F32)


def _inproj_kernel(x_ref, g_ref, wa_ref, wbt_ref, o_ref):
    h = _rms(x_ref[...], g_ref[...]).astype(BF16)
    na = wa_ref.shape[1]
    o_ref[:, :na] = _dot(h, wa_ref[...])
    o_ref[:, na:] = _dot_nt(h, wbt_ref[...])


def _inproj(x2, g, w_a, w_b, tm=512):
    t, d = x2.shape
    n = w_a.shape[1] + w_b.shape[0]
    once = pl.Buffered(1)
    return pl.pallas_call(
        _inproj_kernel,
        grid=(t // tm,),
        in_specs=[
            pl.BlockSpec((tm, d), lambda i: (i, 0)),
            pl.BlockSpec((1, d), lambda i: (0, 0)),
            pl.BlockSpec(w_a.shape, lambda i: (0, 0), pipeline_mode=once),
            pl.BlockSpec(w_b.shape, lambda i: (0, 0), pipeline_mode=once),
        ],
        out_specs=pl.BlockSpec((tm, n), lambda i: (i, 0)),
        out_shape=jax.ShapeDtypeStruct((t, n), F32),
        compiler_params=_cparams(("parallel",)),
        name="inproj",
    )(x2, g, w_a, w_b)


RWKV_INS = 18


def _rwkv_kernel(*refs, n_cast):
    (r_ref, k_ref, v_ref, sm_ref, mur_ref, muk_ref, muv_ref, musm_ref,
     w0_ref, a0_ref, kk_ref, ka_ref, rk_ref, gnw_ref, gnb_ref,
     w2_ref, a2_ref, g2_ref) = refs[:RWKV_INS]
    cast_in = refs[RWKV_INS:RWKV_INS + n_cast]
    o_ref = refs[RWKV_INS + n_cast]
    cast_out = refs[RWKV_INS + n_cast + 1:RWKV_INS + 2 * n_cast + 1]
    st_scr, pr_scr, pk_scr, pv_scr, psm_scr = refs[RWKV_INS + 2 * n_cast + 1:]
    for src, dst in zip(cast_in, cast_out):
        dst[...] = src[...].astype(dst.dtype)
    c = CHUNK
    half = RWKV_HEAD
    rows = r_ref.shape[0]
    nch = rows // c

    @pl.when(pl.program_id(1) == 0)
    def _():
        st_scr[...] = jnp.zeros_like(st_scr)
        pr_scr[...] = jnp.zeros_like(pr_scr)
        pk_scr[...] = jnp.zeros_like(pk_scr)
        pv_scr[...] = jnp.zeros_like(pv_scr)
        psm_scr[...] = jnp.zeros_like(psm_scr)

    def shift(x_ref, p_scr, mu_ref, ci):
        x = x_ref[c * ci:c * (ci + 1), :]
        prev = p_scr[...] if ci == 0 else x_ref[c * ci - 1:c * ci, :]
        row = lax.broadcasted_iota(jnp.int32, x.shape, 0)
        xp = jnp.where(row == 0, prev, pltpu.roll(x, 1, 0))
        return x + (xp - x) * mu_ref[...]

    ri = lax.broadcasted_iota(jnp.int32, (c, LANES), 0)
    li = lax.broadcasted_iota(jnp.int32, (c, LANES), 1)
    lj = li & (half - 1)
    lane_a = li < half
    strict = lj < ri
    incl = lj <= ri
    eye2 = (lj == ri).astype(F32)
    r2 = lax.broadcasted_iota(jnp.int32, (LANES, LANES), 0)
    l2 = lax.broadcasted_iota(jnp.int32, (LANES, LANES), 1)
    bd = (r2 < half) == (l2 < half)
    bd_ones = bd.astype(BF16)
    bd_ones2 = jnp.concatenate([bd_ones, bd_ones], axis=0)
    tr = lax.broadcasted_iota(jnp.int32, (c, 3 * c), 0)
    tc = lax.broadcasted_iota(jnp.int32, (c, 3 * c), 1)
    tril3 = ((tc & (c - 1)) <= tr).astype(BF16)

    def bd_stack(x):
        xb = x.astype(BF16)
        z = jnp.zeros_like(xb)
        return jnp.concatenate([jnp.where(lane_a, xb, z), jnp.where(lane_a, z, xb)], axis=0)

    def segsum(x):
        xs = jnp.concatenate([x[:, LANES * g:LANES * (g + 1)] for g in range(PAIRS)], axis=0)
        hi = xs.astype(BF16)
        lo = (xs - hi.astype(F32)).astype(BF16)
        s = _dot(jnp.concatenate([hi, lo], axis=1), bd_ones2)
        return jnp.concatenate([s[c * g:c * (g + 1), :] for g in range(PAIRS)], axis=1)

    def cumsum_rows(x):
        hi = x.astype(BF16)
        r1 = x - hi.astype(F32)
        mid = r1.astype(BF16)
        lo = (r1 - mid.astype(F32)).astype(BF16)
        return _dot(tril3, jnp.concatenate([hi, mid, lo], axis=0))

    prs = range(PAIRS)
    sls = [slice(LANES * p, LANES * (p + 1)) for p in prs]
    inv_n = 1.0 / RWKV_HEAD
    state = {"s": [st_scr[p] for p in prs]}
    pre, ind, dep = {}, {}, {}

    def prologue(ci):
        rs = shift(r_ref, pr_scr, mur_ref, ci)
        ks = shift(k_ref, pk_scr, muk_ref, ci)
        vs = shift(v_ref, pv_scr, muv_ref, ci)
        sm = shift(sm_ref, psm_scr, musm_ref, ci)
        g0 = sm[:, 0:LANES]
        zw = _dot(jnp.tanh(g0).astype(BF16), w2_ref[...])
        za = _dot(g0.astype(BF16), a2_ref[...])
        kkr = ks * kk_ref[...]
        nrm = segsum(kkr * kkr)
        gate = _dot(_sigmoid(sm[:, LANES:3 * LANES]).astype(BF16), g2_ref[...])
        yield
        ld = (-math.exp(-0.5)) * _sigmoid(w0_ref[...] + zw)
        lg = cumsum_rows(ld)
        a = _sigmoid(a0_ref[...] + za)
        kk = kkr * lax.rsqrt(jnp.maximum(nrm, 1e-24))
        kp = ks * (1.0 + (a - 1.0) * ka_ref[...])
        beta = kk * a
        bonus = segsum(rs * kp * rk_ref[...])
        yield
        lgc = lg[c - 1:c, :]
        einv = jnp.exp(-lg)
        ec = jnp.exp(lgc - lg)
        rt = rs * jnp.exp(lg)
        at = -kk * jnp.exp(lg - ld)
        pre[ci] = dict(vs=vs, gate=gate, bonus=bonus, gc=jnp.exp(lgc), rt=rt, at=at,
                       bt=beta * einv, kt=kp * einv, bh=beta * ec, kh=kp * ec)
        yield

    def independent(ci):
        q = pre[ci]
        v_p = [q["vs"][:, sl] for sl in sls]
        lhs1 = [jnp.concatenate([q["at"][:, sl], q["rt"][:, sl]], axis=0).astype(BF16) for sl in sls]
        rhs1 = [jnp.concatenate([bd_stack(q["bt"][:, sl]), bd_stack(q["kt"][:, sl])], axis=0) for sl in sls]
        pm = [_dot_nt(lhs1[p], rhs1[p]) for p in prs]
        bk = [jnp.concatenate([q["bh"][:, sl], q["kh"][:, sl]], axis=0).astype(BF16) for sl in sls]
        yield
        lab = [jnp.where(strict, m[0:c, 0:LANES], 0.0) for m in pm]
        lm = [jnp.concatenate([jnp.where(strict, m[0:c, LANES:2 * LANES], 0.0),
                               jnp.where(incl, m[c:2 * c, LANES:2 * LANES], 0.0)], axis=0).astype(BF16)
              for m in pm]
        mrb = [jnp.where(incl, m[c:2 * c, 0:LANES], 0.0).astype(BF16) for m in pm]
        lv = [_dot(lm[p], bd_stack(v_p[p])) for p in prs]
        pw = [_dot(l.astype(BF16), bd_stack(l)) for l in lab]
        tinv = [eye2 + l for l in lab]
        yield
        for _ in range(4):
            res = [_dot(pw[p].astype(BF16),
                        jnp.concatenate([bd_stack(tinv[p]), bd_stack(pw[p])], axis=1)) for p in prs]
            tinv = [tinv[p] + res[p][:, 0:LANES] for p in prs]
            pw = [res[p][:, LANES:2 * LANES] for p in prs]
            yield
        tinv = [(tinv[p] + _dot(pw[p].astype(BF16), bd_stack(tinv[p]))).astype(BF16) for p in prs]
        ind[ci] = dict(v_p=v_p, lhs1=lhs1, bk=bk, mrb=mrb, lv=lv, tinv=tinv)
        yield

    def dependent(ci):
        q, st = ind[ci], state["s"]
        xh = [_dot_nt(q["lhs1"][p], st[p].astype(BF16)) for p in prs]
        yield
        x = [xh[p][0:c] + q["lv"][p][0:c] for p in prs]
        u = [_dot(q["tinv"][p], bd_stack(x[p])) for p in prs]
        yield
        ys = [xh[p][c:2 * c] + q["lv"][p][c:2 * c] + _dot(q["mrb"][p], bd_stack(u[p])) for p in prs]
        uv = [jnp.concatenate([u[p], q["v_p"][p]], axis=0).astype(BF16) for p in prs]
        ds = [_dot_tn(uv[p], q["bk"][p]) for p in prs]
        yield
        gc = pre[ci]["gc"]
        state["s"] = [st[p] * gc[:, sls[p]] + jnp.where(bd, ds[p], 0.0) for p in prs]
        dep[ci] = jnp.concatenate(ys, axis=1)
        yield

    def epilogue(ci):
        y, q = dep[ci], pre[ci]
        mean = segsum(y) * inv_n
        yield
        d = y - mean
        var = segsum(d * d) * inv_n
        yield
        yn = d * lax.rsqrt(var + GN_EPS) * gnw_ref[...] + gnb_ref[...]
        o_ref[c * ci:c * (ci + 1), :] = ((yn + q["bonus"] * q["vs"]) * q["gate"]).astype(o_ref.dtype)
        yield

    phases = (prologue, independent, dependent, epilogue)
    done = object()
    for tick in range(nch + len(phases) - 1):
        live = [ph(tick - k) for k, ph in enumerate(phases) if 0 <= tick - k < nch]
        while live:
            live = [g for g in live if next(g, done) is not done]

    for p in prs:
        st_scr[p] = state["s"][p]
    for x_ref, p_scr in ((r_ref, pr_scr), (k_ref, pk_scr), (v_ref, pv_scr), (sm_ref, psm_scr)):
        p_scr[...] = x_ref[rows - 1:rows, :]


def _rwkv(proj, bsz, seq, mu_r, mu_k, mu_v, mu_sm, w0, a0, k_k, k_a, r_k, gn_w, gn_b, w2p, a2p, g2p,
          to_bf16=(), nch=8):
    t = proj.shape[0]
    w = RWKV_WIDTH
    rows = CHUNK * nch
    nc = seq // rows
    steps = bsz * nc
    row = lambda b, c: b * nc + c
    vec = lambda n: pl.BlockSpec((1, n), lambda b, c: (0, 0))
    full = lambda a: pl.BlockSpec(a.shape, lambda b, c: (0, 0))
    assert all(a.shape[0] % (steps * BF16_ROWS) == 0 for a in to_bf16), "row slabs must be bf16-tile aligned"
    slab = lambda a: pl.BlockSpec((a.shape[0] // steps, a.shape[1]), lambda b, c: (row(b, c), 0))
    return pl.pallas_call(
        functools.partial(_rwkv_kernel, n_cast=len(to_bf16)),
        grid=(bsz, nc),
        in_specs=[
            pl.BlockSpec((rows, w), lambda b, c: (row(b, c), 0)),
            pl.BlockSpec((rows, w), lambda b, c: (row(b, c), 1)),
            pl.BlockSpec((rows, w), lambda b, c: (row(b, c), 2)),
            pl.BlockSpec((rows, SMALL), lambda b, c: (row(b, c), COL_SMALL)),
            vec(w), vec(w), vec(w), vec(SMALL),
            vec(w), vec(w), vec(w), vec(w), vec(w), vec(w), vec(w),
            full(w2p), full(a2p), full(g2p),
        ] + [slab(a) for a in to_bf16],
        out_specs=[pl.BlockSpec((rows, w), lambda b, c: (row(b, c), 0))] + [slab(a) for a in to_bf16],
        out_shape=[jax.ShapeDtypeStruct((t, w), BF16)]
        + [jax.ShapeDtypeStruct(a.shape, BF16) for a in to_bf16],
        scratch_shapes=[
            pltpu.VMEM((PAIRS, LANES, LANES), F32),
            pltpu.VMEM((1, w), F32), pltpu.VMEM((1, w), F32), pltpu.VMEM((1, w), F32),
            pltpu.VMEM((1, SMALL), F32),
        ],
        compiler_params=_cparams(("parallel", "arbitrary")),
        name="rwkv",
    )(proj, proj, proj, proj, mu_r, mu_k, mu_v, mu_sm, w0, a0, k_k, k_a, r_k, gn_w, gn_b,
      w2p, a2p, g2p, *to_bf16)


def _rope_tables(pos_ref, invf_ref):
    ang = pos_ref[...].astype(F32) * invf_ref[...]
    keep = lax.broadcasted_iota(jnp.int32, ang.shape, 1) < QK_ROPE
    return jnp.where(keep, jnp.cos(ang), 0.0), jnp.where(keep, jnp.sin(ang), 0.0)


def _mla_proj_kernel(cq_ref, ckv_ref, sm_ref, pos_ref, invf_ref, gq_ref, gkv_ref,
                     w1_ref, w2_ref, wkv_ref, q_ref, k_ref, v_ref):
    cosm, sinm = _rope_tables(pos_ref, invf_ref)
    hw = QK_NOPE + LANES

    h = _rms(cq_ref[...], gq_ref[...]).astype(BF16)
    z1 = _dot(h, w1_ref[...])
    z2 = _dot(h, w2_ref[...])
    cosq, sinq = cosm * SCORE_SCALE, sinm * SCORE_SCALE
    for hd in range(MLA_HEADS):
        q_ref[:, hw * hd:hw * hd + QK_NOPE] = (z1[:, hw * hd:hw * hd + QK_NOPE] * SCORE_SCALE).astype(BF16)
        pe = z1[:, hw * hd + QK_NOPE:hw * (hd + 1)] * cosq + z2[:, LANES * hd:LANES * (hd + 1)] * sinq
        q_ref[:, hw * hd + QK_NOPE:hw * (hd + 1)] = pe.astype(BF16)

    h = _rms(ckv_ref[...], gkv_ref[...]).astype(BF16)
    z = _dot(h, wkv_ref[...])
    kpe = (sm_ref[:, 2 * LANES:3 * LANES] * cosm + sm_ref[:, 3 * LANES:4 * LANES] * sinm).astype(BF16)
    for hd in range(MLA_HEADS):
        k_ref[:, hw * hd:hw * hd + QK_NOPE] = z[:, QK_NOPE * hd:QK_NOPE * (hd + 1)].astype(BF16)
        k_ref[:, hw * hd + QK_NOPE:hw * (hd + 1)] = kpe
    ones = jnp.ones((V_EXT - V_HEAD, z.shape[0]), BF16)
    for hd in range(MLA_HEADS):
        vcol = MLA_HEADS * QK_NOPE + V_HEAD * hd
        v_ref[V_EXT * hd:V_EXT * hd + V_HEAD, :] = z[:, vcol:vcol + V_HEAD].T.astype(BF16)
        v_ref[V_EXT * hd + V_HEAD:V_EXT * (hd + 1), :] = ones


def _mla_proj(proj, pos, invf, gq, gkv, w1, w2, wkv, tm=512):
    t = proj.shape[0]
    nk = MLA_HEADS * (QK_NOPE + LANES)
    nv = MLA_HEADS * V_EXT
    const = lambda a: pl.BlockSpec(a.shape, lambda i: (0, 0))
    return pl.pallas_call(
        _mla_proj_kernel,
        grid=(t // tm,),
        in_specs=[
            pl.BlockSpec((tm, Q_LORA), lambda i: (i, COL_CQ)),
            pl.BlockSpec((tm, KV_LORA), lambda i: (i, COL_CKV)),
            pl.BlockSpec((tm, SMALL), lambda i: (i, COL_SMALL)),
            pl.BlockSpec((tm, 1), lambda i: (i, 0)),
            const(invf), const(gq), const(gkv), const(w1), const(w2), const(wkv),
        ],
        out_specs=[
            pl.BlockSpec((tm, nk), lambda i: (i, 0)),
            pl.BlockSpec((tm, nk), lambda i: (i, 0)),
            pl.BlockSpec((nv, tm), lambda i: (0, i)),
        ],
        out_shape=[jax.ShapeDtypeStruct((t, nk), BF16), jax.ShapeDtypeStruct((t, nk), BF16),
                   jax.ShapeDtypeStruct((nv, t), BF16)],
        compiler_params=_cparams(("parallel",)),
        name="mla_proj",
    )(proj, proj, proj, pos, invf, gq, gkv, w1, w2, wkv)


ATTN_STRIP = 64


def _attn_kernel(q_ref, k_ref, vt_ref, o_ref, *, blk, nh):
    qi = pl.program_id(2)
    hw = QK_NOPE + LANES
    hds = range(nh)

    def weights(s, m_new):
        return jnp.concatenate(
            [jnp.exp2(s[ATTN_STRIP * r:ATTN_STRIP * (r + 1)] - m_new).astype(BF16)
             for r in range(s.shape[0] // ATTN_STRIP)], axis=0)

    def pipeline(scores, softmax, values):
        m_out, acc_out = [], []
        s_next = scores(0)
        prev = None
        for h in hds:
            s_cur = s_next
            if h + 1 < nh:
                s_next = scores(h + 1)
            m_new, rest = softmax(h, s_cur)
            m_out.append(m_new)
            if prev is not None:
                acc_out.append(values(*prev))
            prev = (h, rest)
        acc_out.append(values(*prev))
        return m_out, acc_out

    def full_step(j, carry):
        m, acc = carry
        start = pl.multiple_of(j * blk, blk)

        def scores(h):
            return _dot_nt(k_ref[pl.ds(start, blk), hw * h:hw * (h + 1)], q_ref[:, hw * h:hw * (h + 1)])

        def softmax(h, s):
            m_new = jnp.maximum(m[h], jnp.max(s, axis=0, keepdims=True))
            return m_new, (jnp.exp2(m[h] - m_new), weights(s, m_new))

        def values(h, rest):
            alpha, p = rest
            return alpha * acc[h] + _dot(vt_ref[V_EXT * h:V_EXT * (h + 1), pl.ds(start, blk)], p)

        return pipeline(scores, softmax, values)

    def diag_step(carry):
        m, acc = carry
        hb = blk // 2
        lo = pl.multiple_of(qi * blk, blk)
        hi = pl.multiple_of(qi * blk + hb, hb)
        key = lax.broadcasted_iota(jnp.int32, (hb, blk), 0)
        qry = lax.broadcasted_iota(jnp.int32, (hb, blk), 1)
        causal_a = key <= qry
        causal_b = causal_a[:, :hb]

        def scores(h):
            hc = slice(hw * h, hw * (h + 1))
            return (_dot_nt(k_ref[pl.ds(lo, hb), hc], q_ref[:, hc]),
                    _dot_nt(k_ref[pl.ds(hi, hb), hc], q_ref[hb:, hc]))

        def softmax(h, s):
            sa = jnp.where(causal_a, s[0], NEG_INF)
            sb = jnp.where(causal_b, s[1], NEG_INF)
            mx = jnp.max(sa, axis=0, keepdims=True)
            mx = jnp.concatenate([mx[:, :hb], jnp.maximum(mx[:, hb:], jnp.max(sb, axis=0, keepdims=True))],
                                 axis=1)
            m_new = jnp.maximum(m[h], mx)
            return m_new, (jnp.exp2(m[h] - m_new), weights(sa, m_new), weights(sb, m_new[:, hb:]))

        def values(h, rest):
            alpha, pa, pb = rest
            vh = slice(V_EXT * h, V_EXT * (h + 1))
            a = alpha * acc[h] + _dot(vt_ref[vh, pl.ds(lo, hb)], pa)
            b = _dot(vt_ref[vh, pl.ds(hi, hb)], pb)
            return jnp.concatenate([a[:, :hb], a[:, hb:] + b], axis=1)

        return pipeline(scores, softmax, values)

    init = ([jnp.full((1, blk), NEG_INF, F32) for _ in hds], [jnp.zeros((V_EXT, blk), F32) for _ in hds])
    carry = lax.fori_loop(0, qi, full_step, init)
    _, acc = diag_step(carry)
    for h in hds:
        out = acc[h][:V_HEAD] / acc[h][V_HEAD:V_HEAD + 1]
        o_ref[:, V_HEAD * h:V_HEAD * (h + 1)] = out.T.astype(o_ref.dtype)


def _attn(q, k, v, bsz, seq, blk=512, nh=4):
    t = q.shape[0]
    nq = seq // blk
    hw = QK_NOPE + LANES
    kern = functools.partial(_attn_kernel, blk=blk, nh=nh)
    return pl.pallas_call(
        kern,
        grid=(bsz, MLA_HEADS // nh, nq),
        in_specs=[
            pl.BlockSpec((blk, hw * nh), lambda b, h, i: (b * nq + i, h)),
            pl.BlockSpec((seq, hw * nh), lambda b, h, i: (b, h)),
            pl.BlockSpec((V_EXT * nh, seq), lambda b, h, i: (h, b)),
        ],
        out_specs=pl.BlockSpec((blk, V_HEAD * nh), lambda b, h, i: (b * nq + i, h)),
        out_shape=jax.ShapeDtypeStruct((t, MLA_HEADS * V_HEAD), BF16),
        compiler_params=_cparams(("parallel", "parallel", "arbitrary")),
        name="attn",
    )(q, k, v)


def _outproj_kernel(yr_ref, ym_ref, w_ref, x_ref, g_ref, x1_ref, h2_ref):
    nr = yr_ref.shape[1]
    x1 = x_ref[...] + _dot(yr_ref[...], w_ref[:nr, :]) + _dot(ym_ref[...], w_ref[nr:, :])
    x1_ref[...] = x1
    h2_ref[...] = _rms(x1, g_ref[...]).astype(BF16)


def _outproj(yr, ym, w, x2, g, tm=512):
    t, d = x2.shape
    return pl.pallas_call(
        _outproj_kernel,
        grid=(t // tm,),
        in_specs=[
            pl.BlockSpec((tm, RWKV_WIDTH), lambda i: (i, 0)),
            pl.BlockSpec((tm, MLA_WIDTH), lambda i: (i, 0)),
            pl.BlockSpec(w.shape, lambda i: (0, 0), pipeline_mode=pl.Buffered(1)),
            pl.BlockSpec((tm, d), lambda i: (i, 0)),
            pl.BlockSpec((1, d), lambda i: (0, 0)),
        ],
        out_specs=[pl.BlockSpec((tm, d), lambda i: (i, 0)), pl.BlockSpec((tm, d), lambda i: (i, 0))],
        out_shape=[jax.ShapeDtypeStruct((t, d), F32), jax.ShapeDtypeStruct((t, d), BF16)],
        compiler_params=_cparams(("parallel",)),
        name="outproj",
    )(yr, ym, w, x2, g)


HALO = BF16_ROWS


MXU_N = 256


def _shift_down(x, first):
    r = pltpu.roll(x, 1, 0)
    top = r[0:SUBLANES]
    row = lax.broadcasted_iota(jnp.int32, top.shape, 0)
    return jnp.concatenate([jnp.where(row == 0, first, top), r[SUBLANES:]], axis=0)


def _ffn_act_kernel(h_ref, halo_ref, wg_ref, wu_ref, cw_ref, cb_ref, a_ref, hh_scr, *, tm, seq):
    i = pl.program_id(1)
    keep = jnp.where((i * tm) % seq != 0, 1.0, 0.0)
    hh_scr[0:HALO, :] = (halo_ref[...] * keep.astype(BF16)).astype(BF16)
    hh_scr[HALO:, :] = h_ref[...]
    n_sub = wg_ref.shape[1] // MXU_N

    def gate_dot(s):
        return _dot(hh_scr[...], wg_ref[:, MXU_N * s:MXU_N * (s + 1)])

    def up_dot(s):
        return _dot(h_ref[...], wu_ref[:, MXU_N * s:MXU_N * (s + 1)])

    ge, u = gate_dot(0), up_dot(0)
    for s in range(n_sub):
        cs = slice(MXU_N * s, MXU_N * (s + 1))
        more = s + 1 < n_sub
        ge_next = gate_dot(s + 1) if more else None
        c0, c1, c2 = cw_ref[0:1, cs], cw_ref[1:2, cs], cw_ref[2:3, cs]
        g = ge[HALO:]
        gm1 = ge[HALO - 1:HALO]
        gm2 = ge[HALO - 2:HALO - 1]
        w = c1 * g + _shift_down(c0 * g, c0 * gm1)
        conv = c2 * g + _shift_down(w, c1 * gm1 + c0 * gm2) + cb_ref[:, cs]
        u_next = up_dot(s + 1) if more else None
        a_ref[:, cs] = (conv * _sigmoid(conv) * u).astype(BF16)
        ge, u = ge_next, u_next


def _ffn_act(h2, wg, wu, cw, cb, seq, tm=512, n_col=2):
    t, d = h2.shape
    f = wg.shape[1]
    tf = f // n_col
    kern = functools.partial(_ffn_act_kernel, tm=tm, seq=seq)
    hb = tm // HALO
    once = pl.Buffered(1)
    return pl.pallas_call(
        kern,
        grid=(n_col, t // tm),
        in_specs=[
            pl.BlockSpec((tm, d), lambda j, i: (i, 0)),
            pl.BlockSpec((HALO, d), lambda j, i: (jnp.maximum(i * hb - 1, 0), 0)),
            pl.BlockSpec((d, tf), lambda j, i: (0, j), pipeline_mode=once),
            pl.BlockSpec((d, tf), lambda j, i: (0, j), pipeline_mode=once),
            pl.BlockSpec((3, tf), lambda j, i: (0, j)),
            pl.BlockSpec((1, tf), lambda j, i: (0, j)),
        ],
        out_specs=pl.BlockSpec((tm, tf), lambda j, i: (i, j)),
        out_shape=jax.ShapeDtypeStruct((t, f), BF16),
        scratch_shapes=[pltpu.VMEM((tm + HALO, d), BF16)],
        compiler_params=_cparams(("parallel", "parallel")),
        name="ffn_act",
    )(h2, h2, wg, wu, cw, cb)


def _ffn_down_kernel(a_ref, wd_ref, x1_ref, gf_ref, o_ref):
    o_ref[...] = _rms(x1_ref[...] + _dot(a_ref[...], wd_ref[...]), gf_ref[...])


def _ffn_down(act, wd, x1, gf, tm=256):
    t, d = x1.shape
    f = act.shape[1]
    return pl.pallas_call(
        _ffn_down_kernel,
        grid=(t // tm,),
        in_specs=[
            pl.BlockSpec((tm, f), lambda i: (i, 0)),
            pl.BlockSpec((f, d), lambda i: (0, 0), pipeline_mode=pl.Buffered(1)),
            pl.BlockSpec((tm, d), lambda i: (i, 0)),
            pl.BlockSpec((1, d), lambda i: (0, 0)),
        ],
        out_specs=pl.BlockSpec((tm, d), lambda i: (i, 0)),
        out_shape=jax.ShapeDtypeStruct((t, d), F32),
        compiler_params=_cparams(("parallel",)),
        name="ffn_down",
    )(act, wd, x1, gf)


def _pack_in(w_in, mu):
    d = w_in.shape[0]
    o = 3 * RWKV_WIDTH
    og = o + DECAY_LORA + AAA_LORA
    wt = w_in[:, o:].T
    lo = DECAY_LORA + AAA_LORA
    wl_al = wt[:lo]
    gl = wt[lo:lo + GATE_LORA]
    oq = lo + GATE_LORA
    cq = wt[oq:oq + Q_LORA]
    ckv = wt[oq + Q_LORA:oq + Q_LORA + KV_LORA]
    kpe = wt[oq + Q_LORA + KV_LORA:]
    hr = QK_ROPE // 2
    kpe_rot = jnp.concatenate([-kpe[hr:], kpe[:hr]], axis=0)
    z = lambda n: jnp.zeros((n, d), w_in.dtype)
    rest = GATE_LORA - LANES
    w_a = w_in[:, :o].astype(BF16)
    w_bt = jnp.concatenate(
        [cq, ckv, wl_al, gl[:LANES], kpe, gl[LANES:], z(LANES - QK_ROPE - rest), kpe_rot, z(LANES - QK_ROPE)],
        axis=0).astype(BF16)
    zm = lambda n: jnp.zeros((n,), mu.dtype)
    mu_sm = jnp.concatenate(
        [mu[o:og], mu[og:og + LANES], zm(QK_ROPE), mu[og + LANES:og + GATE_LORA],
         zm(LANES - QK_ROPE - rest), zm(LANES)])
    return w_a, w_bt, mu[:RWKV_WIDTH], mu[RWKV_WIDTH:2 * RWKV_WIDTH], mu[2 * RWKV_WIDTH:o], mu_sm


def _pack_lora(w2, a2, g2):
    n = w2.shape[1]
    z = lambda r: jnp.zeros((r, n), w2.dtype)
    rest = GATE_LORA - LANES
    w2p = jnp.concatenate([w2, z(LANES - DECAY_LORA)], axis=0)
    a2p = jnp.concatenate([z(DECAY_LORA), a2], axis=0)
    g2p = jnp.concatenate([g2[:LANES], z(QK_ROPE), g2[LANES:], z(LANES - QK_ROPE - rest)], axis=0)
    return w2p.astype(BF16), a2p.astype(BF16), g2p.astype(BF16)


def _pack_q(w_uq):
    d = w_uq.shape[0]
    hw = QK_NOPE + QK_ROPE
    hr = QK_ROPE // 2
    w = w_uq.reshape(d, MLA_HEADS, hw)
    pe = w[:, :, QK_NOPE:]
    z = jnp.zeros((d, MLA_HEADS, LANES - QK_ROPE), w_uq.dtype)
    w1 = jnp.concatenate([w[:, :, :QK_NOPE], pe, z], axis=2).reshape(d, -1)
    w2 = jnp.concatenate([-pe[:, :, hr:], pe[:, :, :hr], z], axis=2).reshape(d, -1)
    return w1.astype(BF16), w2.astype(BF16)


def _pack_kv(w_ukv):
    d = w_ukv.shape[0]
    w = w_ukv.reshape(d, MLA_HEADS, QK_NOPE + V_HEAD)
    return jnp.concatenate([w[:, :, :QK_NOPE].reshape(d, -1), w[:, :, QK_NOPE:].reshape(d, -1)],
                           axis=1).astype(BF16)


def kernel(x, positions, attn_norm_g, w_in, rwkv_mu, rwkv_w0, rwkv_w2, rwkv_a0, rwkv_a2, rwkv_g2,
           rwkv_k_k, rwkv_k_a, rwkv_r_k, rwkv_gn_w, rwkv_gn_b, mla_q_norm_g, mla_w_uq,
           mla_kv_norm_g, mla_w_ukv, w_out, ffn_norm_g, ffn_w_gate, ffn_w_up, ffn_conv_w,
           ffn_conv_b, ffn_w_down, final_norm_g):
    bsz, seq, d = x.shape
    t = bsz * seq
    depth = w_in.shape[0]
    row = lambda a: a.reshape(1, -1)
    xc = x.reshape(t, d)
    pos = positions.reshape(t, 1)
    hr = QK_ROPE // 2
    inv_freq = ROPE_THETA ** (-jnp.arange(hr, dtype=F32) / hr)
    invf = jnp.concatenate([inv_freq, inv_freq, jnp.zeros((LANES - QK_ROPE,), F32)]).reshape(1, LANES)

    assert depth == 1, "the final RMSNorm is fused into the (single) layer's ffn kernel"
    l = 0
    w_a, w_b, mu_r, mu_k, mu_v, mu_sm = _pack_in(w_in[l], rwkv_mu[l])
    w2p, a2p, g2p = _pack_lora(rwkv_w2[l], rwkv_a2[l], rwkv_g2[l])
    wq1, wq2 = _pack_q(mla_w_uq[l])
    wkv = _pack_kv(mla_w_ukv[l])

    proj = _inproj(xc, row(attn_norm_g[l]), w_a, w_b)
    y_r, wo, wg, wu, wd = _rwkv(
        proj, bsz, seq, row(mu_r), row(mu_k), row(mu_v), row(mu_sm),
        row(rwkv_w0[l]), row(rwkv_a0[l]), row(rwkv_k_k[l]), row(rwkv_k_a[l]),
        row(rwkv_r_k[l]), row(rwkv_gn_w[l]), row(rwkv_gn_b[l]), w2p, a2p, g2p,
        to_bf16=(w_out[l], ffn_w_gate[l], ffn_w_up[l], ffn_w_down[l]))
    q, k, v = _mla_proj(proj, pos, invf, row(mla_q_norm_g[l]), row(mla_kv_norm_g[l]), wq1, wq2, wkv)
    y_m = _attn(q, k, v, bsz, seq)
    x1, h2 = _outproj(y_r, y_m, wo, xc, row(ffn_norm_g[l]))
    act = _ffn_act(h2, wg, wu, ffn_conv_w[l], row(ffn_conv_b[l]), seq)
    out = _ffn_down(act, wd, x1, row(final_norm_g))
    return out.reshape(bsz, seq, d)
```

```python
import functools
import math

import jax
import jax.numpy as jnp
from jax import lax
from jax.experimental import pallas as pl
from jax.experimental.pallas import tpu as pltpu

F32 = jnp.float32
BF16 = jnp.bfloat16

D_MODEL = 2048
RWKV_HEAD = 64
RWKV_WIDTH = 1024
RWKV_HEADS = 16
DECAY_LORA = 64
AAA_LORA = 64
GATE_LORA = 160
GN_EPS = 64e-5
QK_NOPE = 128
QK_ROPE = 64
V_HEAD = 128
MLA_WIDTH = 1024
MLA_HEADS = 8
Q_LORA = 512
KV_LORA = 512
ROPE_THETA = 10000.0
D_FF = 5632
NORM_EPS = 1e-6
NEG_INF = -1e30

LANES = 128
SUBLANES = 8
BF16_ROWS = 16
SMALL = 512
D_IN_P = 3 * RWKV_WIDTH + Q_LORA + KV_LORA + SMALL
COL_CQ = 3 * RWKV_WIDTH // Q_LORA
COL_CKV = COL_CQ + 1
COL_SMALL = COL_CKV + 1
CHUNK = 64
PAIRS = RWKV_WIDTH // LANES
VMEM_LIMIT = 56 * 1024 * 1024
V_EXT = V_HEAD + BF16_ROWS
SCORE_SCALE = (QK_NOPE + QK_ROPE) ** -0.5 * math.log2(math.e)


def _cparams(sem):
    return pltpu.CompilerParams(dimension_semantics=sem, vmem_limit_bytes=VMEM_LIMIT)


def _rms(x, g):
    ms = jnp.mean(x * x, axis=-1, keepdims=True)
    return x * lax.rsqrt(ms + NORM_EPS) * g


def _sigmoid(x):
    return 1.0 / (1.0 + jnp.exp(-x))


def _dot(a, b):
    return jnp.dot(a, b, preferred_element_type=F32)


def _dot_nt(a, b):
    return lax.dot_general(a, b, (((1,), (1,)), ((), ())), preferred_element_type=F32)


def _dot_tn(a, b):
    return lax.dot_general(a, b, (((0,), (0,)), ((), ())), preferred_element_type=F32)


def _inproj_kernel(x_ref, g_ref, wa_ref, wbt_ref, o_ref):
    h = _rms(x_ref[...], g_ref[...]).astype(BF16)
    na = wa_ref.shape[1]
    o_ref[:, :na] = _dot(h, wa_ref[...])
    o_ref[:, na:] = _dot_nt(h, wbt_ref[...])


def _inproj(x2, g, w_a, w_b, tm=512):
    t, d = x2.shape
    n = w_a.shape[1] + w_b.shape[0]
    once = pl.Buffered(1)
    return pl.pallas_call(
        _inproj_kernel,
        grid=(t // tm,),
        in_specs=[
            pl.BlockSpec((tm, d), lambda i: (i, 0)),
            pl.BlockSpec((1, d), lambda i: (0, 0)),
            pl.BlockSpec(w_a.shape, lambda i: (0, 0), pipeline_mode=once),
            pl.BlockSpec(w_b.shape, lambda i: (0, 0), pipeline_mode=once),
        ],
        out_specs=pl.BlockSpec((tm, n), lambda i: (i, 0)),
        out_shape=jax.ShapeDtypeStruct((t, n), F32),
        compiler_params=_cparams(("parallel",)),
        name="inproj",
    )(x2, g, w_a, w_b)


RWKV_INS = 18


def _rwkv_kernel(*refs, n_cast):
    (r_ref, k_ref, v_ref, sm_ref, mur_ref, muk_ref, muv_ref, musm_ref,
     w0_ref, a0_ref, kk_ref, ka_ref, rk_ref, gnw_ref, gnb_ref,
     w2_ref, a2_ref, g2_ref) = refs[:RWKV_INS]
    cast_in = refs[RWKV_INS:RWKV_INS + n_cast]
    o_ref = refs[RWKV_INS + n_cast]
    cast_out = refs[RWKV_INS + n_cast + 1:RWKV_INS + 2 * n_cast + 1]
    st_scr, pr_scr, pk_scr, pv_scr, psm_scr = refs[RWKV_INS + 2 * n_cast + 1:]
    for src, dst in zip(cast_in, cast_out):
        dst[...] = src[...].astype(dst.dtype)
    c = CHUNK
    half = RWKV_HEAD
    rows = r_ref.shape[0]
    nch = rows // c

    @pl.when(pl.program_id(1) == 0)
    def _():
        st_scr[...] = jnp.zeros_like(st_scr)
        pr_scr[...] = jnp.zeros_like(pr_scr)
        pk_scr[...] = jnp.zeros_like(pk_scr)
        pv_scr[...] = jnp.zeros_like(pv_scr)
        psm_scr[...] = jnp.zeros_like(psm_scr)

    def shift(x_ref, p_scr, mu_ref, ci):
        x = x_ref[c * ci:c * (ci + 1), :]
        prev = p_scr[...] if ci == 0 else x_ref[c * ci - 1:c * ci, :]
        row = lax.broadcasted_iota(jnp.int32, x.shape, 0)
        xp = jnp.where(row == 0, prev, pltpu.roll(x, 1, 0))
        return x + (xp - x) * mu_ref[...]

    ri = lax.broadcasted_iota(jnp.int32, (c, LANES), 0)
    li = lax.broadcasted_iota(jnp.int32, (c, LANES), 1)
    lj = li & (half - 1)
    lane_a = li < half
    strict = lj < ri
    incl = lj <= ri
    eye2 = (lj == ri).astype(F32)
    r2 = lax.broadcasted_iota(jnp.int32, (LANES, LANES), 0)
    l2 = lax.broadcasted_iota(jnp.int32, (LANES, LANES), 1)
    bd = (r2 < half) == (l2 < half)
    bd_ones = bd.astype(BF16)
    bd_ones2 = jnp.concatenate([bd_ones, bd_ones], axis=0)
    tr = lax.broadcasted_iota(jnp.int32, (c, 3 * c), 0)
    tc = lax.broadcasted_iota(jnp.int32, (c, 3 * c), 1)
    tril3 = ((tc & (c - 1)) <= tr).astype(BF16)

    def bd_stack(x):
        xb = x.astype(BF16)
        z = jnp.zeros_like(xb)
        return jnp.concatenate([jnp.where(lane_a, xb, z), jnp.where(lane_a, z, xb)], axis=0)

    def segsum(x):
        xs = jnp.concatenate([x[:, LANES * g:LANES * (g + 1)] for g in range(PAIRS)], axis=0)
        hi = xs.astype(BF16)
        lo = (xs - hi.astype(F32)).astype(BF16)
        s = _dot(jnp.concatenate([hi, lo], axis=1), bd_ones2)
        return jnp.concatenate([s[c * g:c * (g + 1), :] for g in range(PAIRS)], axis=1)

    def cumsum_rows(x):
        hi = x.astype(BF16)
        r1 = x - hi.astype(F32)
        mid = r1.astype(BF16)
        lo = (r1 - mid.astype(F32)).astype(BF16)
        return _dot(tril3, jnp.concatenate([hi, mid, lo], axis=0))

    prs = range(PAIRS)
    sls = [slice(LANES * p, LANES * (p + 1)) for p in prs]
    inv_n = 1.0 / RWKV_HEAD
    state = {"s": [st_scr[p] for p in prs]}
    pre, ind, dep = {}, {}, {}

    def prologue(ci):
        rs = shift(r_ref, pr_scr, mur_ref, ci)
        ks = shift(k_ref, pk_scr, muk_ref, ci)
        vs = shift(v_ref, pv_scr, muv_ref, ci)
        sm = shift(sm_ref, psm_scr, musm_ref, ci)
        g0 = sm[:, 0:LANES]
        zw = _dot(jnp.tanh(g0).astype(BF16), w2_ref[...])
        za = _dot(g0.astype(BF16), a2_ref[...])
        kkr = ks * kk_ref[...]
        nrm = segsum(kkr * kkr)
        gate = _dot(_sigmoid(sm[:, LANES:3 * LANES]).astype(BF16), g2_ref[...])
        yield
        ld = (-math.exp(-0.5)) * _sigmoid(w0_ref[...] + zw)
        lg = cumsum_rows(ld)
        a = _sigmoid(a0_ref[...] + za)
        kk = kkr * lax.rsqrt(jnp.maximum(nrm, 1e-24))
        kp = ks * (1.0 + (a - 1.0) * ka_ref[...])
        beta = kk * a
        bonus = segsum(rs * kp * rk_ref[...])
        yield
        lgc = lg[c - 1:c, :]
        einv = jnp.exp(-lg)
        ec = jnp.exp(lgc - lg)
        rt = rs * jnp.exp(lg)
        at = -kk * jnp.exp(lg - ld)
        pre[ci] = dict(vs=vs, gate=gate, bonus=bonus, gc=jnp.exp(lgc), rt=rt, at=at,
                       bt=beta * einv, kt=kp * einv, bh=beta * ec, kh=kp * ec)
        yield

    def independent(ci):
        q = pre[ci]
        v_p = [q["vs"][:, sl] for sl in sls]
        lhs1 = [jnp.concatenate([q["at"][:, sl], q["rt"][:, sl]], axis=0).astype(BF16) for sl in sls]
        rhs1 = [jnp.concatenate([bd_stack(q["bt"][:, sl]), bd_stack(q["kt"][:, sl])], axis=0) for sl in sls]
        pm = [_dot_nt(lhs1[p], rhs1[p]) for p in prs]
        bk = [jnp.concatenate([q["bh"][:, sl], q["kh"][:, sl]], axis=0).astype(BF16) for sl in sls]
        yield
        lab = [jnp.where(strict, m[0:c, 0:LANES], 0.0) for m in pm]
        lm = [jnp.concatenate([jnp.where(strict, m[0:c, LANES:2 * LANES], 0.0),
                               jnp.where(incl, m[c:2 * c, LANES:2 * LANES], 0.0)], axis=0).astype(BF16)
              for m in pm]
        mrb = [jnp.where(incl, m[c:2 * c, 0:LANES], 0.0).astype(BF16) for m in pm]
        lv = [_dot(lm[p], bd_stack(v_p[p])) for p in prs]
        pw = [_dot(l.astype(BF16), bd_stack(l)) for l in lab]
        tinv = [eye2 + l for l in lab]
        yield
        for _ in range(4):
            res = [_dot(pw[p].astype(BF16),
                        jnp.concatenate([bd_stack(tinv[p]), bd_stack(pw[p])], axis=1)) for p in prs]
            tinv = [tinv[p] + res[p][:, 0:LANES] for p in prs]
            pw = [res[p][:, LANES:2 * LANES] for p in prs]
            yield
        tinv = [(tinv[p] + _dot(pw[p].astype(BF16), bd_stack(tinv[p]))).astype(BF16) for p in prs]
        ind[ci] = dict(v_p=v_p, lhs1=lhs1, bk=bk, mrb=mrb, lv=lv, tinv=tinv)
        yield

    def dependent(ci):
        q, st = ind[ci], state["s"]
        xh = [_dot_nt(q["lhs1"][p], st[p].astype(BF16)) for p in prs]
        yield
        x = [xh[p][0:c] + q["lv"][p][0:c] for p in prs]
        u = [_dot(q["tinv"][p], bd_stack(x[p])) for p in prs]
        yield
        ys = [xh[p][c:2 * c] + q["lv"][p][c:2 * c] + _dot(q["mrb"][p], bd_stack(u[p])) for p in prs]
        uv = [jnp.concatenate([u[p], q["v_p"][p]], axis=0).astype(BF16) for p in prs]
        ds = [_dot_tn(uv[p], q["bk"][p]) for p in prs]
        yield
        gc = pre[ci]["gc"]
        state["s"] = [st[p] * gc[:, sls[p]] + jnp.where(bd, ds[p], 0.0) for p in prs]
        dep[ci] = jnp.concatenate(ys, axis=1)
        yield

    def epilogue(ci):
        y, q = dep[ci], pre[ci]
        mean = segsum(y) * inv_n
        yield
        d = y - mean
        var = segsum(d * d) * inv_n
        yield
        yn = d * lax.rsqrt(var + GN_EPS) * gnw_ref[...] + gnb_ref[...]
        o_ref[c * ci:c * (ci + 1), :] = ((yn + q["bonus"] * q["vs"]) * q["gate"]).astype(o_ref.dtype)
        yield

    phases = (prologue, independent, dependent, epilogue)
    done = object()
    for tick in range(nch + len(phases) - 1):
        live = [ph(tick - k) for k, ph in enumerate(phases) if 0 <= tick - k < nch]
        while live:
            live = [g for g in live if next(g, done) is not done]

    for p in prs:
        st_scr[p] = state["s"][p]
    for x_ref, p_scr in ((r_ref, pr_scr), (k_ref, pk_scr), (v_ref, pv_scr), (sm_ref, psm_scr)):
        p_scr[...] = x_ref[rows - 1:rows, :]


def _rwkv(proj, bsz, seq, mu_r, mu_k, mu_v, mu_sm, w0, a0, k_k, k_a, r_k, gn_w, gn_b, w2p, a2p, g2p,
          to_bf16=(), nch=8):
    t = proj.shape[0]
    w = RWKV_WIDTH
    rows = CHUNK * nch
    nc = seq // rows
    steps = bsz * nc
    row = lambda b, c: b * nc + c
    vec = lambda n: pl.BlockSpec((1, n), lambda b, c: (0, 0))
    full = lambda a: pl.BlockSpec(a.shape, lambda b, c: (0, 0))
    assert all(a.shape[0] % (steps * BF16_ROWS) == 0 for a in to_bf16), "row slabs must be bf16-tile aligned"
    slab = lambda a: pl.BlockSpec((a.shape[0] // steps, a.shape[1]), lambda b, c: (row(b, c), 0))
    return pl.pallas_call(
        functools.partial(_rwkv_kernel, n_cast=len(to_bf16)),
        grid=(bsz, nc),
        in_specs=[
            pl.BlockSpec((rows, w), lambda b, c: (row(b, c), 0)),
            pl.BlockSpec((rows, w), lambda b, c: (row(b, c), 1)),
            pl.BlockSpec((rows, w), lambda b, c: (row(b, c), 2)),
            pl.BlockSpec((rows, SMALL), lambda b, c: (row(b, c), COL_SMALL)),
            vec(w), vec(w), vec(w), vec(SMALL),
            vec(w), vec(w), vec(w), vec(w), vec(w), vec(w), vec(w),
            full(w2p), full(a2p), full(g2p),
        ] + [slab(a) for a in to_bf16],
        out_specs=[pl.BlockSpec((rows, w), lambda b, c: (row(b, c), 0))] + [slab(a) for a in to_bf16],
        out_shape=[jax.ShapeDtypeStruct((t, w), BF16)]
        + [jax.ShapeDtypeStruct(a.shape, BF16) for a in to_bf16],
        scratch_shapes=[
            pltpu.VMEM((PAIRS, LANES, LANES), F32),
            pltpu.VMEM((1, w), F32), pltpu.VMEM((1, w), F32), pltpu.VMEM((1, w), F32),
            pltpu.VMEM((1, SMALL), F32),
        ],
        compiler_params=_cparams(("parallel", "arbitrary")),
        name="rwkv",
    )(proj, proj, proj, proj, mu_r, mu_k, mu_v, mu_sm, w0, a0, k_k, k_a, r_k, gn_w, gn_b,
      w2p, a2p, g2p, *to_bf16)


def _rope_tables(pos_ref, invf_ref):
    ang = pos_ref[...].astype(F32) * invf_ref[...]
    keep = lax.broadcasted_iota(jnp.int32, ang.shape, 1) < QK_ROPE
    return jnp.where(keep, jnp.cos(ang), 0.0), jnp.where(keep, jnp.sin(ang), 0.0)


def _mla_proj_kernel(cq_ref, ckv_ref, sm_ref, pos_ref, invf_ref, gq_ref, gkv_ref,
                     w1_ref, w2_ref, wkv_ref, q_ref, k_ref, v_ref):
    cosm, sinm = _rope_tables(pos_ref, invf_ref)
    hw = QK_NOPE + LANES

    h = _rms(cq_ref[...], gq_ref[...]).astype(BF16)
    z1 = _dot(h, w1_ref[...])
    z2 = _dot(h, w2_ref[...])
    cosq, sinq = cosm * SCORE_SCALE, sinm * SCORE_SCALE
    for hd in range(MLA_HEADS):
        q_ref[:, hw * hd:hw * hd + QK_NOPE] = (z1[:, hw * hd:hw * hd + QK_NOPE] * SCORE_SCALE).astype(BF16)
        pe = z1[:, hw * hd + QK_NOPE:hw * (hd + 1)] * cosq + z2[:, LANES * hd:LANES * (hd + 1)] * sinq
        q_ref[:, hw * hd + QK_NOPE:hw * (hd + 1)] = pe.astype(BF16)

    h = _rms(ckv_ref[...], gkv_ref[...]).astype(BF16)
    z = _dot(h, wkv_ref[...])
    kpe = (sm_ref[:, 2 * LANES:3 * LANES] * cosm + sm_ref[:, 3 * LANES:4 * LANES] * sinm).astype(BF16)
    for hd in range(MLA_HEADS):
        k_ref[:, hw * hd:hw * hd + QK_NOPE] = z[:, QK_NOPE * hd:QK_NOPE * (hd + 1)].astype(BF16)
        k_ref[:, hw * hd + QK_NOPE:hw * (hd + 1)] = kpe
    ones = jnp.ones((V_EXT - V_HEAD, z.shape[0]), BF16)
    for hd in range(MLA_HEADS):
        vcol = MLA_HEADS * QK_NOPE + V_HEAD * hd
        v_ref[V_EXT * hd:V_EXT * hd + V_HEAD, :] = z[:, vcol:vcol + V_HEAD].T.astype(BF16)
        v_ref[V_EXT * hd + V_HEAD:V_EXT * (hd + 1), :] = ones


def _mla_proj(proj, pos, invf, gq, gkv, w1, w2, wkv, tm=512):
    t = proj.shape[0]
    nk = MLA_HEADS * (QK_NOPE + LANES)
    nv = MLA_HEADS * V_EXT
    const = lambda a: pl.BlockSpec(a.shape, lambda i: (0, 0))
    return pl.pallas_call(
        _mla_proj_kernel,
        grid=(t // tm,),
        in_specs=[
            pl.BlockSpec((tm, Q_LORA), lambda i: (i, COL_CQ)),
            pl.BlockSpec((tm, KV_LORA), lambda i: (i, COL_CKV)),
            pl.BlockSpec((tm, SMALL), lambda i: (i, COL_SMALL)),
            pl.BlockSpec((tm, 1), lambda i: (i, 0)),
            const(invf), const(gq), const(gkv), const(w1), const(w2), const(wkv),
        ],
        out_specs=[
            pl.BlockSpec((tm, nk), lambda i: (i, 0)),
            pl.BlockSpec((tm, nk), lambda i: (i, 0)),
            pl.BlockSpec((nv, tm), lambda i: (0, i)),
        ],
        out_shape=[jax.ShapeDtypeStruct((t, nk), BF16), jax.ShapeDtypeStruct((t, nk), BF16),
                   jax.ShapeDtypeStruct((nv, t), BF16)],
        compiler_params=_cparams(("parallel",)),
        name="mla_proj",
    )(proj, proj, proj, pos, invf, gq, gkv, w1, w2, wkv)


ATTN_STRIP = 64


def _attn_kernel(q_ref, k_ref, vt_ref, o_ref, *, blk, nh):
    qi = pl.program_id(2)
    hw = QK_NOPE + LANES
    hds = range(nh)

    def weights(s, m_new):
        return jnp.concatenate(
            [jnp.exp2(s[ATTN_STRIP * r:ATTN_STRIP * (r + 1)] - m_new).astype(BF16)
             for r in range(s.shape[0] // ATTN_STRIP)], axis=0)

    def pipeline(scores, softmax, values):
        m_out, acc_out = [], []
        s_next = scores(0)
        prev = None
        for h in hds:
            s_cur = s_next
            if h + 1 < nh:
                s_next = scores(h + 1)
            m_new, rest = softmax(h, s_cur)
            m_out.append(m_new)
            if prev is not None:
                acc_out.append(values(*prev))
            prev = (h, rest)
        acc_out.append(values(*prev))
        return m_out, acc_out

    def full_step(j, carry):
        m, acc = carry
        start = pl.multiple_of(j * blk, blk)

        def scores(h):
            return _dot_nt(k_ref[pl.ds(start, blk), hw * h:hw * (h + 1)], q_ref[:, hw * h:hw * (h + 1)])

        def softmax(h, s):
            m_new = jnp.maximum(m[h], jnp.max(s, axis=0, keepdims=True))
            return m_new, (jnp.exp2(m[h] - m_new), weights(s, m_new))

        def values(h, rest):
            alpha, p = rest
            return alpha * acc[h] + _dot(vt_ref[V_EXT * h:V_EXT * (h + 1), pl.ds(start, blk)], p)

        return pipeline(scores, softmax, values)

    def diag_step(carry):
        m, acc = carry
        hb = blk // 2
        lo = pl.multiple_of(qi * blk, blk)
        hi = pl.multiple_of(qi * blk + hb, hb)
        key = lax.broadcasted_iota(jnp.int32, (hb, blk), 0)
        qry = lax.broadcasted_iota(jnp.int32, (hb, blk), 1)
        causal_a = key <= qry
        causal_b = causal_a[:, :hb]

        def scores(h):
            hc = slice(hw * h, hw * (h + 1))
            return (_dot_nt(k_ref[pl.ds(lo, hb), hc], q_ref[:, hc]),
                    _dot_nt(k_ref[pl.ds(hi, hb), hc], q_ref[hb:, hc]))

        def softmax(h, s):
            sa = jnp.where(causal_a, s[0], NEG_INF)
            sb = jnp.where(causal_b, s[1], NEG_INF)
            mx = jnp.max(sa, axis=0, keepdims=True)
            mx = jnp.concatenate([mx[:, :hb], jnp.maximum(mx[:, hb:], jnp.max(sb, axis=0, keepdims=True))],
                                 axis=1)
            m_new = jnp.maximum(m[h], mx)
            return m_new, (jnp.exp2(m[h] - m_new), weights(sa, m_new), weights(sb, m_new[:, hb:]))

        def values(h, rest):
            alpha, pa, pb = rest
            vh = slice(V_EXT * h, V_EXT * (h + 1))
            a = alpha * acc[h] + _dot(vt_ref[vh, pl.ds(lo, hb)], pa)
            b = _dot(vt_ref[vh, pl.ds(hi, hb)], pb)
            return jnp.concatenate([a[:, :hb], a[:, hb:] + b], axis=1)

        return pipeline(scores, softmax, values)

    init = ([jnp.full((1, blk), NEG_INF, F32) for _ in hds], [jnp.zeros((V_EXT, blk), F32) for _ in hds])
    carry = lax.fori_loop(0, qi, full_step, init)
    _, acc = diag_step(carry)
    for h in hds:
        out = acc[h][:V_HEAD] / acc[h][V_HEAD:V_HEAD + 1]
        o_ref[:, V_HEAD * h:V_HEAD * (h + 1)] = out.T.astype(o_ref.dtype)


def _attn(q, k, v, bsz, seq, blk=512, nh=4):
    t = q.shape[0]
    nq = seq // blk
    hw = QK_NOPE + LANES
    kern = functools.partial(_attn_kernel, blk=blk, nh=nh)
    return pl.pallas_call(
        kern,
        grid=(bsz, MLA_HEADS // nh, nq),
        in_specs=[
            pl.BlockSpec((blk, hw * nh), lambda b, h, i: (b * nq + i, h)),
            pl.BlockSpec((seq, hw * nh), lambda b, h, i: (b, h)),
            pl.BlockSpec((V_EXT * nh, seq), lambda b, h, i: (h, b)),
        ],
        out_specs=pl.BlockSpec((blk, V_HEAD * nh), lambda b, h, i: (b * nq + i, h)),
        out_shape=jax.ShapeDtypeStruct((t, MLA_HEADS * V_HEAD), BF16),
        compiler_params=_cparams(("parallel", "parallel", "arbitrary")),
        name="attn",
    )(q, k, v)


def _outproj_kernel(yr_ref, ym_ref, w_ref, x_ref, g_ref, x1_ref, h2_ref):
    nr = yr_ref.shape[1]
    x1 = x_ref[...] + _dot(yr_ref[...], w_ref[:nr, :]) + _dot(ym_ref[...], w_ref[nr:, :])
    x1_ref[...] = x1
    h2_ref[...] = _rms(x1, g_ref[...]).astype(BF16)


def _outproj(yr, ym, w, x2, g, tm=512):
    t, d = x2.shape
    return pl.pallas_call(
        _outproj_kernel,
        grid=(t // tm,),
        in_specs=[
            pl.BlockSpec((tm, RWKV_WIDTH), lambda i: (i, 0)),
            pl.BlockSpec((tm, MLA_WIDTH), lambda i: (i, 0)),
            pl.BlockSpec(w.shape, lambda i: (0, 0), pipeline_mode=pl.Buffered(1)),
            pl.BlockSpec((tm, d), lambda i: (i, 0)),
            pl.BlockSpec((1, d), lambda i: (0, 0)),
        ],
        out_specs=[pl.BlockSpec((tm, d), lambda i: (i, 0)), pl.BlockSpec((tm, d), lambda i: (i, 0))],
        out_shape=[jax.ShapeDtypeStruct((t, d), F32), jax.ShapeDtypeStruct((t, d), BF16)],
        compiler_params=_cparams(("parallel",)),
        name="outproj",
    )(yr, ym, w, x2, g)


HALO = BF16_ROWS


MXU_N = 256


def _shift_down(x, first):
    r = pltpu.roll(x, 1, 0)
    top = r[0:SUBLANES]
    row = lax.broadcasted_iota(jnp.int32, top.shape, 0)
    return jnp.concatenate([jnp.where(row == 0, first, top), r[SUBLANES:]], axis=0)


def _ffn_act_kernel(h_ref, halo_ref, wg_ref, wu_ref, cw_ref, cb_ref, a_ref, hh_scr, *, tm, seq):
    i = pl.program_id(1)
    keep = jnp.where((i * tm) % seq != 0, 1.0, 0.0)
    hh_scr[0:HALO, :] = (halo_ref[...] * keep.astype(BF16)).astype(BF16)
    hh_scr[HALO:, :] = h_ref[...]
    n_sub = wg_ref.shape[1] // MXU_N

    def gate_dot(s):
        return _dot(hh_scr[...], wg_ref[:, MXU_N * s:MXU_N * (s + 1)])

    def up_dot(s):
        return _dot(h_ref[...], wu_ref[:, MXU_N * s:MXU_N * (s + 1)])

    ge, u = gate_dot(0), up_dot(0)
    for s in range(n_sub):
        cs = slice(MXU_N * s, MXU_N * (s + 1))
        more = s + 1 < n_sub
        ge_next = gate_dot(s + 1) if more else None
        c0, c1, c2 = cw_ref[0:1, cs], cw_ref[1:2, cs], cw_ref[2:3, cs]
        g = ge[HALO:]
        gm1 = ge[HALO - 1:HALO]
        gm2 = ge[HALO - 2:HALO - 1]
        w = c1 * g + _shift_down(c0 * g, c0 * gm1)
        conv = c2 * g + _shift_down(w, c1 * gm1 + c0 * gm2) + cb_ref[:, cs]
        u_next = up_dot(s + 1) if more else None
        a_ref[:, cs] = (conv * _sigmoid(conv) * u).astype(BF16)
        ge, u = ge_next, u_next


def _ffn_act(h2, wg, wu, cw, cb, seq, tm=512, n_col=2):
    t, d = h2.shape
    f = wg.shape[1]
    tf = f // n_col
    kern = functools.partial(_ffn_act_kernel, tm=tm, seq=seq)
    hb = tm // HALO
    once = pl.Buffered(1)
    return pl.pallas_call(
        kern,
        grid=(n_col, t // tm),
        in_specs=[
            pl.BlockSpec((tm, d), lambda j, i: (i, 0)),
            pl.BlockSpec((HALO, d), lambda j, i: (jnp.maximum(i * hb - 1, 0), 0)),
            pl.BlockSpec((d, tf), lambda j, i: (0, j), pipeline_mode=once),
            pl.BlockSpec((d, tf), lambda j, i: (0, j), pipeline_mode=once),
            pl.BlockSpec((3, tf), lambda j, i: (0, j)),
            pl.BlockSpec((1, tf), lambda j, i: (0, j)),
        ],
        out_specs=pl.BlockSpec((tm, tf), lambda j, i: (i, j)),
        out_shape=jax.ShapeDtypeStruct((t, f), BF16),
        scratch_shapes=[pltpu.VMEM((tm + HALO, d), BF16)],
        compiler_params=_cparams(("parallel", "parallel")),
        name="ffn_act",
    )(h2, h2, wg, wu, cw, cb)


def _ffn_down_kernel(a_ref, wd_ref, x1_ref, gf_ref, o_ref):
    o_ref[...] = _rms(x1_ref[...] + _dot(a_ref[...], wd_ref[...]), gf_ref[...])


def _ffn_down(act, wd, x1, gf, tm=512):
    t, d = x1.shape
    f = act.shape[1]
    return pl.pallas_call(
        _ffn_down_kernel,
        grid=(t // tm,),
        in_specs=[
            pl.BlockSpec((tm, f), lambda i: (i, 0)),
            pl.BlockSpec((f, d), lambda i: (0, 0), pipeline_mode=pl.Buffered(1)),
            pl.BlockSpec((tm, d), lambda i: (i, 0)),
            pl.BlockSpec((1, d), lambda i: (0, 0)),
        ],
        out_specs=pl.BlockSpec((tm, d), lambda i: (i, 0)),
        out_shape=jax.ShapeDtypeStruct((t, d), F32),
        compiler_params=_cparams(("parallel",)),
        name="ffn_down",
    )(act, wd, x1, gf)


def _pack_in(w_in, mu):
    d = w_in.shape[0]
    o = 3 * RWKV_WIDTH
    og = o + DECAY_LORA + AAA_LORA
    wt = w_in[:, o:].T
    lo = DECAY_LORA + AAA_LORA
    wl_al = wt[:lo]
    gl = wt[lo:lo + GATE_LORA]
    oq = lo + GATE_LORA
    cq = wt[oq:oq + Q_LORA]
    ckv = wt[oq + Q_LORA:oq + Q_LORA + KV_LORA]
    kpe = wt[oq + Q_LORA + KV_LORA:]
    hr = QK_ROPE // 2
    kpe_rot = jnp.concatenate([-kpe[hr:], kpe[:hr]], axis=0)
    z = lambda n: jnp.zeros((n, d), w_in.dtype)
    rest = GATE_LORA - LANES
    w_a = w_in[:, :o].astype(BF16)
    w_bt = jnp.concatenate(
        [cq, ckv, wl_al, gl[:LANES], kpe, gl[LANES:], z(LANES - QK_ROPE - rest), kpe_rot, z(LANES - QK_ROPE)],
        axis=0).astype(BF16)
    zm = lambda n: jnp.zeros((n,), mu.dtype)
    mu_sm = jnp.concatenate(
        [mu[o:og], mu[og:og + LANES], zm(QK_ROPE), mu[og + LANES:og + GATE_LORA],
         zm(LANES - QK_ROPE - rest), zm(LANES)])
    return w_a, w_bt, mu[:RWKV_WIDTH], mu[RWKV_WIDTH:2 * RWKV_WIDTH], mu[2 * RWKV_WIDTH:o], mu_sm


def _pack_lora(w2, a2, g2):
    n = w2.shape[1]
    z = lambda r: jnp.zeros((r, n), w2.dtype)
    rest = GATE_LORA - LANES
    w2p = jnp.concatenate([w2, z(LANES - DECAY_LORA)], axis=0)
    a2p = jnp.concatenate([z(DECAY_LORA), a2], axis=0)
    g2p = jnp.concatenate([g2[:LANES], z(QK_ROPE), g2[LANES:], z(LANES - QK_ROPE - rest)], axis=0)
    return w2p.astype(BF16), a2p.astype(BF16), g2p.astype(BF16)


def _pack_q(w_uq):
    d = w_uq.shape[0]
    hw = QK_NOPE + QK_ROPE
    hr = QK_ROPE // 2
    w = w_uq.reshape(d, MLA_HEADS, hw)
    pe = w[:, :, QK_NOPE:]
    z = jnp.zeros((d, MLA_HEADS, LANES - QK_ROPE), w_uq.dtype)
    w1 = jnp.concatenate([w[:, :, :QK_NOPE], pe, z], axis=2).reshape(d, -1)
    w2 = jnp.concatenate([-pe[:, :, hr:], pe[:, :, :hr], z], axis=2).reshape(d, -1)
    return w1.astype(BF16), w2.astype(BF16)


def _pack_kv(w_ukv):
    d = w_ukv.shape[0]
    w = w_ukv.reshape(d, MLA_HEADS, QK_NOPE + V_HEAD)
    return jnp.concatenate([w[:, :, :QK_NOPE].reshape(d, -1), w[:, :, QK_NOPE:].reshape(d, -1)],
                           axis=1).astype(BF16)


def kernel(x, positions, attn_norm_g, w_in, rwkv_mu, rwkv_w0, rwkv_w2, rwkv_a0, rwkv_a2, rwkv_g2,
           rwkv_k_k, rwkv_k_a, rwkv_r_k, rwkv_gn_w, rwkv_gn_b, mla_q_norm_g, mla_w_uq,
           mla_kv_norm_g, mla_w_ukv, w_out, ffn_norm_g, ffn_w_gate, ffn_w_up, ffn_conv_w,
           ffn_conv_b, ffn_w_down, final_norm_g):
    bsz, seq, d = x.shape
    t = bsz * seq
    depth = w_in.shape[0]
    row = lambda a: a.reshape(1, -1)
    xc = x.reshape(t, d)
    pos = positions.reshape(t, 1)
    hr = QK_ROPE // 2
    inv_freq = ROPE_THETA ** (-jnp.arange(hr, dtype=F32) / hr)
    invf = jnp.concatenate([inv_freq, inv_freq, jnp.zeros((LANES - QK_ROPE,), F32)]).reshape(1, LANES)

    assert depth == 1, "the final RMSNorm is fused into the (single) layer's ffn kernel"
    l = 0
    w_a, w_b, mu_r, mu_k, mu_v, mu_sm = _pack_in(w_in[l], rwkv_mu[l])
    w2p, a2p, g2p = _pack_lora(rwkv_w2[l], rwkv_a2[l], rwkv_g2[l])
    wq1, wq2 = _pack_q(mla_w_uq[l])
    wkv = _pack_kv(mla_w_ukv[l])

    proj = _inproj(xc, row(attn_norm_g[l]), w_a, w_b)
    y_r, wo, wg, wu, wd = _rwkv(
        proj, bsz, seq, row(mu_r), row(mu_k), row(mu_v), row(mu_sm),
        row(rwkv_w0[l]), row(rwkv_a0[l]), row(rwkv_k_k[l]), row(rwkv_k_a[l]),
        row(rwkv_r_k[l]), row(rwkv_gn_w[l]), row(rwkv_gn_b[l]), w2p, a2p, g2p,
        to_bf16=(w_out[l], ffn_w_gate[l], ffn_w_up[l], ffn_w_down[l]))
    q, k, v = _mla_proj(proj, pos, invf, row(mla_q_norm_g[l]), row(mla_kv_norm_g[l]), wq1, wq2, wkv)
    y_m = _attn(q, k, v, bsz, seq)
    x1, h2 = _outproj(y_r, y_m, wo, xc, row(ffn_norm_g[l]))
    act = _ffn_act(h2, wg, wu, ffn_conv_w[l], row(ffn_conv_b[l]), seq)
    out = _ffn_down(act, wd, x1, row(final_norm_g))
    return out.reshape(bsz, seq, d)
```

```python
import functools
import math

import jax
import jax.numpy as jnp
from jax import lax
from jax.experimental import pallas as pl
from jax.experimental.pallas import tpu as pltpu

F32 = jnp.float32
BF16 = jnp.bfloat16

D_MODEL = 2048
RWKV_HEAD = 64
RWKV_WIDTH = 1024
RWKV_HEADS = 16
DECAY_LORA = 64
AAA_LORA = 64
GATE_LORA = 160
GN_EPS = 64e-5
QK_NOPE = 128
QK_ROPE = 64
V_HEAD = 128
MLA_WIDTH = 1024
MLA_HEADS = 8
Q_LORA = 512
KV_LORA = 512
ROPE_THETA = 10000.0
D_FF = 5632
NORM_EPS = 1e-6
NEG_INF = -1e30

LANES = 128
SUBLANES = 8
BF16_ROWS = 16
SMALL = 512
D_IN_P = 3 * RWKV_WIDTH + Q_LORA + KV_LORA + SMALL
COL_CQ = 3 * RWKV_WIDTH // Q_LORA
COL_CKV = COL_CQ + 1
COL_SMALL = COL_CKV + 1
CHUNK = 64
PAIRS = RWKV_WIDTH // LANES
VMEM_LIMIT = 56 * 1024 * 1024
V_EXT = V_HEAD + BF16_ROWS
SCORE_SCALE = (QK_NOPE + QK_ROPE) ** -0.5 * math.log2(math.e)


def _cparams(sem):
    return pltpu.CompilerParams(dimension_semantics=sem, vmem_limit_bytes=VMEM_LIMIT)


def _rms(x, g):
    ms = jnp.mean(x * x, axis=-1, keepdims=True)
    return x * lax.rsqrt(ms + NORM_EPS) * g


def _sigmoid(x):
    return 1.0 / (1.0 + jnp.exp(-x))


def _dot(a, b):
    return jnp.dot(a, b, preferred_element_type=F32)


def _dot_nt(a, b):
    return lax.dot_general(a, b, (((1,), (1,)), ((), ())), preferred_element_type=F32)


def _dot_tn(a, b):
    return lax.dot_general(a, b, (((0,), (0,)), ((), ())), preferred_element_type=F32)


def _inproj_kernel(x_ref, g_ref, wa_ref, wbt_ref, o_ref):
    h = _rms(x_ref[...], g_ref[...]).astype(BF16)
    na = wa_ref.shape[1]
    o_ref[:, :na] = _dot(h, wa_ref[...])
    o_ref[:, na:] = _dot_nt(h, wbt_ref[...])


def _inproj(x2, g, w_a, w_b, tm=512):
    t, d = x2.shape
    n = w_a.shape[1] + w_b.shape[0]
    once = pl.Buffered(1)
    return pl.pallas_call(
        _inproj_kernel,
        grid=(t // tm,),
        in_specs=[
            pl.BlockSpec((tm, d), lambda i: (i, 0)),
            pl.BlockSpec((1, d), lambda i: (0, 0)),
            pl.BlockSpec(w_a.shape, lambda i: (0, 0), pipeline_mode=once),
            pl.BlockSpec(w_b.shape, lambda i: (0, 0), pipeline_mode=once),
        ],
        out_specs=pl.BlockSpec((tm, n), lambda i: (i, 0)),
        out_shape=jax.ShapeDtypeStruct((t, n), F32),
        compiler_params=_cparams(("parallel",)),
        name="inproj",
    )(x2, g, w_a, w_b)


RWKV_INS = 18


def _rwkv_kernel(*refs, n_cast):
    (r_ref, k_ref, v_ref, sm_ref, mur_ref, muk_ref, muv_ref, musm_ref,
     w0_ref, a0_ref, kk_ref, ka_ref, rk_ref, gnw_ref, gnb_ref,
     w2_ref, a2_ref, g2_ref) = refs[:RWKV_INS]
    cast_in = refs[RWKV_INS:RWKV_INS + n_cast]
    o_ref = refs[RWKV_INS + n_cast]
    cast_out = refs[RWKV_INS + n_cast + 1:RWKV_INS + 2 * n_cast + 1]
    st_scr, pr_scr, pk_scr, pv_scr, psm_scr = refs[RWKV_INS + 2 * n_cast + 1:]
    for src, dst in zip(cast_in, cast_out):
        dst[...] = src[...].astype(dst.dtype)
    c = CHUNK
    half = RWKV_HEAD
    rows = r_ref.shape[0]
    nch = rows // c

    @pl.when(pl.program_id(1) == 0)
    def _():
        st_scr[...] = jnp.zeros_like(st_scr)
        pr_scr[...] = jnp.zeros_like(pr_scr)
        pk_scr[...] = jnp.zeros_like(pk_scr)
        pv_scr[...] = jnp.zeros_like(pv_scr)
        psm_scr[...] = jnp.zeros_like(psm_scr)

    def shift(x_ref, p_scr, mu_ref, ci):
        x = x_ref[c * ci:c * (ci + 1), :]
        prev = p_scr[...] if ci == 0 else x_ref[c * ci - 1:c * ci, :]
        row = lax.broadcasted_iota(jnp.int32, x.shape, 0)
        xp = jnp.where(row == 0, prev, pltpu.roll(x, 1, 0))
        return x + (xp - x) * mu_ref[...]

    ri = lax.broadcasted_iota(jnp.int32, (c, LANES), 0)
    li = lax.broadcasted_iota(jnp.int32, (c, LANES), 1)
    lj = li & (half - 1)
    lane_a = li < half
    strict = lj < ri
    incl = lj <= ri
    eye2 = (lj == ri).astype(F32)
    r2 = lax.broadcasted_iota(jnp.int32, (LANES, LANES), 0)
    l2 = lax.broadcasted_iota(jnp.int32, (LANES, LANES), 1)
    bd = (r2 < half) == (l2 < half)
    bd_ones = bd.astype(BF16)
    bd_ones2 = jnp.concatenate([bd_ones, bd_ones], axis=0)
    tr = lax.broadcasted_iota(jnp.int32, (c, 3 * c), 0)
    tc = lax.broadcasted_iota(jnp.int32, (c, 3 * c), 1)
    tril3 = ((tc & (c - 1)) <= tr).astype(BF16)

    def bd_stack(x):
        xb = x.astype(BF16)
        z = jnp.zeros_like(xb)
        return jnp.concatenate([jnp.where(lane_a, xb, z), jnp.where(lane_a, z, xb)], axis=0)

    def segsum(x):
        xs = jnp.concatenate([x[:, LANES * g:LANES * (g + 1)] for g in range(PAIRS)], axis=0)
        hi = xs.astype(BF16)
        lo = (xs - hi.astype(F32)).astype(BF16)
        s = _dot(jnp.concatenate([hi, lo], axis=1), bd_ones2)
        return jnp.concatenate([s[c * g:c * (g + 1), :] for g in range(PAIRS)], axis=1)

    def cumsum_rows(x):
        hi = x.astype(BF16)
        r1 = x - hi.astype(F32)
        mid = r1.astype(BF16)
        lo = (r1 - mid.astype(F32)).astype(BF16)
        return _dot(tril3, jnp.concatenate([hi, mid, lo], axis=0))

    prs = range(PAIRS)
    sls = [slice(LANES * p, LANES * (p + 1)) for p in prs]
    inv_n = 1.0 / RWKV_HEAD
    state = {"s": [st_scr[p] for p in prs]}
    pre, ind, dep = {}, {}, {}

    def prologue(ci):
        rs = shift(r_ref, pr_scr, mur_ref, ci)
        ks = shift(k_ref, pk_scr, muk_ref, ci)
        vs = shift(v_ref, pv_scr, muv_ref, ci)
        sm = shift(sm_ref, psm_scr, musm_ref, ci)
        g0 = sm[:, 0:LANES]
        zw = _dot(jnp.tanh(g0).astype(BF16), w2_ref[...])
        za = _dot(g0.astype(BF16), a2_ref[...])
        kkr = ks * kk_ref[...]
        nrm = segsum(kkr * kkr)
        gate = _dot(_sigmoid(sm[:, LANES:3 * LANES]).astype(BF16), g2_ref[...])
        yield
        ld = (-math.exp(-0.5)) * _sigmoid(w0_ref[...] + zw)
        lg = cumsum_rows(ld)
        a = _sigmoid(a0_ref[...] + za)
        kk = kkr * lax.rsqrt(jnp.maximum(nrm, 1e-24))
        kp = ks * (1.0 + (a - 1.0) * ka_ref[...])
        beta = kk * a
        bonus = segsum(rs * kp * rk_ref[...])
        yield
        lgc = lg[c - 1:c, :]
        einv = jnp.exp(-lg)
        ec = jnp.exp(lgc - lg)
        rt = rs * jnp.exp(lg)
        at = -kk * jnp.exp(lg - ld)
        pre[ci] = dict(vs=vs, gate=gate, bonus=bonus, gc=jnp.exp(lgc), rt=rt, at=at,
                       bt=beta * einv, kt=kp * einv, bh=beta * ec, kh=kp * ec)
        yield

    def independent(ci):
        q = pre[ci]
        v_p = [q["vs"][:, sl] for sl in sls]
        lhs1 = [jnp.concatenate([q["at"][:, sl], q["rt"][:, sl]], axis=0).astype(BF16) for sl in sls]
        rhs1 = [jnp.concatenate([bd_stack(q["bt"][:, sl]), bd_stack(q["kt"][:, sl])], axis=0) for sl in sls]
        pm = [_dot_nt(lhs1[p], rhs1[p]) for p in prs]
        bk = [jnp.concatenate([q["bh"][:, sl], q["kh"][:, sl]], axis=0).astype(BF16) for sl in sls]
        yield
        lab = [jnp.where(strict, m[0:c, 0:LANES], 0.0) for m in pm]
        lm = [jnp.concatenate([jnp.where(strict, m[0:c, LANES:2 * LANES], 0.0),
                               jnp.where(incl, m[c:2 * c, LANES:2 * LANES], 0.0)], axis=0).astype(BF16)
              for m in pm]
        mrb = [jnp.where(incl, m[c:2 * c, 0:LANES], 0.0).astype(BF16) for m in pm]
        lv = [_dot(lm[p], bd_stack(v_p[p])) for p in prs]
        pw = [_dot(l.astype(BF16), bd_stack(l)) for l in lab]
        tinv = [eye2 + l for l in lab]
        yield
        for _ in range(4):
            res = [_dot(pw[p].astype(BF16),
                        jnp.concatenate([bd_stack(tinv[p]), bd_stack(pw[p])], axis=1)) for p in prs]
            tinv = [tinv[p] + res[p][:, 0:LANES] for p in prs]
            pw = [res[p][:, LANES:2 * LANES] for p in prs]
            yield
        tinv = [(tinv[p] + _dot(pw[p].astype(BF16), bd_stack(tinv[p]))).astype(BF16) for p in prs]
        ind[ci] = dict(v_p=v_p, lhs1=lhs1, bk=bk, mrb=mrb, lv=lv, tinv=tinv)
        yield

    def dependent(ci):
        q, st = ind[ci], state["s"]
        xh = [_dot_nt(q["lhs1"][p], st[p].astype(BF16)) for p in prs]
        yield
        x = [xh[p][0:c] + q["lv"][p][0:c] for p in prs]
        u = [_dot(q["tinv"][p], bd_stack(x[p])) for p in prs]
        yield
        ys = [xh[p][c:2 * c] + q["lv"][p][c:2 * c] + _dot(q["mrb"][p], bd_stack(u[p])) for p in prs]
        uv = [jnp.concatenate([u[p], q["v_p"][p]], axis=0).astype(BF16) for p in prs]
        ds = [_dot_tn(uv[p], q["bk"][p]) for p in prs]
        yield
        gc = pre[ci]["gc"]
        state["s"] = [st[p] * gc[:, sls[p]] + jnp.where(bd, ds[p], 0.0) for p in prs]
        dep[ci] = jnp.concatenate(ys, axis=1)
        yield

    def epilogue(ci):
        y, q = dep[ci], pre[ci]
        mean = segsum(y) * inv_n
        yield
        d = y - mean
        var = segsum(d * d) * inv_n
        yield
        yn = d * lax.rsqrt(var + GN_EPS) * gnw_ref[...] + gnb_ref[...]
        o_ref[c * ci:c * (ci + 1), :] = ((yn + q["bonus"] * q["vs"]) * q["gate"]).astype(o_ref.dtype)
        yield

    phases = (prologue, independent, dependent, epilogue)
    done = object()
    for tick in range(nch + len(phases) - 1):
        live = [ph(tick - k) for k, ph in enumerate(phases) if 0 <= tick - k < nch]
        while live:
            live = [g for g in live if next(g, done) is not done]

    for p in prs:
        st_scr[p] = state["s"][p]
    for x_ref, p_scr in ((r_ref, pr_scr), (k_ref, pk_scr), (v_ref, pv_scr), (sm_ref, psm_scr)):
        p_scr[...] = x_ref[rows - 1:rows, :]


def _rwkv(proj, bsz, seq, mu_r, mu_k, mu_v, mu_sm, w0, a0, k_k, k_a, r_k, gn_w, gn_b, w2p, a2p, g2p,
          to_bf16=(), nch=8):
    t = proj.shape[0]
    w = RWKV_WIDTH
    rows = CHUNK * nch
    nc = seq // rows
    steps = bsz * nc
    row = lambda b, c: b * nc + c
    vec = lambda n: pl.BlockSpec((1, n), lambda b, c: (0, 0))
    full = lambda a: pl.BlockSpec(a.shape, lambda b, c: (0, 0))
    assert all(a.shape[0] % (steps * BF16_ROWS) == 0 for a in to_bf16), "row slabs must be bf16-tile aligned"
    slab = lambda a: pl.BlockSpec((a.shape[0] // steps, a.shape[1]), lambda b, c: (row(b, c), 0))
    return pl.pallas_call(
        functools.partial(_rwkv_kernel, n_cast=len(to_bf16)),
        grid=(bsz, nc),
        in_specs=[
            pl.BlockSpec((rows, w), lambda b, c: (row(b, c), 0)),
            pl.BlockSpec((rows, w), lambda b, c: (row(b, c), 1)),
            pl.BlockSpec((rows, w), lambda b, c: (row(b, c), 2)),
            pl.BlockSpec((rows, SMALL), lambda b, c: (row(b, c), COL_SMALL)),
            vec(w), vec(w), vec(w), vec(SMALL),
            vec(w), vec(w), vec(w), vec(w), vec(w), vec(w), vec(w),
            full(w2p), full(a2p), full(g2p),
        ] + [slab(a) for a in to_bf16],
        out_specs=[pl.BlockSpec((rows, w), lambda b, c: (row(b, c), 0))] + [slab(a) for a in to_bf16],
        out_shape=[jax.ShapeDtypeStruct((t, w), BF16)]
        + [jax.ShapeDtypeStruct(a.shape, BF16) for a in to_bf16],
        scratch_shapes=[
            pltpu.VMEM((PAIRS, LANES, LANES), F32),
            pltpu.VMEM((1, w), F32), pltpu.VMEM((1, w), F32), pltpu.VMEM((1, w), F32),
            pltpu.VMEM((1, SMALL), F32),
        ],
        compiler_params=_cparams(("parallel", "arbitrary")),
        name="rwkv",
    )(proj, proj, proj, proj, mu_r, mu_k, mu_v, mu_sm, w0, a0, k_k, k_a, r_k, gn_w, gn_b,
      w2p, a2p, g2p, *to_bf16)


def _rope_tables(pos_ref, invf_ref):
    ang = pos_ref[...].astype(F32) * invf_ref[...]
    keep = lax.broadcasted_iota(jnp.int32, ang.shape, 1) < QK_ROPE
    return jnp.where(keep, jnp.cos(ang), 0.0), jnp.where(keep, jnp.sin(ang), 0.0)


def _mla_proj_kernel(cq_ref, ckv_ref, sm_ref, pos_ref, invf_ref, gq_ref, gkv_ref,
                     w1_ref, w2_ref, wkv_ref, q_ref, k_ref, v_ref):
    cosm, sinm = _rope_tables(pos_ref, invf_ref)
    hw = QK_NOPE + LANES

    h = _rms(cq_ref[...], gq_ref[...]).astype(BF16)
    z1 = _dot(h, w1_ref[...])
    z2 = _dot(h, w2_ref[...])
    cosq, sinq = cosm * SCORE_SCALE, sinm * SCORE_SCALE
    for hd in range(MLA_HEADS):
        q_ref[:, hw * hd:hw * hd + QK_NOPE] = (z1[:, hw * hd:hw * hd + QK_NOPE] * SCORE_SCALE).astype(BF16)
        pe = z1[:, hw * hd + QK_NOPE:hw * (hd + 1)] * cosq + z2[:, LANES * hd:LANES * (hd + 1)] * sinq
        q_ref[:, hw * hd + QK_NOPE:hw * (hd + 1)] = pe.astype(BF16)

    h = _rms(ckv_ref[...], gkv_ref[...]).astype(BF16)
    z = _dot(h, wkv_ref[...])
    kpe = (sm_ref[:, 2 * LANES:3 * LANES] * cosm + sm_ref[:, 3 * LANES:4 * LANES] * sinm).astype(BF16)
    for hd in range(MLA_HEADS):
        k_ref[:, hw * hd:hw * hd + QK_NOPE] = z[:, QK_NOPE * hd:QK_NOPE * (hd + 1)].astype(BF16)
        k_ref[:, hw * hd + QK_NOPE:hw * (hd + 1)] = kpe
    ones = jnp.ones((V_EXT - V_HEAD, z.shape[0]), BF16)
    for hd in range(MLA_HEADS):
        vcol = MLA_HEADS * QK_NOPE + V_HEAD * hd
        v_ref[V_EXT * hd:V_EXT * hd + V_HEAD, :] = z[:, vcol:vcol + V_HEAD].T.astype(BF16)
        v_ref[V_EXT * hd + V_HEAD:V_EXT * (hd + 1), :] = ones


def _mla_proj(proj, pos, invf, gq, gkv, w1, w2, wkv, tm=512):
    t = proj.shape[0]
    nk = MLA_HEADS * (QK_NOPE + LANES)
    nv = MLA_HEADS * V_EXT
    const = lambda a: pl.BlockSpec(a.shape, lambda i: (0, 0))
    return pl.pallas_call(
        _mla_proj_kernel,
        grid=(t // tm,),
        in_specs=[
            pl.BlockSpec((tm, Q_LORA), lambda i: (i, COL_CQ)),
            pl.BlockSpec((tm, KV_LORA), lambda i: (i, COL_CKV)),
            pl.BlockSpec((tm, SMALL), lambda i: (i, COL_SMALL)),
            pl.BlockSpec((tm, 1), lambda i: (i, 0)),
            const(invf), const(gq), const(gkv), const(w1), const(w2), const(wkv),
        ],
        out_specs=[
            pl.BlockSpec((tm, nk), lambda i: (i, 0)),
            pl.BlockSpec((tm, nk), lambda i: (i, 0)),
            pl.BlockSpec((nv, tm), lambda i: (0, i)),
        ],
        out_shape=[jax.ShapeDtypeStruct((t, nk), BF16), jax.ShapeDtypeStruct((t, nk), BF16),
                   jax.ShapeDtypeStruct((nv, t), BF16)],
        compiler_params=_cparams(("parallel",)),
        name="mla_proj",
    )(proj, proj, proj, pos, invf, gq, gkv, w1, w2, wkv)


ATTN_STRIP = 64


def _attn_kernel(q_ref, k_ref, vt_ref, o_ref, *, blk, nh):
    qi = pl.program_id(2)
    hw = QK_NOPE + LANES
    hds = range(nh)

    def weights(s, m_new):
        return jnp.concatenate(
            [jnp.exp2(s[ATTN_STRIP * r:ATTN_STRIP * (r + 1)] - m_new).astype(BF16)
             for r in range(s.shape[0] // ATTN_STRIP)], axis=0)

    def pipeline(scores, softmax, values):
        m_out, acc_out = [], []
        s_next = scores(0)
        prev = None
        for h in hds:
            s_cur = s_next
            if h + 1 < nh:
                s_next = scores(h + 1)
            m_new, rest = softmax(h, s_cur)
            m_out.append(m_new)
            if prev is not None:
                acc_out.append(values(*prev))
            prev = (h, rest)
        acc_out.append(values(*prev))
        return m_out, acc_out

    def full_step(j, carry):
        m, acc = carry
        start = pl.multiple_of(j * blk, blk)

        def scores(h):
            return _dot_nt(k_ref[pl.ds(start, blk), hw * h:hw * (h + 1)], q_ref[:, hw * h:hw * (h + 1)])

        def softmax(h, s):
            m_new = jnp.maximum(m[h], jnp.max(s, axis=0, keepdims=True))
            return m_new, (jnp.exp2(m[h] - m_new), weights(s, m_new))

        def values(h, rest):
            alpha, p = rest
            return alpha * acc[h] + _dot(vt_ref[V_EXT * h:V_EXT * (h + 1), pl.ds(start, blk)], p)

        return pipeline(scores, softmax, values)

    def diag_step(carry):
        m, acc = carry
        hb = blk // 2
        lo = pl.multiple_of(qi * blk, blk)
        hi = pl.multiple_of(qi * blk + hb, hb)
        key = lax.broadcasted_iota(jnp.int32, (hb, blk), 0)
        qry = lax.broadcasted_iota(jnp.int32, (hb, blk), 1)
        causal_a = key <= qry
        causal_b = causal_a[:, :hb]

        def scores(h):
            hc = slice(hw * h, hw * (h + 1))
            return (_dot_nt(k_ref[pl.ds(lo, hb), hc], q_ref[:, hc]),
                    _dot_nt(k_ref[pl.ds(hi, hb), hc], q_ref[hb:, hc]))

        def softmax(h, s):
            sa = jnp.where(causal_a, s[0], NEG_INF)
            sb = jnp.where(causal_b, s[1], NEG_INF)
            mx = jnp.max(sa, axis=0, keepdims=True)
            mx = jnp.concatenate([mx[:, :hb], jnp.maximum(mx[:, hb:], jnp.max(sb, axis=0, keepdims=True))],
                                 axis=1)
            m_new = jnp.maximum(m[h], mx)
            return m_new, (jnp.exp2(m[h] - m_new), weights(sa, m_new), weights(sb, m_new[:, hb:]))

        def values(h, rest):
            alpha, pa, pb = rest
            vh = slice(V_EXT * h, V_EXT * (h + 1))
            a = alpha * acc[h] + _dot(vt_ref[vh, pl.ds(lo, hb)], pa)
            b = _dot(vt_ref[vh, pl.ds(hi, hb)], pb)
            return jnp.concatenate([a[:, :hb], a[:, hb:] + b], axis=1)

        return pipeline(scores, softmax, values)

    init = ([jnp.full((1, blk), NEG_INF, F32) for _ in hds], [jnp.zeros((V_EXT, blk), F32) for _ in hds])
    carry = lax.fori_loop(0, qi, full_step, init)
    _, acc = diag_step(carry)
    for h in hds:
        out = acc[h][:V_HEAD] / acc[h][V_HEAD:V_HEAD + 1]
        o_ref[:, V_HEAD * h:V_HEAD * (h + 1)] = out.T.astype(o_ref.dtype)


def _attn(q, k, v, bsz, seq, blk=512, nh=4):
    t = q.shape[0]
    nq = seq // blk
    hw = QK_NOPE + LANES
    kern = functools.partial(_attn_kernel, blk=blk, nh=nh)
    return pl.pallas_call(
        kern,
        grid=(bsz, MLA_HEADS // nh, nq),
        in_specs=[
            pl.BlockSpec((blk, hw * nh), lambda b, h, i: (b * nq + i, h)),
            pl.BlockSpec((seq, hw * nh), lambda b, h, i: (b, h)),
            pl.BlockSpec((V_EXT * nh, seq), lambda b, h, i: (h, b)),
        ],
        out_specs=pl.BlockSpec((blk, V_HEAD * nh), lambda b, h, i: (b * nq + i, h)),
        out_shape=jax.ShapeDtypeStruct((t, MLA_HEADS * V_HEAD), BF16),
        compiler_params=_cparams(("parallel", "parallel", "arbitrary")),
        name="attn",
    )(q, k, v)


X_SLOTS = 3


def _outproj_kernel(yr_ref, ym_ref, w_ref, x_hbm, g_ref, x1_ref, h2_ref, xbuf, sem):
    s = pl.program_id(0)
    n = pl.num_programs(0)
    tm = xbuf.shape[1]

    def fetch(blk):
        slot = blk % X_SLOTS
        return pltpu.make_async_copy(x_hbm.at[pl.ds(blk * tm, tm), :], xbuf.at[slot], sem.at[slot])

    @pl.when(s == 0)
    def _():
        fetch(0).start()
        fetch(1).start()

    @pl.when(s + 2 < n)
    def _():
        fetch(s + 2).start()

    fetch(s).wait()
    nr = yr_ref.shape[1]
    x1 = xbuf[s % X_SLOTS] + _dot(yr_ref[...], w_ref[:nr, :]) + _dot(ym_ref[...], w_ref[nr:, :])
    x1_ref[...] = x1
    h2_ref[...] = _rms(x1, g_ref[...]).astype(BF16)


def _outproj(yr, ym, w, x2, g, tm=512):
    t, d = x2.shape
    assert t // tm >= 2, "the ring prefetches two blocks ahead"
    return pl.pallas_call(
        _outproj_kernel,
        grid=(t // tm,),
        in_specs=[
            pl.BlockSpec((tm, RWKV_WIDTH), lambda i: (i, 0)),
            pl.BlockSpec((tm, MLA_WIDTH), lambda i: (i, 0)),
            pl.BlockSpec(w.shape, lambda i: (0, 0), pipeline_mode=pl.Buffered(1)),
            pl.BlockSpec(memory_space=pl.ANY),
            pl.BlockSpec((1, d), lambda i: (0, 0)),
        ],
        out_specs=[pl.BlockSpec((tm, d), lambda i: (i, 0)), pl.BlockSpec((tm, d), lambda i: (i, 0))],
        out_shape=[jax.ShapeDtypeStruct((t, d), F32), jax.ShapeDtypeStruct((t, d), BF16)],
        scratch_shapes=[pltpu.VMEM((X_SLOTS, tm, d), F32), pltpu.SemaphoreType.DMA((X_SLOTS,))],
        compiler_params=_cparams(("arbitrary",)),
        name="outproj",
    )(yr, ym, w, x2, g)


HALO = BF16_ROWS


MXU_N = 256


def _shift_down(x, first):
    r = pltpu.roll(x, 1, 0)
    top = r[0:SUBLANES]
    row = lax.broadcasted_iota(jnp.int32, top.shape, 0)
    return jnp.concatenate([jnp.where(row == 0, first, top), r[SUBLANES:]], axis=0)


def _ffn_act_kernel(h_ref, halo_ref, wg_ref, wu_ref, cw_ref, cb_ref, a_ref, hh_scr, *, tm, seq):
    i = pl.program_id(1)
    keep = jnp.where((i * tm) % seq != 0, 1.0, 0.0)
    hh_scr[0:HALO, :] = (halo_ref[...] * keep.astype(BF16)).astype(BF16)
    hh_scr[HALO:, :] = h_ref[...]
    n_sub = wg_ref.shape[1] // MXU_N

    def gate_dot(s):
        return _dot(hh_scr[...], wg_ref[:, MXU_N * s:MXU_N * (s + 1)])

    def up_dot(s):
        return _dot(h_ref[...], wu_ref[:, MXU_N * s:MXU_N * (s + 1)])

    ge, u = gate_dot(0), up_dot(0)
    for s in range(n_sub):
        cs = slice(MXU_N * s, MXU_N * (s + 1))
        more = s + 1 < n_sub
        ge_next = gate_dot(s + 1) if more else None
        c0, c1, c2 = cw_ref[0:1, cs], cw_ref[1:2, cs], cw_ref[2:3, cs]
        g = ge[HALO:]
        gm1 = ge[HALO - 1:HALO]
        gm2 = ge[HALO - 2:HALO - 1]
        w = c1 * g + _shift_down(c0 * g, c0 * gm1)
        conv = c2 * g + _shift_down(w, c1 * gm1 + c0 * gm2) + cb_ref[:, cs]
        u_next = up_dot(s + 1) if more else None
        a_ref[:, cs] = (conv * _sigmoid(conv) * u).astype(BF16)
        ge, u = ge_next, u_next


def _ffn_act(h2, wg, wu, cw, cb, seq, tm=512, n_col=2):
    t, d = h2.shape
    f = wg.shape[1]
    tf = f // n_col
    kern = functools.partial(_ffn_act_kernel, tm=tm, seq=seq)
    hb = tm // HALO
    once = pl.Buffered(1)
    return pl.pallas_call(
        kern,
        grid=(n_col, t // tm),
        in_specs=[
            pl.BlockSpec((tm, d), lambda j, i: (i, 0)),
            pl.BlockSpec((HALO, d), lambda j, i: (jnp.maximum(i * hb - 1, 0), 0)),
            pl.BlockSpec((d, tf), lambda j, i: (0, j), pipeline_mode=once),
            pl.BlockSpec((d, tf), lambda j, i: (0, j), pipeline_mode=once),
            pl.BlockSpec((3, tf), lambda j, i: (0, j)),
            pl.BlockSpec((1, tf), lambda j, i: (0, j)),
        ],
        out_specs=pl.BlockSpec((tm, tf), lambda j, i: (i, j)),
        out_shape=jax.ShapeDtypeStruct((t, f), BF16),
        scratch_shapes=[pltpu.VMEM((tm + HALO, d), BF16)],
        compiler_params=_cparams(("parallel", "parallel")),
        name="ffn_act",
    )(h2, h2, wg, wu, cw, cb)


def _ffn_down_kernel(a_ref, wd_ref, x1_ref, gf_ref, o_ref):
    o_ref[...] = _rms(x1_ref[...] + _dot(a_ref[...], wd_ref[...]), gf_ref[...])


def _ffn_down(act, wd, x1, gf, tm=512):
    t, d = x1.shape
    f = act.shape[1]
    return pl.pallas_call(
        _ffn_down_kernel,
        grid=(t // tm,),
        in_specs=[
            pl.BlockSpec((tm, f), lambda i: (i, 0)),
            pl.BlockSpec((f, d), lambda i: (0, 0), pipeline_mode=pl.Buffered(1)),
            pl.BlockSpec((tm, d), lambda i: (i, 0)),
            pl.BlockSpec((1, d), lambda i: (0, 0)),
        ],
        out_specs=pl.BlockSpec((tm, d), lambda i: (i, 0)),
        out_shape=jax.ShapeDtypeStruct((t, d), F32),
        compiler_params=_cparams(("parallel",)),
        name="ffn_down",
    )(act, wd, x1, gf)


def _pack_in(w_in, mu):
    d = w_in.shape[0]
    o = 3 * RWKV_WIDTH
    og = o + DECAY_LORA + AAA_LORA
    wt = w_in[:, o:].T
    lo = DECAY_LORA + AAA_LORA
    wl_al = wt[:lo]
    gl = wt[lo:lo + GATE_LORA]
    oq = lo + GATE_LORA
    cq = wt[oq:oq + Q_LORA]
    ckv = wt[oq + Q_LORA:oq + Q_LORA + KV_LORA]
    kpe = wt[oq + Q_LORA + KV_LORA:]
    hr = QK_ROPE // 2
    kpe_rot = jnp.concatenate([-kpe[hr:], kpe[:hr]], axis=0)
    z = lambda n: jnp.zeros((n, d), w_in.dtype)
    rest = GATE_LORA - LANES
    w_a = w_in[:, :o].astype(BF16)
    w_bt = jnp.concatenate(
        [cq, ckv, wl_al, gl[:LANES], kpe, gl[LANES:], z(LANES - QK_ROPE - rest), kpe_rot, z(LANES - QK_ROPE)],
        axis=0).astype(BF16)
    zm = lambda n: jnp.zeros((n,), mu.dtype)
    mu_sm = jnp.concatenate(
        [mu[o:og], mu[og:og + LANES], zm(QK_ROPE), mu[og + LANES:og + GATE_LORA],
         zm(LANES - QK_ROPE - rest), zm(LANES)])
    return w_a, w_bt, mu[:RWKV_WIDTH], mu[RWKV_WIDTH:2 * RWKV_WIDTH], mu[2 * RWKV_WIDTH:o], mu_sm


def _pack_lora(w2, a2, g2):
    n = w2.shape[1]
    z = lambda r: jnp.zeros((r, n), w2.dtype)
    rest = GATE_LORA - LANES
    w2p = jnp.concatenate([w2, z(LANES - DECAY_LORA)], axis=0)
    a2p = jnp.concatenate([z(DECAY_LORA), a2], axis=0)
    g2p = jnp.concatenate([g2[:LANES], z(QK_ROPE), g2[LANES:], z(LANES - QK_ROPE - rest)], axis=0)
    return w2p.astype(BF16), a2p.astype(BF16), g2p.astype(BF16)


def _pack_q(w_uq):
    d = w_uq.shape[0]
    hw = QK_NOPE + QK_ROPE
    hr = QK_ROPE // 2
    w = w_uq.reshape(d, MLA_HEADS, hw)
    pe = w[:, :, QK_NOPE:]
    z = jnp.zeros((d, MLA_HEADS, LANES - QK_ROPE), w_uq.dtype)
    w1 = jnp.concatenate([w[:, :, :QK_NOPE], pe, z], axis=2).reshape(d, -1)
    w2 = jnp.concatenate([-pe[:, :, hr:], pe[:, :, :hr], z], axis=2).reshape(d, -1)
    return w1.astype(BF16), w2.astype(BF16)


def _pack_kv(w_ukv):
    d = w_ukv.shape[0]
    w = w_ukv.reshape(d, MLA_HEADS, QK_NOPE + V_HEAD)
    return jnp.concatenate([w[:, :, :QK_NOPE].reshape(d, -1), w[:, :, QK_NOPE:].reshape(d, -1)],
                           axis=1).astype(BF16)


def kernel(x, positions, attn_norm_g, w_in, rwkv_mu, rwkv_w0, rwkv_w2, rwkv_a0, rwkv_a2, rwkv_g2,
           rwkv_k_k, rwkv_k_a, rwkv_r_k, rwkv_gn_w, rwkv_gn_b, mla_q_norm_g, mla_w_uq,
           mla_kv_norm_g, mla_w_ukv, w_out, ffn_norm_g, ffn_w_gate, ffn_w_up, ffn_conv_w,
           ffn_conv_b, ffn_w_down, final_norm_g):
    bsz, seq, d = x.shape
    t = bsz * seq
    depth = w_in.shape[0]
    row = lambda a: a.reshape(1, -1)
    xc = x.reshape(t, d)
    pos = positions.reshape(t, 1)
    hr = QK_ROPE // 2
    inv_freq = ROPE_THETA ** (-jnp.arange(hr, dtype=F32) / hr)
    invf = jnp.concatenate([inv_freq, inv_freq, jnp.zeros((LANES - QK_ROPE,), F32)]).reshape(1, LANES)

    assert depth == 1, "the final RMSNorm is fused into the (single) layer's ffn kernel"
    l = 0
    w_a, w_b, mu_r, mu_k, mu_v, mu_sm = _pack_in(w_in[l], rwkv_mu[l])
    w2p, a2p, g2p = _pack_lora(rwkv_w2[l], rwkv_a2[l], rwkv_g2[l])
    wq1, wq2 = _pack_q(mla_w_uq[l])
    wkv = _pack_kv(mla_w_ukv[l])

    proj = _inproj(xc, row(attn_norm_g[l]), w_a, w_b)
    y_r, wo, wg, wu, wd = _rwkv(
        proj, bsz, seq, row(mu_r), row(mu_k), row(mu_v), row(mu_sm),
        row(rwkv_w0[l]), row(rwkv_a0[l]), row(rwkv_k_k[l]), row(rwkv_k_a[l]),
        row(rwkv_r_k[l]), row(rwkv_gn_w[l]), row(rwkv_gn_b[l]), w2p, a2p, g2p,
        to_bf16=(w_out[l], ffn_w_gate[l], ffn_w_up[l], ffn_w_down[l]))
    q, k, v = _mla_proj(proj, pos, invf, row(mla_q_norm_g[l]), row(mla_kv_norm_g[l]), wq1, wq2, wkv)
    y_m = _attn(q, k, v, bsz, seq)
    x1, h2 = _outproj(y_r, y_m, wo, xc, row(ffn_norm_g[l]))
    act = _ffn_act(h2, wg, wu, ffn_conv_w[l], row(ffn_conv_b[l]), seq)
    out = _ffn_down(act, wd, x1, row(final_norm_g))
    return out.reshape(bsz, seq, d)
```

```python
import functools
import math

import jax
import jax.numpy as jnp
from jax import lax
from jax.experimental import pallas as pl
from jax.experimental.pallas import tpu as pltpu

F32 = jnp.float32
BF16 = jnp.bfloat16

D_MODEL = 2048
RWKV_HEAD = 64
RWKV_WIDTH = 1024
RWKV_HEADS = 16
DECAY_LORA = 64
AAA_LORA = 64
GATE_LORA = 160
GN_EPS = 64e-5
QK_NOPE = 128
QK_ROPE = 64
V_HEAD = 128
MLA_WIDTH = 1024
MLA_HEADS = 8
Q_LORA = 512
KV_LORA = 512
ROPE_THETA = 10000.0
D_FF = 5632
NORM_EPS = 1e-6
NEG_INF = -1e30

LANES = 128
SUBLANES = 8
BF16_ROWS = 16
SMALL = 512
D_IN_P = 3 * RWKV_WIDTH + Q_LORA + KV_LORA + SMALL
COL_CQ = 3 * RWKV_WIDTH // Q_LORA
COL_CKV = COL_CQ + 1
COL_SMALL = COL_CKV + 1
CHUNK = 64
PAIRS = RWKV_WIDTH // LANES
VMEM_LIMIT = 56 * 1024 * 1024
V_EXT = V_HEAD + BF16_ROWS
SCORE_SCALE = (QK_NOPE + QK_ROPE) ** -0.5 * math.log2(math.e)


def _cparams(sem):
    return pltpu.CompilerParams(dimension_semantics=sem, vmem_limit_bytes=VMEM_LIMIT)


def _rms(x, g):
    ms = jnp.mean(x * x, axis=-1, keepdims=True)
    return x * lax.rsqrt(ms + NORM_EPS) * g


def _sigmoid(x):
    return 1.0 / (1.0 + jnp.exp(-x))


def _dot(a, b):
    return jnp.dot(a, b, preferred_element_type=F32)


def _dot_nt(a, b):
    return lax.dot_general(a, b, (((1,), (1,)), ((), ())), preferred_element_type=F32)


def _dot_tn(a, b):
    return lax.dot_general(a, b, (((0,), (0,)), ((), ())), preferred_element_type=F32)


def _inproj_kernel(x_ref, g_ref, wa_ref, wbt_ref, o_ref):
    h = _rms(x_ref[...], g_ref[...]).astype(BF16)
    na = wa_ref.shape[1]
    o_ref[:, :na] = _dot(h, wa_ref[...])
    o_ref[:, na:] = _dot_nt(h, wbt_ref[...])


def _inproj(x2, g, w_a, w_b, tm=512):
    t, d = x2.shape
    n = w_a.shape[1] + w_b.shape[0]
    once = pl.Buffered(1)
    return pl.pallas_call(
        _inproj_kernel,
        grid=(t // tm,),
        in_specs=[
            pl.BlockSpec((tm, d), lambda i: (i, 0)),
            pl.BlockSpec((1, d), lambda i: (0, 0)),
            pl.BlockSpec(w_a.shape, lambda i: (0, 0), pipeline_mode=once),
            pl.BlockSpec(w_b.shape, lambda i: (0, 0), pipeline_mode=once),
        ],
        out_specs=pl.BlockSpec((tm, n), lambda i: (i, 0)),
        out_shape=jax.ShapeDtypeStruct((t, n), F32),
        compiler_params=_cparams(("parallel",)),
        name="inproj",
    )(x2, g, w_a, w_b)


RWKV_INS = 18


def _rwkv_kernel(*refs, n_cast):
    (r_ref, k_ref, v_ref, sm_ref, mur_ref, muk_ref, muv_ref, musm_ref,
     w0_ref, a0_ref, kk_ref, ka_ref, rk_ref, gnw_ref, gnb_ref,
     w2_ref, a2_ref, g2_ref) = refs[:RWKV_INS]
    cast_in = refs[RWKV_INS:RWKV_INS + n_cast]
    o_ref = refs[RWKV_INS + n_cast]
    cast_out = refs[RWKV_INS + n_cast + 1:RWKV_INS + 2 * n_cast + 1]
    st_scr, pr_scr, pk_scr, pv_scr, psm_scr = refs[RWKV_INS + 2 * n_cast + 1:]
    for src, dst in zip(cast_in, cast_out):
        dst[...] = src[...].astype(dst.dtype)
    c = CHUNK
    half = RWKV_HEAD
    rows = r_ref.shape[0]
    nch = rows // c

    @pl.when(pl.program_id(1) == 0)
    def _():
        st_scr[...] = jnp.zeros_like(st_scr)
        pr_scr[...] = jnp.zeros_like(pr_scr)
        pk_scr[...] = jnp.zeros_like(pk_scr)
        pv_scr[...] = jnp.zeros_like(pv_scr)
        psm_scr[...] = jnp.zeros_like(psm_scr)

    def shift(x_ref, p_scr, mu_ref, ci):
        x = x_ref[c * ci:c * (ci + 1), :]
        prev = p_scr[...] if ci == 0 else x_ref[c * ci - 1:c * ci, :]
        row = lax.broadcasted_iota(jnp.int32, x.shape, 0)
        xp = jnp.where(row == 0, prev, pltpu.roll(x, 1, 0))
        return x + (xp - x) * mu_ref[...]

    ri = lax.broadcasted_iota(jnp.int32, (c, LANES), 0)
    li = lax.broadcasted_iota(jnp.int32, (c, LANES), 1)
    lj = li & (half - 1)
    lane_a = li < half
    strict = lj < ri
    incl = lj <= ri
    eye2 = (lj == ri).astype(F32)
    r2 = lax.broadcasted_iota(jnp.int32, (LANES, LANES), 0)
    l2 = lax.broadcasted_iota(jnp.int32, (LANES, LANES), 1)
    bd = (r2 < half) == (l2 < half)
    bd_ones = bd.astype(BF16)
    bd_ones2 = jnp.concatenate([bd_ones, bd_ones], axis=0)
    tr = lax.broadcasted_iota(jnp.int32, (c, 3 * c), 0)
    tc = lax.broadcasted_iota(jnp.int32, (c, 3 * c), 1)
    tril3 = ((tc & (c - 1)) <= tr).astype(BF16)

    def bd_stack(x):
        xb = x.astype(BF16)
        z = jnp.zeros_like(xb)
        return jnp.concatenate([jnp.where(lane_a, xb, z), jnp.where(lane_a, z, xb)], axis=0)

    def segsum(x):
        xs = jnp.concatenate([x[:, LANES * g:LANES * (g + 1)] for g in range(PAIRS)], axis=0)
        hi = xs.astype(BF16)
        lo = (xs - hi.astype(F32)).astype(BF16)
        s = _dot(jnp.concatenate([hi, lo], axis=1), bd_ones2)
        return jnp.concatenate([s[c * g:c * (g + 1), :] for g in range(PAIRS)], axis=1)

    def cumsum_rows(x):
        hi = x.astype(BF16)
        r1 = x - hi.astype(F32)
        mid = r1.astype(BF16)
        lo = (r1 - mid.astype(F32)).astype(BF16)
        return _dot(tril3, jnp.concatenate([hi, mid, lo], axis=0))

    prs = range(PAIRS)
    sls = [slice(LANES * p, LANES * (p + 1)) for p in prs]
    inv_n = 1.0 / RWKV_HEAD
    state = {"s": [st_scr[p] for p in prs]}
    pre, ind, dep = {}, {}, {}

    def prologue(ci):
        rs = shift(r_ref, pr_scr, mur_ref, ci)
        ks = shift(k_ref, pk_scr, muk_ref, ci)
        vs = shift(v_ref, pv_scr, muv_ref, ci)
        sm = shift(sm_ref, psm_scr, musm_ref, ci)
        g0 = sm[:, 0:LANES]
        zw = _dot(jnp.tanh(g0).astype(BF16), w2_ref[...])
        za = _dot(g0.astype(BF16), a2_ref[...])
        kkr = ks * kk_ref[...]
        nrm = segsum(kkr * kkr)
        gate = _dot(_sigmoid(sm[:, LANES:3 * LANES]).astype(BF16), g2_ref[...])
        yield
        ld = (-math.exp(-0.5)) * _sigmoid(w0_ref[...] + zw)
        lg = cumsum_rows(ld)
        a = _sigmoid(a0_ref[...] + za)
        kk = kkr * lax.rsqrt(jnp.maximum(nrm, 1e-24))
        kp = ks * (1.0 + (a - 1.0) * ka_ref[...])
        beta = kk * a
        bonus = segsum(rs * kp * rk_ref[...])
        yield
        lgc = lg[c - 1:c, :]
        einv = jnp.exp(-lg)
        ec = jnp.exp(lgc - lg)
        rt = rs * jnp.exp(lg)
        at = -kk * jnp.exp(lg - ld)
        pre[ci] = dict(vs=vs, gate=gate, bonus=bonus, gc=jnp.exp(lgc), rt=rt, at=at,
                       bt=beta * einv, kt=kp * einv, bh=beta * ec, kh=kp * ec)
        yield

    def independent(ci):
        q = pre[ci]
        v_p = [q["vs"][:, sl] for sl in sls]
        lhs1 = [jnp.concatenate([q["at"][:, sl], q["rt"][:, sl]], axis=0).astype(BF16) for sl in sls]
        rhs1 = [jnp.concatenate([bd_stack(q["bt"][:, sl]), bd_stack(q["kt"][:, sl])], axis=0) for sl in sls]
        pm = [_dot_nt(lhs1[p], rhs1[p]) for p in prs]
        bk = [jnp.concatenate([q["bh"][:, sl], q["kh"][:, sl]], axis=0).astype(BF16) for sl in sls]
        yield
        lab = [jnp.where(strict, m[0:c, 0:LANES], 0.0) for m in pm]
        lm = [jnp.concatenate([jnp.where(strict, m[0:c, LANES:2 * LANES], 0.0),
                               jnp.where(incl, m[c:2 * c, LANES:2 * LANES], 0.0)], axis=0).astype(BF16)
              for m in pm]
        mrb = [jnp.where(incl, m[c:2 * c, 0:LANES], 0.0).astype(BF16) for m in pm]
        lv = [_dot(lm[p], bd_stack(v_p[p])) for p in prs]
        pw = [_dot(l.astype(BF16), bd_stack(l)) for l in lab]
        tinv = [eye2 + l for l in lab]
        yield
        for _ in range(4):
            res = [_dot(pw[p].astype(BF16),
                        jnp.concatenate([bd_stack(tinv[p]), bd_stack(pw[p])], axis=1)) for p in prs]
            tinv = [tinv[p] + res[p][:, 0:LANES] for p in prs]
            pw = [res[p][:, LANES:2 * LANES] for p in prs]
            yield
        tinv = [(tinv[p] + _dot(pw[p].astype(BF16), bd_stack(tinv[p]))).astype(BF16) for p in prs]
        ind[ci] = dict(v_p=v_p, lhs1=lhs1, bk=bk, mrb=mrb, lv=lv, tinv=tinv)
        yield

    def dependent(ci):
        q, st = ind[ci], state["s"]
        xh = [_dot_nt(q["lhs1"][p], st[p].astype(BF16)) for p in prs]
        yield
        x = [xh[p][0:c] + q["lv"][p][0:c] for p in prs]
        u = [_dot(q["tinv"][p], bd_stack(x[p])) for p in prs]
        yield
        ys = [xh[p][c:2 * c] + q["lv"][p][c:2 * c] + _dot(q["mrb"][p], bd_stack(u[p])) for p in prs]
        uv = [jnp.concatenate([u[p], q["v_p"][p]], axis=0).astype(BF16) for p in prs]
        ds = [_dot_tn(uv[p], q["bk"][p]) for p in prs]
        yield
        gc = pre[ci]["gc"]
        state["s"] = [st[p] * gc[:, sls[p]] + jnp.where(bd, ds[p], 0.0) for p in prs]
        dep[ci] = jnp.concatenate(ys, axis=1)
        yield

    def epilogue(ci):
        y, q = dep[ci], pre[ci]
        mean = segsum(y) * inv_n
        yield
        d = y - mean
        var = segsum(d * d) * inv_n
        yield
        yn = d * lax.rsqrt(var + GN_EPS) * gnw_ref[...] + gnb_ref[...]
        o_ref[c * ci:c * (ci + 1), :] = ((yn + q["bonus"] * q["vs"]) * q["gate"]).astype(o_ref.dtype)
        yield

    phases = (prologue, independent, dependent, epilogue)
    done = object()
    for tick in range(nch + len(phases) - 1):
        live = [ph(tick - k) for k, ph in enumerate(phases) if 0 <= tick - k < nch]
        while live:
            live = [g for g in live if next(g, done) is not done]

    for p in prs:
        st_scr[p] = state["s"][p]
    for x_ref, p_scr in ((r_ref, pr_scr), (k_ref, pk_scr), (v_ref, pv_scr), (sm_ref, psm_scr)):
        p_scr[...] = x_ref[rows - 1:rows, :]


def _rwkv(proj, bsz, seq, mu_r, mu_k, mu_v, mu_sm, w0, a0, k_k, k_a, r_k, gn_w, gn_b, w2p, a2p, g2p,
          to_bf16=(), nch=8):
    t = proj.shape[0]
    w = RWKV_WIDTH
    rows = CHUNK * nch
    nc = seq // rows
    steps = bsz * nc
    row = lambda b, c: b * nc + c
    vec = lambda n: pl.BlockSpec((1, n), lambda b, c: (0, 0))
    full = lambda a: pl.BlockSpec(a.shape, lambda b, c: (0, 0))
    assert all(a.shape[0] % (steps * BF16_ROWS) == 0 for a in to_bf16), "row slabs must be bf16-tile aligned"
    slab = lambda a: pl.BlockSpec((a.shape[0] // steps, a.shape[1]), lambda b, c: (row(b, c), 0))
    return pl.pallas_call(
        functools.partial(_rwkv_kernel, n_cast=len(to_bf16)),
        grid=(bsz, nc),
        in_specs=[
            pl.BlockSpec((rows, w), lambda b, c: (row(b, c), 0)),
            pl.BlockSpec((rows, w), lambda b, c: (row(b, c), 1)),
            pl.BlockSpec((rows, w), lambda b, c: (row(b, c), 2)),
            pl.BlockSpec((rows, SMALL), lambda b, c: (row(b, c), COL_SMALL)),
            vec(w), vec(w), vec(w), vec(SMALL),
            vec(w), vec(w), vec(w), vec(w), vec(w), vec(w), vec(w),
            full(w2p), full(a2p), full(g2p),
        ] + [slab(a) for a in to_bf16],
        out_specs=[pl.BlockSpec((rows, w), lambda b, c: (row(b, c), 0))] + [slab(a) for a in to_bf16],
        out_shape=[jax.ShapeDtypeStruct((t, w), BF16)]
        + [jax.ShapeDtypeStruct(a.shape, BF16) for a in to_bf16],
        scratch_shapes=[
            pltpu.VMEM((PAIRS, LANES, LANES), F32),
            pltpu.VMEM((1, w), F32), pltpu.VMEM((1, w), F32), pltpu.VMEM((1, w), F32),
            pltpu.VMEM((1, SMALL), F32),
        ],
        compiler_params=_cparams(("parallel", "arbitrary")),
        name="rwkv",
    )(proj, proj, proj, proj, mu_r, mu_k, mu_v, mu_sm, w0, a0, k_k, k_a, r_k, gn_w, gn_b,
      w2p, a2p, g2p, *to_bf16)


def _rope_tables(pos_ref, invf_ref):
    ang = pos_ref[...].astype(F32) * invf_ref[...]
    keep = lax.broadcasted_iota(jnp.int32, ang.shape, 1) < QK_ROPE
    return jnp.where(keep, jnp.cos(ang), 0.0), jnp.where(keep, jnp.sin(ang), 0.0)


def _mla_proj_kernel(cq_ref, ckv_ref, sm_ref, pos_ref, invf_ref, gq_ref, gkv_ref,
                     w1_ref, w2_ref, wkv_ref, q_ref, k_ref, v_ref):
    cosm, sinm = _rope_tables(pos_ref, invf_ref)
    hw = QK_NOPE + LANES

    h = _rms(cq_ref[...], gq_ref[...]).astype(BF16)
    z1 = _dot(h, w1_ref[...])
    z2 = _dot(h, w2_ref[...])
    cosq, sinq = cosm * SCORE_SCALE, sinm * SCORE_SCALE
    for hd in range(MLA_HEADS):
        q_ref[:, hw * hd:hw * hd + QK_NOPE] = (z1[:, hw * hd:hw * hd + QK_NOPE] * SCORE_SCALE).astype(BF16)
        pe = z1[:, hw * hd + QK_NOPE:hw * (hd + 1)] * cosq + z2[:, LANES * hd:LANES * (hd + 1)] * sinq
        q_ref[:, hw * hd + QK_NOPE:hw * (hd + 1)] = pe.astype(BF16)

    h = _rms(ckv_ref[...], gkv_ref[...]).astype(BF16)
    z = _dot(h, wkv_ref[...])
    kpe = (sm_ref[:, 2 * LANES:3 * LANES] * cosm + sm_ref[:, 3 * LANES:4 * LANES] * sinm).astype(BF16)
    for hd in range(MLA_HEADS):
        k_ref[:, hw * hd:hw * hd + QK_NOPE] = z[:, QK_NOPE * hd:QK_NOPE * (hd + 1)].astype(BF16)
        k_ref[:, hw * hd + QK_NOPE:hw * (hd + 1)] = kpe
    ones = jnp.ones((V_EXT - V_HEAD, z.shape[0]), BF16)
    for hd in range(MLA_HEADS):
        vcol = MLA_HEADS * QK_NOPE + V_HEAD * hd
        v_ref[V_EXT * hd:V_EXT * hd + V_HEAD, :] = z[:, vcol:vcol + V_HEAD].T.astype(BF16)
        v_ref[V_EXT * hd + V_HEAD:V_EXT * (hd + 1), :] = ones


def _mla_proj(proj, pos, invf, gq, gkv, w1, w2, wkv, tm=512):
    t = proj.shape[0]
    nk = MLA_HEADS * (QK_NOPE + LANES)
    nv = MLA_HEADS * V_EXT
    const = lambda a: pl.BlockSpec(a.shape, lambda i: (0, 0))
    return pl.pallas_call(
        _mla_proj_kernel,
        grid=(t // tm,),
        in_specs=[
            pl.BlockSpec((tm, Q_LORA), lambda i: (i, COL_CQ)),
            pl.BlockSpec((tm, KV_LORA), lambda i: (i, COL_CKV)),
            pl.BlockSpec((tm, SMALL), lambda i: (i, COL_SMALL)),
            pl.BlockSpec((tm, LANES), lambda i: (i, 0)),
            const(invf), const(gq), const(gkv), const(w1), const(w2), const(wkv),
        ],
        out_specs=[
            pl.BlockSpec((tm, nk), lambda i: (i, 0)),
            pl.BlockSpec((tm, nk), lambda i: (i, 0)),
            pl.BlockSpec((nv, tm), lambda i: (0, i)),
        ],
        out_shape=[jax.ShapeDtypeStruct((t, nk), BF16), jax.ShapeDtypeStruct((t, nk), BF16),
                   jax.ShapeDtypeStruct((nv, t), BF16)],
        compiler_params=_cparams(("parallel",)),
        name="mla_proj",
    )(proj, proj, proj, pos, invf, gq, gkv, w1, w2, wkv)


ATTN_STRIP = 64


def _attn_kernel(q_ref, k_ref, vt_ref, o_ref, *, blk, nh):
    qi = pl.program_id(2)
    hw = QK_NOPE + LANES
    hds = range(nh)

    def weights(s, m_new):
        return jnp.concatenate(
            [jnp.exp2(s[ATTN_STRIP * r:ATTN_STRIP * (r + 1)] - m_new).astype(BF16)
             for r in range(s.shape[0] // ATTN_STRIP)], axis=0)

    def pipeline(scores, softmax, values):
        m_out, acc_out = [], []
        s_next = scores(0)
        prev = None
        for h in hds:
            s_cur = s_next
            if h + 1 < nh:
                s_next = scores(h + 1)
            m_new, rest = softmax(h, s_cur)
            m_out.append(m_new)
            if prev is not None:
                acc_out.append(values(*prev))
            prev = (h, rest)
        acc_out.append(values(*prev))
        return m_out, acc_out

    def full_step(j, carry):
        m, acc = carry
        start = pl.multiple_of(j * blk, blk)

        def scores(h):
            return _dot_nt(k_ref[pl.ds(start, blk), hw * h:hw * (h + 1)], q_ref[:, hw * h:hw * (h + 1)])

        def softmax(h, s):
            m_new = jnp.maximum(m[h], jnp.max(s, axis=0, keepdims=True))
            return m_new, (jnp.exp2(m[h] - m_new), weights(s, m_new))

        def values(h, rest):
            alpha, p = rest
            return alpha * acc[h] + _dot(vt_ref[V_EXT * h:V_EXT * (h + 1), pl.ds(start, blk)], p)

        return pipeline(scores, softmax, values)

    def diag_step(carry):
        m, acc = carry
        hb = blk // 2
        lo = pl.multiple_of(qi * blk, blk)
        hi = pl.multiple_of(qi * blk + hb, hb)
        key = lax.broadcasted_iota(jnp.int32, (hb, blk), 0)
        qry = lax.broadcasted_iota(jnp.int32, (hb, blk), 1)
        causal_a = key <= qry
        causal_b = causal_a[:, :hb]

        def scores(h):
            hc = slice(hw * h, hw * (h + 1))
            return (_dot_nt(k_ref[pl.ds(lo, hb), hc], q_ref[:, hc]),
                    _dot_nt(k_ref[pl.ds(hi, hb), hc], q_ref[hb:, hc]))

        def softmax(h, s):
            sa = jnp.where(causal_a, s[0], NEG_INF)
            sb = jnp.where(causal_b, s[1], NEG_INF)
            mx = jnp.max(sa, axis=0, keepdims=True)
            mx = jnp.concatenate([mx[:, :hb], jnp.maximum(mx[:, hb:], jnp.max(sb, axis=0, keepdims=True))],
                                 axis=1)
            m_new = jnp.maximum(m[h], mx)
            return m_new, (jnp.exp2(m[h] - m_new), weights(sa, m_new), weights(sb, m_new[:, hb:]))

        def values(h, rest):
            alpha, pa, pb = rest
            vh = slice(V_EXT * h, V_EXT * (h + 1))
            a = alpha * acc[h] + _dot(vt_ref[vh, pl.ds(lo, hb)], pa)
            b = _dot(vt_ref[vh, pl.ds(hi, hb)], pb)
            return jnp.concatenate([a[:, :hb], a[:, hb:] + b], axis=1)

        return pipeline(scores, softmax, values)

    init = ([jnp.full((1, blk), NEG_INF, F32) for _ in hds], [jnp.zeros((V_EXT, blk), F32) for _ in hds])
    carry = lax.fori_loop(0, qi, full_step, init)
    _, acc = diag_step(carry)
    for h in hds:
        out = acc[h][:V_HEAD] / acc[h][V_HEAD:V_HEAD + 1]
        o_ref[:, V_HEAD * h:V_HEAD * (h + 1)] = out.T.astype(o_ref.dtype)


def _attn(q, k, v, bsz, seq, blk=512, nh=4):
    t = q.shape[0]
    nq = seq // blk
    hw = QK_NOPE + LANES
    kern = functools.partial(_attn_kernel, blk=blk, nh=nh)
    return pl.pallas_call(
        kern,
        grid=(bsz, MLA_HEADS // nh, nq),
        in_specs=[
            pl.BlockSpec((blk, hw * nh), lambda b, h, i: (b * nq + i, h)),
            pl.BlockSpec((seq, hw * nh), lambda b, h, i: (b, h)),
            pl.BlockSpec((V_EXT * nh, seq), lambda b, h, i: (h, b)),
        ],
        out_specs=pl.BlockSpec((blk, V_HEAD * nh), lambda b, h, i: (b * nq + i, h)),
        out_shape=jax.ShapeDtypeStruct((t, MLA_HEADS * V_HEAD), BF16),
        compiler_params=_cparams(("parallel", "parallel", "arbitrary")),
        name="attn",
    )(q, k, v)


def _outproj_kernel(yr_ref, ym_ref, w_ref, x_ref, g_ref, x1_ref, h2_ref):
    nr = yr_ref.shape[1]
    x1 = x_ref[...] + _dot(yr_ref[...], w_ref[:nr, :]) + _dot(ym_ref[...], w_ref[nr:, :])
    x1_ref[...] = x1
    h2_ref[...] = _rms(x1, g_ref[...]).astype(BF16)


def _outproj(yr, ym, w, x2, g, tm=512):
    t, d = x2.shape
    return pl.pallas_call(
        _outproj_kernel,
        grid=(t // tm,),
        in_specs=[
            pl.BlockSpec((tm, RWKV_WIDTH), lambda i: (i, 0)),
            pl.BlockSpec((tm, MLA_WIDTH), lambda i: (i, 0)),
            pl.BlockSpec(w.shape, lambda i: (0, 0), pipeline_mode=pl.Buffered(1)),
            pl.BlockSpec((tm, d), lambda i: (i, 0)),
            pl.BlockSpec((1, d), lambda i: (0, 0)),
        ],
        out_specs=[pl.BlockSpec((tm, d), lambda i: (i, 0)), pl.BlockSpec((tm, d), lambda i: (i, 0))],
        out_shape=[jax.ShapeDtypeStruct((t, d), F32), jax.ShapeDtypeStruct((t, d), BF16)],
        compiler_params=_cparams(("parallel",)),
        name="outproj",
    )(yr, ym, w, x2, g)


HALO = BF16_ROWS


MXU_N = 256


def _shift_down(x, first):
    r = pltpu.roll(x, 1, 0)
    top = r[0:SUBLANES]
    row = lax.broadcasted_iota(jnp.int32, top.shape, 0)
    return jnp.concatenate([jnp.where(row == 0, first, top), r[SUBLANES:]], axis=0)


def _ffn_act_kernel(h_ref, halo_ref, wg_ref, wu_ref, cw_ref, cb_ref, a_ref, hh_scr, *, tm, seq):
    i = pl.program_id(1)
    keep = jnp.where((i * tm) % seq != 0, 1.0, 0.0)
    hh_scr[0:HALO, :] = (halo_ref[...] * keep.astype(BF16)).astype(BF16)
    hh_scr[HALO:, :] = h_ref[...]
    n_sub = wg_ref.shape[1] // MXU_N

    def gate_dot(s):
        return _dot(hh_scr[...], wg_ref[:, MXU_N * s:MXU_N * (s + 1)])

    def up_dot(s):
        return _dot(h_ref[...], wu_ref[:, MXU_N * s:MXU_N * (s + 1)])

    ge, u = gate_dot(0), up_dot(0)
    for s in range(n_sub):
        cs = slice(MXU_N * s, MXU_N * (s + 1))
        more = s + 1 < n_sub
        ge_next = gate_dot(s + 1) if more else None
        c0, c1, c2 = cw_ref[0:1, cs], cw_ref[1:2, cs], cw_ref[2:3, cs]
        g = ge[HALO:]
        gm1 = ge[HALO - 1:HALO]
        gm2 = ge[HALO - 2:HALO - 1]
        w = c1 * g + _shift_down(c0 * g, c0 * gm1)
        conv = c2 * g + _shift_down(w, c1 * gm1 + c0 * gm2) + cb_ref[:, cs]
        u_next = up_dot(s + 1) if more else None
        a_ref[:, cs] = (conv * _sigmoid(conv) * u).astype(BF16)
        ge, u = ge_next, u_next


def _ffn_act(h2, wg, wu, cw, cb, seq, tm=512, n_col=2):
    t, d = h2.shape
    f = wg.shape[1]
    tf = f // n_col
    kern = functools.partial(_ffn_act_kernel, tm=tm, seq=seq)
    hb = tm // HALO
    once = pl.Buffered(1)
    return pl.pallas_call(
        kern,
        grid=(n_col, t // tm),
        in_specs=[
            pl.BlockSpec((tm, d), lambda j, i: (i, 0)),
            pl.BlockSpec((HALO, d), lambda j, i: (jnp.maximum(i * hb - 1, 0), 0)),
            pl.BlockSpec((d, tf), lambda j, i: (0, j), pipeline_mode=once),
            pl.BlockSpec((d, tf), lambda j, i: (0, j), pipeline_mode=once),
            pl.BlockSpec((3, tf), lambda j, i: (0, j)),
            pl.BlockSpec((1, tf), lambda j, i: (0, j)),
        ],
        out_specs=pl.BlockSpec((tm, tf), lambda j, i: (i, j)),
        out_shape=jax.ShapeDtypeStruct((t, f), BF16),
        scratch_shapes=[pltpu.VMEM((tm + HALO, d), BF16)],
        compiler_params=_cparams(("parallel", "parallel")),
        name="ffn_act",
    )(h2, h2, wg, wu, cw, cb)


def _ffn_down_kernel(a_ref, wd_ref, x1_ref, gf_ref, o_ref):
    o_ref[...] = _rms(x1_ref[...] + _dot(a_ref[...], wd_ref[...]), gf_ref[...])


def _ffn_down(act, wd, x1, gf, tm=512):
    t, d = x1.shape
    f = act.shape[1]
    return pl.pallas_call(
        _ffn_down_kernel,
        grid=(t // tm,),
        in_specs=[
            pl.BlockSpec((tm, f), lambda i: (i, 0)),
            pl.BlockSpec((f, d), lambda i: (0, 0), pipeline_mode=pl.Buffered(1)),
            pl.BlockSpec((tm, d), lambda i: (i, 0)),
            pl.BlockSpec((1, d), lambda i: (0, 0)),
        ],
        out_specs=pl.BlockSpec((tm, d), lambda i: (i, 0)),
        out_shape=jax.ShapeDtypeStruct((t, d), F32),
        compiler_params=_cparams(("parallel",)),
        name="ffn_down",
    )(act, wd, x1, gf)


def _pack_in(w_in, mu):
    d = w_in.shape[0]
    o = 3 * RWKV_WIDTH
    og = o + DECAY_LORA + AAA_LORA
    wt = w_in[:, o:].T
    lo = DECAY_LORA + AAA_LORA
    wl_al = wt[:lo]
    gl = wt[lo:lo + GATE_LORA]
    oq = lo + GATE_LORA
    cq = wt[oq:oq + Q_LORA]
    ckv = wt[oq + Q_LORA:oq + Q_LORA + KV_LORA]
    kpe = wt[oq + Q_LORA + KV_LORA:]
    hr = QK_ROPE // 2
    kpe_rot = jnp.concatenate([-kpe[hr:], kpe[:hr]], axis=0)
    z = lambda n: jnp.zeros((n, d), w_in.dtype)
    rest = GATE_LORA - LANES
    w_a = w_in[:, :o].astype(BF16)
    w_bt = jnp.concatenate(
        [cq, ckv, wl_al, gl[:LANES], kpe, gl[LANES:], z(LANES - QK_ROPE - rest), kpe_rot, z(LANES - QK_ROPE)],
        axis=0).astype(BF16)
    zm = lambda n: jnp.zeros((n,), mu.dtype)
    mu_sm = jnp.concatenate(
        [mu[o:og], mu[og:og + LANES], zm(QK_ROPE), mu[og + LANES:og + GATE_LORA],
         zm(LANES - QK_ROPE - rest), zm(LANES)])
    return w_a, w_bt, mu[:RWKV_WIDTH], mu[RWKV_WIDTH:2 * RWKV_WIDTH], mu[2 * RWKV_WIDTH:o], mu_sm


def _pack_lora(w2, a2, g2):
    n = w2.shape[1]
    z = lambda r: jnp.zeros((r, n), w2.dtype)
    rest = GATE_LORA - LANES
    w2p = jnp.concatenate([w2, z(LANES - DECAY_LORA)], axis=0)
    a2p = jnp.concatenate([z(DECAY_LORA), a2], axis=0)
    g2p = jnp.concatenate([g2[:LANES], z(QK_ROPE), g2[LANES:], z(LANES - QK_ROPE - rest)], axis=0)
    return w2p.astype(BF16), a2p.astype(BF16), g2p.astype(BF16)


def _pack_q(w_uq):
    d = w_uq.shape[0]
    hw = QK_NOPE + QK_ROPE
    hr = QK_ROPE // 2
    w = w_uq.reshape(d, MLA_HEADS, hw)
    pe = w[:, :, QK_NOPE:]
    z = jnp.zeros((d, MLA_HEADS, LANES - QK_ROPE), w_uq.dtype)
    w1 = jnp.concatenate([w[:, :, :QK_NOPE], pe, z], axis=2).reshape(d, -1)
    w2 = jnp.concatenate([-pe[:, :, hr:], pe[:, :, :hr], z], axis=2).reshape(d, -1)
    return w1.astype(BF16), w2.astype(BF16)


def _pack_kv(w_ukv):
    d = w_ukv.shape[0]
    w = w_ukv.reshape(d, MLA_HEADS, QK_NOPE + V_HEAD)
    return jnp.concatenate([w[:, :, :QK_NOPE].reshape(d, -1), w[:, :, QK_NOPE:].reshape(d, -1)],
                           axis=1).astype(BF16)


def kernel(x, positions, attn_norm_g, w_in, rwkv_mu, rwkv_w0, rwkv_w2, rwkv_a0, rwkv_a2, rwkv_g2,
           rwkv_k_k, rwkv_k_a, rwkv_r_k, rwkv_gn_w, rwkv_gn_b, mla_q_norm_g, mla_w_uq,
           mla_kv_norm_g, mla_w_ukv, w_out, ffn_norm_g, ffn_w_gate, ffn_w_up, ffn_conv_w,
           ffn_conv_b, ffn_w_down, final_norm_g):
    bsz, seq, d = x.shape
    t = bsz * seq
    depth = w_in.shape[0]
    row = lambda a: a.reshape(1, -1)
    xc = x.reshape(t, d)
    pos = jnp.broadcast_to(positions.reshape(t, 1), (t, LANES))
    hr = QK_ROPE // 2
    inv_freq = ROPE_THETA ** (-jnp.arange(hr, dtype=F32) / hr)
    invf = jnp.concatenate([inv_freq, inv_freq, jnp.zeros((LANES - QK_ROPE,), F32)]).reshape(1, LANES)

    assert depth == 1, "the final RMSNorm is fused into the (single) layer's ffn kernel"
    l = 0
    w_a, w_b, mu_r, mu_k, mu_v, mu_sm = _pack_in(w_in[l], rwkv_mu[l])
    w2p, a2p, g2p = _pack_lora(rwkv_w2[l], rwkv_a2[l], rwkv_g2[l])
    wq1, wq2 = _pack_q(mla_w_uq[l])
    wkv = _pack_kv(mla_w_ukv[l])

    proj = _inproj(xc, row(attn_norm_g[l]), w_a, w_b)
    y_r, wo, wg, wu, wd = _rwkv(
        proj, bsz, seq, row(mu_r), row(mu_k), row(mu_v), row(mu_sm),
        row(rwkv_w0[l]), row(rwkv_a0[l]), row(rwkv_k_k[l]), row(rwkv_k_a[l]),
        row(rwkv_r_k[l]), row(rwkv_gn_w[l]), row(rwkv_gn_b[l]), w2p, a2p, g2p,
        to_bf16=(w_out[l], ffn_w_gate[l], ffn_w_up[l], ffn_w_down[l]))
    q, k, v = _mla_proj(proj, pos, invf, row(mla_q_norm_g[l]), row(mla_kv_norm_g[l]), wq1, wq2, wkv)
    y_m = _attn(q, k, v, bsz, seq)
    x1, h2 = _outproj(y_r, y_m, wo, xc, row(ffn_norm_g[l]))
    act = _ffn_act(h2, wg, wu, ffn_conv_w[l], row(ffn_conv_b[l]), seq)
    out = _ffn_down(act, wd, x1, row(final_norm_g))
    return out.reshape(bsz, seq, d)
```
